```python
import math
import jax, jax.numpy as jnp
from jax import lax
import numpy as np

D_MODEL = 1024
BATCH = 16
SEQ = 2048
DEPTH = 2

RET_HEADS = 4
RET_DK = 128
RET_DV = 128
RET_CHUNK = 128
ROPE_BASE = 10000.0
GDN_HEADS = 4
GDN_DK = 128
GDN_DV = 128
GDN_CHUNK = 64
CONV_K = 4
NSA_HEADS = 8
NSA_GROUPS = 2
NSA_HPG = NSA_HEADS // NSA_GROUPS
NSA_DK = 64
CMP_LEN = 32
CMP_STRIDE = 16
CMP_HIDDEN = 128
SLC_BLOCK = 64
N_SEL = 8
SLC_Q_CHUNK = 64
WIN = 256
WIN_BLOCK = 128
PLE_DIM = 256
ALPHA = (2 * DEPTH) ** 0.25
BETA_INIT = (8 * DEPTH) ** -0.25
NEG = -1e30
BIG = 1e30
EPS = 1e-6

RET_QK = RET_HEADS * RET_DK
RET_V = RET_HEADS * RET_DV
GDN_QKV = GDN_HEADS * (2 * GDN_DK + GDN_DV)
GDN_V = GDN_HEADS * GDN_DV
NSA_Q = NSA_HEADS * NSA_DK
NSA_KV = NSA_GROUPS * NSA_DK
IN_SPLITS = (RET_QK, RET_QK, RET_V, RET_V,
             GDN_QKV, GDN_HEADS, GDN_HEADS, GDN_V,
             NSA_Q, NSA_KV, NSA_KV, NSA_KV, NSA_KV, NSA_KV, NSA_KV, 3 * NSA_HEADS, NSA_Q,
             3 * D_MODEL)
D_IN = sum(IN_SPLITS)

kernel_name = 'hybrid_retention_gdn_nsa_deepnorm'


def _split(h, sizes):
    idx = np.cumsum(np.array(sizes))[:-1].tolist()
    return jnp.split(h, idx, axis=-1)


def _layer_norm(x, g, b):
    xf = x.astype(jnp.float32)
    mu = xf.mean(-1, keepdims=True)
    var = jnp.square(xf - mu).mean(-1, keepdims=True)
    return ((xf - mu) * lax.rsqrt(var + 1e-5) * g.astype(jnp.float32) + b.astype(jnp.float32)).astype(x.dtype)


def _group_norm(o):
    mu = o.mean(-1, keepdims=True)
    var = jnp.square(o - mu).mean(-1, keepdims=True)
    return (o - mu) * lax.rsqrt(var + 1e-5)


def _rms_norm(o, w):
    return o * lax.rsqrt(jnp.square(o).mean(-1, keepdims=True) + EPS) * w.astype(jnp.float32)


def _l2norm(x):
    return x * lax.rsqrt(jnp.sum(jnp.square(x), -1, keepdims=True) + EPS)


def _rotary(x):
    s, d = x.shape[1], x.shape[-1]
    inv = ROPE_BASE ** (-jnp.arange(0, d, 2, dtype=jnp.float32) / d)
    ang = jnp.arange(s, dtype=jnp.float32)[:, None] * inv[None, :]
    cos = jnp.cos(ang)[None, :, None, :]
    sin = jnp.sin(ang)[None, :, None, :]
    x1, x2 = x[..., 0::2], x[..., 1::2]
    return jnp.stack([x1 * cos - x2 * sin, x1 * sin + x2 * cos], axis=-1).reshape(x.shape)


def _retention(q, k, v):
    B, S, H, dk = q.shape
    dv = v.shape[-1]
    C = RET_CHUNK
    n = S // C
    q = _rotary(q.astype(jnp.float32))
    k = _rotary(k.astype(jnp.float32)) * dk ** -0.5
    v = v.astype(jnp.float32)
    log_g = jnp.log1p(-jnp.exp2(-5.0 - jnp.arange(H, dtype=jnp.float32)))
    qc = q.reshape(B, n, C, H, dk)
    kc = k.reshape(B, n, C, H, dk)
    vc = v.reshape(B, n, C, H, dv)
    pos = jnp.arange(C, dtype=jnp.float32)
    diff = pos[:, None] - pos[None, :]
    causal = diff >= 0
    dmask = jnp.where(causal[None], jnp.exp(jnp.where(causal, diff, 0.0)[None] * log_g[:, None, None]), 0.0)
    scores = jnp.einsum('bnihd,bnjhd->bnhij', qc, kc) * dmask
    o_intra = jnp.einsum('bnhij,bnjhe->bnihe', scores, vc)
    k_dec = kc * jnp.exp((C - 1 - pos)[:, None] * log_g[None, :])[None, None, :, :, None]
    kv = jnp.einsum('bnjhd,bnjhe->nbhde', k_dec, vc)
    chunk_decay = jnp.exp(C * log_g)[None, :, None, None]

    def step(R, kv_c):
        return chunk_decay * R + kv_c, R

    _, R_prev = lax.scan(step, jnp.zeros((B, H, dk, dv), jnp.float32), kv)
    q_dec = qc * jnp.exp((pos + 1.0)[:, None] * log_g[None, :])[None, None, :, :, None]
    o_inter = jnp.einsum('bnihd,nbhde->bnihe', q_dec, R_prev)
    return (o_intra + o_inter).reshape(B, S, H, dv)


def _causal_conv(x, w):
    c = x.shape[-1]
    return lax.conv_general_dilated(x, w[:, None, :], window_strides=(1,), padding=((CONV_K - 1, 0),),
                                    dimension_numbers=('NWC', 'WIO', 'NWC'), feature_group_count=c)


def _gated_delta_net(q, k, v, beta, g):
    B, S, H, dk = q.shape
    dv = v.shape[-1]
    C = GDN_CHUNK
    n = S // C
    q = _l2norm(q.astype(jnp.float32)) * dk ** -0.5
    k = _l2norm(k.astype(jnp.float32))
    v = v.astype(jnp.float32)

    def chunks(t):
        return t.reshape(B, n, C, H, -1).transpose(0, 3, 1, 2, 4)

    qc, kc, vc = chunks(q), chunks(k), chunks(v)
    bc = beta.reshape(B, n, C, H).transpose(0, 3, 1, 2)
    gc = jnp.cumsum(g.reshape(B, n, C, H).transpose(0, 3, 1, 2), axis=-1)
    ii = jnp.arange(C)[:, None]
    jj = jnp.arange(C)[None, :]
    incl = ii >= jj
    strict = ii > jj
    diff = gc[..., :, None] - gc[..., None, :]
    decay = jnp.where(incl, jnp.exp(jnp.where(incl, diff, 0.0)), 0.0)
    kk = jnp.einsum('bhnid,bhnjd->bhnij', kc, kc)
    a_mat = jnp.where(strict, bc[..., :, None] * kk * decay, 0.0) + jnp.eye(C, dtype=jnp.float32)
    rhs = jnp.concatenate([vc * bc[..., None], kc * (bc * jnp.exp(gc))[..., None]], axis=-1)
    sol = lax.linalg.triangular_solve(a_mat, rhs, left_side=True, lower=True, unit_diagonal=True)
    u, w = sol[..., :dv], sol[..., dv:]
    attn = jnp.einsum('bhnid,bhnjd->bhnij', qc, kc) * decay
    q_dec = qc * jnp.exp(gc)[..., None]
    g_last = gc[..., -1]
    k_dec = kc * jnp.exp(g_last[..., None] - gc)[..., None]
    xs = tuple(jnp.moveaxis(t, 2, 0) for t in (u, w, q_dec, k_dec, attn, g_last))

    def step(state, inp):
        u_c, w_c, q_c, k_c, a_c, gl = inp
        v_new = u_c - jnp.einsum('bhcd,bhde->bhce', w_c, state)
        o_c = jnp.einsum('bhcd,bhde->bhce', q_c, state) + jnp.einsum('bhij,bhje->bhie', a_c, v_new)
        state = state * jnp.exp(gl)[..., None, None] + jnp.einsum('bhcd,bhce->bhde', k_c, v_new)
        return state, o_c

    _, o = lax.scan(step, jnp.zeros((B, H, dk, dv), jnp.float32), xs)
    return o.transpose(1, 0, 3, 2, 4).reshape(B, S, H, dv)


def _nsa(q, k_c, v_c, k_s, v_s, k_w, v_w, gate_logits, pe_k, pe_v, w1k, w2k, w1v, w2v):
    B, S = q.shape[:2]
    G, HPG, dk = NSA_GROUPS, NSA_HPG, NSA_DK
    scale = dk ** -0.5
    qg = q.reshape(B, S, G, HPG, dk)
    kv4 = lambda t: t.reshape(B, S, G, dk)
    k_c, v_c, k_s, v_s, k_w, v_w = map(kv4, (k_c, v_c, k_s, v_s, k_w, v_w))
    t_pos = jnp.arange(S)

    n_cmp = (S - CMP_LEN) // CMP_STRIDE + 1
    blk_idx = jnp.arange(n_cmp)[:, None] * CMP_STRIDE + jnp.arange(CMP_LEN)[None, :]

    def compress(t, pe, w1, w2):
        blocks = t[:, blk_idx] + pe[None, None, :, None, :]
        flat = blocks.transpose(0, 1, 3, 2, 4).reshape(B, n_cmp, G, CMP_LEN * dk)
        return jax.nn.gelu(flat @ w1) @ w2

    kc_b = compress(k_c, pe_k, w1k, w2k)
    vc_b = compress(v_c, pe_v, w1v, w2v)
    s_cmp = jnp.einsum('bsghd,bcgd->bghsc', qg, kc_b).astype(jnp.float32) * scale
    cmp_end = jnp.arange(n_cmp) * CMP_STRIDE + CMP_LEN - 1
    cmp_mask = cmp_end[None, :] <= t_pos[:, None]
    p_cmp = jnp.where(cmp_mask, jax.nn.softmax(jnp.where(cmp_mask, s_cmp, NEG), axis=-1), 0.0)
    o_cmp = jnp.einsum('bghsc,bcgd->bsghd', p_cmp.astype(vc_b.dtype), vc_b)

    n_slc = S // SLC_BLOCK
    ci = jnp.arange(n_cmp)[:, None] * CMP_STRIDE
    sj = jnp.arange(n_slc)[None, :] * SLC_BLOCK
    overlap = ((ci < sj + SLC_BLOCK) & (ci + CMP_LEN > sj)).astype(jnp.float32)
    imp = jnp.einsum('bghsc,cj->bgsj', p_cmp, overlap)
    cur = t_pos // SLC_BLOCK
    jb = jnp.arange(n_slc)[None, :]
    forced = (jb == 0) | (jb == cur[:, None]) | (jb == cur[:, None] - 1)
    valid = jb <= cur[:, None]
    rank = jnp.where(valid[None, None], jnp.where(forced[None, None], BIG, imp), -BIG)
    n_sel = min(N_SEL, n_slc)
    _, sel = lax.top_k(rank, n_sel)
    ks_blk = k_s.reshape(B, n_slc, SLC_BLOCK, G, dk).transpose(0, 3, 1, 2, 4)
    vs_blk = v_s.reshape(B, n_slc, SLC_BLOCK, G, dk).transpose(0, 3, 1, 2, 4)
    QC = SLC_Q_CHUNK
    n_qc = S // QC
    bi = jnp.arange(B)[:, None, None, None]
    gi = jnp.arange(G)[None, :, None, None]

    def sel_chunk(args):
        q_c, sel_c, t_c = args
        k_g = ks_blk[bi, gi, sel_c]
        v_g = vs_blk[bi, gi, sel_c]
        s = jnp.einsum('bgqhd,bgqnld->bghqnl', q_c, k_g).astype(jnp.float32) * scale
        pos = sel_c[..., None] * SLC_BLOCK + jnp.arange(SLC_BLOCK)
        m = (pos <= t_c[None, None, :, None, None])[:, :, None]
        s = jnp.where(m, s, NEG).reshape(B, G, HPG, QC, n_sel * SLC_BLOCK)
        pr = jax.nn.softmax(s, axis=-1).reshape(B, G, HPG, QC, n_sel, SLC_BLOCK)
        return jnp.einsum('bghqnl,bgqnld->bqghd', pr.astype(v_g.dtype), v_g)

    q_chunks = qg.reshape(B, n_qc, QC, G, HPG, dk).transpose(1, 0, 3, 2, 4, 5)
    sel_chunks = sel.reshape(B, G, n_qc, QC, n_sel).transpose(2, 0, 1, 3, 4)
    t_chunks = t_pos.reshape(n_qc, QC)
    o_slc = lax.map(sel_chunk, (q_chunks, sel_chunks, t_chunks))
    o_slc = o_slc.transpose(1, 0, 2, 3, 4, 5).reshape(B, S, G, HPG, dk)

    WB = WIN_BLOCK
    nb = S // WB
    nw = WIN // WB

    def windows(t):
        tp = jnp.pad(t, ((0, 0), (WIN, 0), (0, 0), (0, 0)))
        tb = tp.reshape(B, nb + nw, WB, G, dk)
        return jnp.concatenate([tb[:, i:i + nb] for i in range(nw + 1)], axis=2)

    kwin = windows(k_w)
    vwin = windows(v_w)
    qb = qg.reshape(B, nb, WB, G, HPG, dk)
    s_w = jnp.einsum('bnqghd,bnkgd->bnghqk', qb, kwin).astype(jnp.float32) * scale
    qpos = jnp.arange(nb)[:, None] * WB + jnp.arange(WB)[None, :]
    kpos = (jnp.arange(nb)[:, None] - nw) * WB + jnp.arange((nw + 1) * WB)[None, :]
    dq = qpos[:, :, None] - kpos[:, None, :]
    wmask = (dq >= 0) & (dq < WIN) & (kpos[:, None, :] >= 0)
    p_w = jax.nn.softmax(jnp.where(wmask[None, :, None, None], s_w, NEG), axis=-1)
    o_win = jnp.einsum('bnghqk,bnkgd->bnqghd', p_w.astype(vwin.dtype), vwin).reshape(B, S, G, HPG, dk)

    gts = jax.nn.sigmoid(gate_logits).reshape(B, S, G, HPG, 3)
    o = gts[..., 0:1] * o_cmp + gts[..., 1:2] * o_slc + gts[..., 2:3] * o_win
    return o.reshape(B, S, NSA_HEADS * dk)


def _layer(x, p_i, w_in, conv_w, a_log, dt_bias, gdn_norm_w, pe_k, pe_v, w1k, w2k, w1v, w2v,
           w_br_a, w_br_b, w_br_c, w_out, ln_g, ln_b, w_ple, w_ple_gate):
    B, S, _ = x.shape
    h = x @ w_in
    (ra_q, ra_k, ra_v, ra_z, gd_qkv, gd_beta, gd_a, gd_z,
     ns_q, ns_kc, ns_vc, ns_ks, ns_vs, ns_kw, ns_vw, ns_gate, ns_z, merge) = _split(h, IN_SPLITS)

    ret = _retention(ra_q.reshape(B, S, RET_HEADS, RET_DK), ra_k.reshape(B, S, RET_HEADS, RET_DK),
                     ra_v.reshape(B, S, RET_HEADS, RET_DV))
    o_a = _group_norm(ret).reshape(B, S, RET_V).astype(x.dtype) * jax.nn.silu(ra_z)

    qkv = jax.nn.silu(_causal_conv(gd_qkv, conv_w))
    gq, gk, gv = _split(qkv, (GDN_HEADS * GDN_DK, GDN_HEADS * GDN_DK, GDN_V))
    beta = jax.nn.sigmoid(gd_beta.astype(jnp.float32))
    g = -jnp.exp(a_log.astype(jnp.float32)) * jax.nn.softplus((gd_a + dt_bias).astype(jnp.float32))
    gdn = _gated_delta_net(gq.reshape(B, S, GDN_HEADS, GDN_DK), gk.reshape(B, S, GDN_HEADS, GDN_DK),
                           gv.reshape(B, S, GDN_HEADS, GDN_DV), beta, g)
    o_b = _rms_norm(gdn, gdn_norm_w).reshape(B, S, GDN_V).astype(x.dtype) * jax.nn.silu(gd_z)

    o_c = _nsa(ns_q, ns_kc, ns_vc, ns_ks, ns_vs, ns_kw, ns_vw, ns_gate,
               pe_k, pe_v, w1k, w2k, w1v, w2v) * jax.nn.silu(ns_z)

    g_a, g_b, g_c = jnp.split(jax.nn.sigmoid(merge), 3, axis=-1)
    mixed = g_a * (o_a @ w_br_a) + g_b * (o_b @ w_br_b) + g_c * (o_c @ w_br_c)
    sub = mixed @ w_out

    ple = jax.nn.sigmoid(x @ w_ple_gate) * (p_i @ w_ple)
    return _layer_norm(ALPHA * x + sub + ple, ln_g, ln_b)


def setup_inputs(seed: int = 0) -> dict:
    key = jax.random.key(seed)
    ks = jax.random.split(key, 24)
    f32 = jnp.float32

    def nrm(k, shape, s):
        return jax.random.normal(k, shape, f32) * s

    x = nrm(ks[0], (BATCH, SEQ, D_MODEL), 1.0)
    p = nrm(ks[1], (DEPTH, BATCH, SEQ, PLE_DIM), 1.0)
    w_in = nrm(ks[2], (DEPTH, D_MODEL, D_IN), D_MODEL ** -0.5)
    gdn_conv_w = nrm(ks[3], (DEPTH, CONV_K, GDN_QKV), CONV_K ** -0.5)
    gdn_a_log = jnp.log(jax.random.uniform(ks[4], (DEPTH, GDN_HEADS), f32, 1.0, 16.0))
    dt = jnp.exp(jax.random.uniform(ks[5], (DEPTH, GDN_HEADS), f32, math.log(1e-3), math.log(1e-1)))
    gdn_dt_bias = dt + jnp.log(-jnp.expm1(-dt))
    gdn_norm_w = 1.0 + nrm(ks[6], (DEPTH, GDN_DV), 0.02)
    nsa_pe_k = nrm(ks[7], (DEPTH, CMP_LEN, NSA_DK), 0.02)
    nsa_pe_v = nrm(ks[8], (DEPTH, CMP_LEN, NSA_DK), 0.02)
    nsa_cmp_w1k = nrm(ks[9], (DEPTH, CMP_LEN * NSA_DK, CMP_HIDDEN), (CMP_LEN * NSA_DK) ** -0.5)
    nsa_cmp_w2k = nrm(ks[10], (DEPTH, CMP_HIDDEN, NSA_DK), CMP_HIDDEN ** -0.5)
    nsa_cmp_w1v = nrm(ks[11], (DEPTH, CMP_LEN * NSA_DK, CMP_HIDDEN), (CMP_LEN * NSA_DK) ** -0.5)
    nsa_cmp_w2v = nrm(ks[12], (DEPTH, CMP_HIDDEN, NSA_DK), CMP_HIDDEN ** -0.5)
    w_branch_a = nrm(ks[13], (DEPTH, RET_V, D_MODEL), RET_V ** -0.5)
    w_branch_b = nrm(ks[14], (DEPTH, GDN_V, D_MODEL), GDN_V ** -0.5)
    w_branch_c = nrm(ks[15], (DEPTH, NSA_Q, D_MODEL), NSA_Q ** -0.5)
    w_out = nrm(ks[16], (DEPTH, D_MODEL, D_MODEL), D_MODEL ** -0.5 * BETA_INIT)
    ln_g = 1.0 + nrm(ks[17], (DEPTH, D_MODEL), 0.02)
    ln_b = nrm(ks[18], (DEPTH, D_MODEL), 0.02)
    w_ple = nrm(ks[19], (DEPTH, PLE_DIM, D_MODEL), PLE_DIM ** -0.5)
    w_ple_gate = nrm(ks[20], (DEPTH, D_MODEL, D_MODEL), D_MODEL ** -0.5)
    return {'x': x, 'p': p, 'w_in': w_in, 'gdn_conv_w': gdn_conv_w, 'gdn_a_log': gdn_a_log,
            'gdn_dt_bias': gdn_dt_bias, 'gdn_norm_w': gdn_norm_w, 'nsa_pe_k': nsa_pe_k, 'nsa_pe_v': nsa_pe_v,
            'nsa_cmp_w1k': nsa_cmp_w1k, 'nsa_cmp_w2k': nsa_cmp_w2k, 'nsa_cmp_w1v': nsa_cmp_w1v,
            'nsa_cmp_w2v': nsa_cmp_w2v, 'w_branch_a': w_branch_a, 'w_branch_b': w_branch_b,
            'w_branch_c': w_branch_c, 'w_out': w_out, 'ln_g': ln_g, 'ln_b': ln_b, 'w_ple': w_ple,
            'w_ple_gate': w_ple_gate}


def reference(x, p, w_in, gdn_conv_w, gdn_a_log, gdn_dt_bias, gdn_norm_w, nsa_pe_k, nsa_pe_v,
              nsa_cmp_w1k, nsa_cmp_w2k, nsa_cmp_w1v, nsa_cmp_w2v, w_branch_a, w_branch_b, w_branch_c,
              w_out, ln_g, ln_b, w_ple, w_ple_gate):
    for i in range(DEPTH):
        x = _layer(x, p[i], w_in[i], gdn_conv_w[i], gdn_a_log[i], gdn_dt_bias[i], gdn_norm_w[i],
                   nsa_pe_k[i], nsa_pe_v[i], nsa_cmp_w1k[i], nsa_cmp_w2k[i], nsa_cmp_w1v[i], nsa_cmp_w2v[i],
                   w_branch_a[i], w_branch_b[i], w_branch_c[i], w_out[i], ln_g[i], ln_b[i],
                   w_ple[i], w_ple_gate[i])
    return x
```

```python
import functools
import math

import numpy as np
import jax
import jax.numpy as jnp
from jax import lax
from jax.experimental import pallas as pl
from jax.experimental.pallas import tpu as pltpu

F32 = jnp.float32
BF16 = jnp.bfloat16

RET_HEADS, RET_D, RET_CHUNK = 4, 128, 128
ROPE_BASE = 10000.0
GDN_HEADS, GDN_D, GDN_CHUNK, CONV_K = 4, 128, 64, 4
NSA_HEADS, NSA_GROUPS, NSA_DK = 8, 2, 64
NSA_HPG = NSA_HEADS // NSA_GROUPS
CMP_LEN, CMP_STRIDE, CMP_HIDDEN = 32, 16, 128
SLC_BLOCK, N_SEL, WIN = 64, 8, 256
NEG, BIG, EPS = -1e30, 1e30, 1e-6
LANES = 128
Q_TILE = 128
VMEM_LIMIT = 56 * 1024 * 1024

RET_W = RET_HEADS * RET_D
GDN_W = GDN_HEADS * GDN_D
NSA_QW = NSA_HEADS * NSA_DK
NSA_KW = NSA_GROUPS * NSA_DK


def _bf(x):
    return x.astype(BF16)


def _mm(a, b):
    return jnp.dot(_bf(a), _bf(b), preferred_element_type=F32)


def _mm_nt(a, b):
    return lax.dot_general(_bf(a), _bf(b), (((1,), (1,)), ((), ())), preferred_element_type=F32)


def _split2(x):
    hi = _bf(x)
    lo = _bf(x - hi.astype(F32))
    return hi, lo


def _mm3(a, b):
    ah, al = _split2(a)
    bh, bl = _split2(b)
    return (jnp.dot(ah, bh, preferred_element_type=F32) + jnp.dot(ah, bl, preferred_element_type=F32)
            + jnp.dot(al, bh, preferred_element_type=F32))


def _sigmoid(x):
    return 1.0 / (1.0 + jnp.exp(-x))


def _silu(x):
    return x * _sigmoid(x)


def _softplus(x):
    return jnp.maximum(x, 0.0) + jnp.log(1.0 + jnp.exp(-jnp.abs(x)))


def _params(*sem):
    return pltpu.CompilerParams(dimension_semantics=sem, vmem_limit_bytes=VMEM_LIMIT)


def _proj_kernel(x_ref, w_ref, *o_refs, widths):
    x = _bf(x_ref[...])
    off = 0
    for o_ref, wd in zip(o_refs, widths):
        o_ref[...] = jnp.dot(x, w_ref[:, off:off + wd], preferred_element_type=F32).astype(o_ref.dtype)
        off += wd


def _proj(x2d, w, widths, tm=512, name="proj"):
    m, k = x2d.shape
    n = w.shape[1]
    assert sum(widths) == n and m % tm == 0
    return pl.pallas_call(
        functools.partial(_proj_kernel, widths=tuple(widths)),
        out_shape=[jax.ShapeDtypeStruct((m, wd), F32) for wd in widths],
        grid=(m // tm,),
        in_specs=[pl.BlockSpec((tm, k), lambda i: (i, 0)), pl.BlockSpec((k, n), lambda i: (0, 0))],
        out_specs=[pl.BlockSpec((tm, wd), lambda i: (i, 0)) for wd in widths],
        compiler_params=_params("parallel"),
        name=name,
    )(x2d, w)


def _proj_gdn_kernel(x_ref, w_ref, wt_ref, qkv_ref, z_ref, sm_ref, smt_ref):
    x = _bf(x_ref[...])
    nq = qkv_ref.shape[1]
    nz = z_ref.shape[1]
    qkv_ref[...] = jnp.dot(x, w_ref[:, :nq], preferred_element_type=F32)
    z_ref[...] = jnp.dot(x, w_ref[:, nq:nq + nz], preferred_element_type=F32)
    sm_ref[...] = jnp.dot(x, w_ref[:, nq + nz:], preferred_element_type=F32)
    smt_ref[...] = lax.dot_general(wt_ref[...], x, (((1,), (1,)), ((), ())), preferred_element_type=F32)


def _proj_gdn(x2d, w, wt, tm=512):
    m, k = x2d.shape
    n = w.shape[1]
    nq, nz = 3 * GDN_W, GDN_W
    return pl.pallas_call(
        _proj_gdn_kernel,
        out_shape=[jax.ShapeDtypeStruct((m, nq), F32), jax.ShapeDtypeStruct((m, nz), F32),
                   jax.ShapeDtypeStruct((m, LANES), F32), jax.ShapeDtypeStruct((8, m), F32)],
        grid=(m // tm,),
        in_specs=[pl.BlockSpec((tm, k), lambda i: (i, 0)), pl.BlockSpec((k, n), lambda i: (0, 0)),
                  pl.BlockSpec((8, k), lambda i: (0, 0))],
        out_specs=[pl.BlockSpec((tm, nq), lambda i: (i, 0)), pl.BlockSpec((tm, nz), lambda i: (i, 0)),
                   pl.BlockSpec((tm, LANES), lambda i: (i, 0)), pl.BlockSpec((8, tm), lambda i: (0, i))],
        compiler_params=_params("parallel"),
        name="proj_gdn",
    )(x2d, w, wt)


def _ret_kernel(q_ref, k_ref, v_ref, z_ref, cos_ref, sin_ref, dmask_ref, qdec_ref, kdec_ref, cdec_ref, o_ref):
    s_len = q_ref.shape[0]
    c = RET_CHUNK
    half = RET_D // 2
    scale = RET_D ** -0.5
    dmask = dmask_ref[...]
    qdec = qdec_ref[...]
    kdec = kdec_ref[...]
    cdec = cdec_ref[...]

    def body(n, state):
        sl = pl.ds(pl.multiple_of(n * c, c), c)
        cos = cos_ref[sl, :]
        sin = sin_ref[sl, :]
        q = q_ref[sl, :]
        k = k_ref[sl, :]
        v = v_ref[sl, :]
        q = q * cos + pltpu.roll(q, half, 1) * sin
        k = (k * cos + pltpu.roll(k, half, 1) * sin) * scale
        scores = _mm_nt(q, k) * dmask
        o = _mm(scores, v) + _mm(q * qdec, state)
        state = cdec * state + _mm((k * kdec).T, v)
        mu = jnp.mean(o, axis=-1, keepdims=True)
        d = o - mu
        var = jnp.mean(d * d, axis=-1, keepdims=True)
        o_ref[sl, :] = d * lax.rsqrt(var + 1e-5) * _silu(z_ref[sl, :])
        return state

    lax.fori_loop(0, s_len // c, body, jnp.zeros((RET_D, RET_D), F32))


def _retention(hq, hk, hv, hz, tabs):
    b, s, _ = hq.shape
    cos, sin, dmask, qdec, kdec, cdec = tabs
    seq = lambda: pl.BlockSpec((None, s, RET_D), lambda i, h: (i, 0, h))
    tab = lambda: pl.BlockSpec((None, RET_CHUNK, RET_D), lambda i, h: (h, 0, 0))
    rot = lambda: pl.BlockSpec((s, RET_D), lambda i, h: (0, 0))
    return pl.pallas_call(
        _ret_kernel,
        out_shape=jax.ShapeDtypeStruct((b, s, RET_W), F32),
        grid=(b, RET_HEADS),
        in_specs=[seq(), seq(), seq(), seq(), rot(), rot(), tab(), tab(), tab(), tab()],
        out_specs=seq(),
        compiler_params=_params("parallel", "parallel"),
        name="retention",
    )(hq, hk, hv, hz, cos, sin, dmask, qdec, kdec, cdec)


def _ret_tables(s):
    inv = ROPE_BASE ** (-jnp.arange(0, RET_D, 2, dtype=F32) / RET_D)
    ang = jnp.arange(s, dtype=F32)[:, None] * inv[None, :]
    cos = jnp.concatenate([jnp.cos(ang), jnp.cos(ang)], axis=-1)
    sin = jnp.concatenate([-jnp.sin(ang), jnp.sin(ang)], axis=-1)
    c = RET_CHUNK
    log_g = jnp.log1p(-jnp.exp2(-5.0 - jnp.arange(RET_HEADS, dtype=F32)))
    pos = jnp.arange(c, dtype=F32)
    diff = pos[:, None] - pos[None, :]
    causal = diff >= 0
    dmask = jnp.where(causal[None], jnp.exp(jnp.where(causal, diff, 0.0)[None] * log_g[:, None, None]), 0.0)
    ones = jnp.ones((1, 1, RET_D), F32)
    kdec = jnp.exp((c - 1 - pos)[None, :, None] * log_g[:, None, None]) * ones
    qdec = jnp.exp((pos + 1.0)[None, :, None] * log_g[:, None, None]) * ones
    cdec = jnp.exp(c * log_g)[:, None, None] * jnp.ones((1, c, RET_D), F32)
    return cos, sin, dmask, qdec, kdec, cdec


def _gdn_kernel(q_ref, k_ref, v_ref, z_ref, sm_ref, smt_ref, cwq_ref, cwk_ref, cwv_ref, alog_ref, dtb_ref, nw_ref,
                o_ref):
    s_len = q_ref.shape[0]
    c = GDN_CHUNK
    h = pl.program_id(1)
    ii = lax.broadcasted_iota(jnp.int32, (c, c), 0)
    jj = lax.broadcasted_iota(jnp.int32, (c, c), 1)
    incl = ii >= jj
    strict = ii > jj
    lane = lax.broadcasted_iota(jnp.int32, (c, LANES), 1)
    neg_a = -jnp.exp(alog_ref[...])
    dtb = dtb_ref[...]
    nw = nw_ref[...]

    def conv_silu(x_ref, w_ref, n):
        start = pl.multiple_of(n * c, c)
        cur = x_ref[pl.ds(start, c), :]
        prev = x_ref[pl.ds(pl.multiple_of(jnp.maximum(start - 8, 0), 8), 8), :]
        prev = jnp.where(n > 0, prev, 0.0)
        ext = jnp.concatenate([prev, cur], axis=0)
        w = w_ref[...]
        y = cur * w[CONV_K - 1:CONV_K, :]
        for j in range(1, CONV_K):
            y = y + ext[8 - j:8 - j + c, :] * w[CONV_K - 1 - j:CONV_K - j, :]
        return _silu(y)

    def body(n, state):
        sl = pl.ds(pl.multiple_of(n * c, c), c)
        q = conv_silu(q_ref, cwq_ref, n)
        k = conv_silu(k_ref, cwk_ref, n)
        v = conv_silu(v_ref, cwv_ref, n)
        q = q * lax.rsqrt(jnp.sum(q * q, axis=-1, keepdims=True) + EPS) * (GDN_D ** -0.5)
        k = k * lax.rsqrt(jnp.sum(k * k, axis=-1, keepdims=True) + EPS)
        sm = sm_ref[sl, :]
        beta_col = jnp.sum(jnp.where(lane == h, _sigmoid(sm), 0.0), axis=-1, keepdims=True)
        g_all = neg_a * _softplus(sm + dtb)
        g_col = jnp.sum(jnp.where(lane == GDN_HEADS + h, g_all, 0.0), axis=-1, keepdims=True)
        smt = smt_ref[n]
        a_row = smt[1:2, :]
        na_h = jnp.sum(jnp.where(lane[0:1, :] == GDN_HEADS + h, neg_a, 0.0), axis=-1, keepdims=True)
        db_h = jnp.sum(jnp.where(lane[0:1, :] == GDN_HEADS + h, dtb, 0.0), axis=-1, keepdims=True)
        g_row = na_h * _softplus(a_row + db_h)
        gc_col = jnp.sum(jnp.where(jj <= ii, jnp.broadcast_to(g_row, (c, c)), 0.0), axis=1, keepdims=True)
        gc_row = jnp.sum(jnp.where(ii <= jj, jnp.broadcast_to(g_col, (c, c)), 0.0), axis=0, keepdims=True)
        g_last = jnp.sum(g_row, axis=1, keepdims=True)
        diff = gc_col - gc_row
        decay = jnp.where(incl, jnp.exp(jnp.where(incl, diff, 0.0)), 0.0)
        kk = _mm_nt(k, k)
        mneg = -jnp.where(strict, beta_col * kk * decay, 0.0)
        egc = jnp.exp(gc_col)
        rhs = jnp.concatenate([v * beta_col, k * (beta_col * egc)], axis=1)
        r = rhs + _mm3(mneg, rhs)
        mp = mneg
        for _ in range(5):
            mp = _mm3(mp, mp)
            r = r + _mm3(mp, r)
        u = r[:, :GDN_D]
        w = r[:, GDN_D:]
        attn = _mm_nt(q, k) * decay
        q_dec = q * egc
        k_dec = k * jnp.exp(g_last - gc_col)
        v_new = u - _mm(w, state)
        o = _mm(q_dec, state) + _mm(attn, v_new)
        state = state * jnp.exp(g_last) + _mm(k_dec.T, v_new)
        o = o * lax.rsqrt(jnp.mean(o * o, axis=-1, keepdims=True) + EPS) * nw
        o_ref[sl, :] = o * _silu(z_ref[sl, :])
        return state

    lax.fori_loop(0, s_len // c, body, jnp.zeros((GDN_D, GDN_D), F32))


def _gdn(qkv, z, sm, smt, conv_w, alog_row, dtb_row, norm_w):
    b, s, _ = qkv.shape
    nh = GDN_HEADS
    seq = lambda off: pl.BlockSpec((None, s, GDN_D), lambda i, h, off=off: (i, 0, off + h))
    cw = lambda off: pl.BlockSpec((CONV_K, GDN_D), lambda i, h, off=off: (0, off + h))
    row = lambda: pl.BlockSpec((1, LANES), lambda i, h: (0, 0))
    return pl.pallas_call(
        _gdn_kernel,
        out_shape=jax.ShapeDtypeStruct((b, s, GDN_W), F32),
        grid=(b, nh),
        in_specs=[seq(0), seq(nh), seq(2 * nh), seq(0),
                  pl.BlockSpec((None, s, LANES), lambda i, h: (i, 0, 0)),
                  pl.BlockSpec((None, None, s // GDN_CHUNK, 8, GDN_CHUNK), lambda i, h: (i, h, 0, 0, 0)),
                  cw(0), cw(nh), cw(2 * nh), row(), row(), row()],
        out_specs=seq(0),
        compiler_params=_params("parallel", "parallel"),
        name="gated_delta_net",
    )(qkv, qkv, qkv, z, sm, smt, conv_w, conv_w, conv_w, alog_row, dtb_row, norm_w)


def _cmp_kernel(yk_ref, yv_ref, pek_ref, pev_ref, w1k_ref, w1v_ref, w2k_ref, w2v_ref, kc_ref, vc_ref):
    nrow = yk_ref.shape[0]

    def one(y_ref, pe_ref, w1_ref, w2_ref, o_ref):
        y = y_ref[...]
        first = _mm(y + pe_ref[0:1, :], w1_ref[0])
        second = _mm(y + pe_ref[1:2, :], w1_ref[1])
        pre = first + pltpu.roll(second, nrow - 1, 0)
        hcur = jax.nn.gelu(pre)
        outs = [_mm(hcur[:, g * CMP_HIDDEN:(g + 1) * CMP_HIDDEN], w2_ref[...]) for g in range(NSA_GROUPS)]
        o_ref[...] = jnp.concatenate(outs, axis=1)

    one(yk_ref, pek_ref, w1k_ref, w2k_ref, kc_ref)
    one(yv_ref, pev_ref, w1v_ref, w2v_ref, vc_ref)


def _compress(yk, yv, pek, pev, w1k, w1v, w2k, w2v):
    b, nrow, width = yk.shape
    hid = NSA_GROUPS * CMP_HIDDEN
    full = lambda shape: pl.BlockSpec(shape, lambda i, n=len(shape): (0,) * n)
    return pl.pallas_call(
        _cmp_kernel,
        out_shape=[jax.ShapeDtypeStruct((b, nrow, NSA_KW), F32)] * 2,
        grid=(b,),
        in_specs=[pl.BlockSpec((None, nrow, width), lambda i: (i, 0, 0)),
                  pl.BlockSpec((None, nrow, width), lambda i: (i, 0, 0)),
                  full((2, width)), full((2, width)), full((2, width, hid)), full((2, width, hid)),
                  full((CMP_HIDDEN, NSA_DK)), full((CMP_HIDDEN, NSA_DK))],
        out_specs=[pl.BlockSpec((None, nrow, NSA_KW), lambda i: (i, 0, 0))] * 2,
        compiler_params=_params("parallel"),
        name="nsa_compress",
    )(yk, yv, pek, pev, w1k, w1v, w2k, w2v)


def _softmax_rows(s):
    m = jnp.max(s, axis=-1, keepdims=True)
    e = jnp.exp(s - m)
    return e / jnp.sum(e, axis=-1, keepdims=True)


def _nsa_kernel(q_ref, z_ref, gl_ref, kc_ref, vc_ref, ks_ref, vs_ref, kw_ref, vw_ref, ovl_ref, exp_ref, o_ref,
                mask_s):
    tq = q_ref.shape[0]
    s_len = ks_ref.shape[0]
    ncmp = kc_ref.shape[0]
    nslc = s_len // SLC_BLOCK
    qt = pl.program_id(1)
    t0 = qt * tq
    scale = NSA_DK ** -0.5
    hp = NSA_HPG
    rows = hp * tq
    lane = lax.broadcasted_iota(jnp.int32, (tq, LANES), 1)
    low = lane < NSA_DK
    t_col = t0 + lax.broadcasted_iota(jnp.int32, (tq, 1), 0)
    t_row = t0 + lax.broadcasted_iota(jnp.int32, (1, tq), 1)
    gates = _sigmoid(gl_ref[...])
    stack = lambda a: jnp.concatenate([a] * hp, axis=0)

    for g in range(NSA_GROUPS):
        qs = []
        for hh in range(hp):
            head = g * hp + hh
            blk = q_ref[:, (head // 2) * LANES:(head // 2 + 1) * LANES]
            if head % 2 != g:
                blk = pltpu.roll(blk, NSA_DK, 1)
            qs.append(jnp.where(low if g == 0 else ~low, blk, 0.0))
        q4 = _bf(jnp.concatenate(qs, axis=0) * scale)

        c_end = lax.broadcasted_iota(jnp.int32, (1, ncmp), 1) * CMP_STRIDE + (CMP_LEN - 1)
        cmask = c_end <= t_col
        cmask4 = stack(cmask)
        s_c = lax.dot_general(q4, _bf(kc_ref[...]), (((1,), (1,)), ((), ())), preferred_element_type=F32)
        p_c = jnp.where(cmask4, _softmax_rows(jnp.where(cmask4, s_c, NEG)), 0.0)
        o_cmp = _mm(p_c, vc_ref[...])

        p_sum = p_c[0:tq]
        for hh in range(1, hp):
            p_sum = p_sum + p_c[hh * tq:(hh + 1) * tq]
        ph, pl_ = _split2(p_sum)
        pm = _bf(p_sum - ph.astype(F32) - pl_.astype(F32))
        ovl = ovl_ref[...]
        nt = lambda a: lax.dot_general(ovl, a, (((1,), (1,)), ((), ())), preferred_element_type=F32)
        imp = nt(ph) + nt(pl_) + nt(pm)
        jb = lax.broadcasted_iota(jnp.int32, (nslc, tq), 0)
        cur = t_row // SLC_BLOCK
        forced = (jb == 0) | (jb == cur) | (jb == cur - 1)
        rank = jnp.where(jb <= cur, jnp.where(forced, BIG, imp), -BIG)
        cnt = jnp.zeros((nslc, tq), F32)
        for i in range(nslc):
            ri = rank[i:i + 1, :]
            beats = (ri > rank) | ((ri == rank) & (jb > i))
            cnt = cnt + jnp.where(beats, 1.0, 0.0)
        sel_t = jnp.where(cnt < float(min(N_SEL, nslc)), 1.0, 0.0)
        sel_t = jnp.concatenate([sel_t, jnp.zeros((LANES - nslc, tq), F32)], axis=0)
        sel = _bf(sel_t.T)
        for kt in range(s_len // LANES):
            pos = kt * LANES + lane
            mk = jnp.dot(sel, exp_ref[:, kt * LANES:(kt + 1) * LANES], preferred_element_type=F32)
            mask_s[kt] = jnp.where((mk > 0.5) & (pos <= t_col), 0.0, NEG)

        def slc_body(kt, carry):
            m_run, l_run, acc = carry
            ksl = pl.ds(pl.multiple_of(kt * LANES, LANES), LANES)
            s_s = lax.dot_general(q4, _bf(ks_ref[ksl, :]), (((1,), (1,)), ((), ())), preferred_element_type=F32)
            s_s = s_s + stack(mask_s[kt])
            m_new = jnp.maximum(m_run, jnp.max(s_s, axis=-1, keepdims=True))
            alpha = jnp.exp(m_run - m_new)
            p = jnp.exp(s_s - m_new)
            l_run = alpha * l_run + jnp.sum(p, axis=-1, keepdims=True)
            acc = alpha * acc + _mm(p, vs_ref[ksl, :])
            return m_new, l_run, acc

        init = (jnp.full((rows, 1), NEG, F32), jnp.zeros((rows, 1), F32), jnp.zeros((rows, LANES), F32))
        _, l_s, acc_s = lax.fori_loop(0, qt + 1, slc_body, init)
        o_slc = acc_s / l_s

        nwin = WIN + tq
        w0 = pl.multiple_of(jnp.maximum(t0 - WIN, 0), LANES)
        kpos = w0 + lax.broadcasted_iota(jnp.int32, (1, nwin), 1)
        dq = t_col - kpos
        wmask4 = stack((dq >= 0) & (dq < WIN))
        s_w = lax.dot_general(q4, _bf(kw_ref[pl.ds(w0, nwin), :]), (((1,), (1,)), ((), ())),
                              preferred_element_type=F32)
        p_w = _softmax_rows(jnp.where(wmask4, s_w, NEG))
        o_win = _mm(p_w, vw_ref[pl.ds(w0, nwin), :])

        outs = []
        for hh in range(hp):
            head = g * hp + hh
            rs = slice(hh * tq, (hh + 1) * tq)
            col = 8 + 3 * head
            o_h = (gates[:, col:col + 1] * o_cmp[rs] + gates[:, col + 1:col + 2] * o_slc[rs]
                   + gates[:, col + 2:col + 3] * o_win[rs])
            outs.append(o_h)
        for pair in range(hp // 2):
            a, b2 = outs[2 * pair], outs[2 * pair + 1]
            if g == 0:
                blk = jnp.where(low, a, pltpu.roll(b2, NSA_DK, 1))
            else:
                blk = jnp.where(low, pltpu.roll(a, NSA_DK, 1), b2)
            cs = slice((g * hp // 2 + pair) * LANES, (g * hp // 2 + pair + 1) * LANES)
            o_ref[:, cs] = blk * _silu(z_ref[:, cs])


def _nsa_attention(q, z, gl, kcb, vcb, ks, vs, kw, vw, ovl_t, expand):
    b, s, _ = q.shape
    tq = Q_TILE
    ncmp = kcb.shape[1]
    tile = lambda wd: pl.BlockSpec((None, tq, wd), lambda i, t: (i, t, 0))
    seq = lambda n: pl.BlockSpec((None, n, NSA_KW), lambda i, t: (i, 0, 0))
    full = lambda shape: pl.BlockSpec(shape, lambda i, t: (0, 0))
    return pl.pallas_call(
        _nsa_kernel,
        out_shape=jax.ShapeDtypeStruct((b, s, NSA_QW), F32),
        grid=(b, s // tq),
        in_specs=[tile(NSA_QW), tile(NSA_QW), tile(LANES), seq(ncmp), seq(ncmp), seq(s), seq(s), seq(s), seq(s),
                  full(ovl_t.shape), full(expand.shape)],
        out_specs=tile(NSA_QW),
        scratch_shapes=[pltpu.VMEM((s // LANES, tq, LANES), F32)],
        compiler_params=_params("parallel", "arbitrary"),
        name="nsa_attention",
    )(q, z, gl, kcb, vcb, ks, vs, kw, vw, ovl_t, expand)


def _nsa_tables(s):
    ncmp_rows = s // CMP_STRIDE
    nslc = s // SLC_BLOCK
    ci = np.arange(ncmp_rows)[None, :] * CMP_STRIDE
    sj = np.arange(nslc)[:, None] * SLC_BLOCK
    ovl_t = ((ci < sj + SLC_BLOCK) & (ci + CMP_LEN > sj)).astype(np.float32)
    expand = (np.arange(LANES)[:, None] == (np.arange(s)[None, :] // SLC_BLOCK)).astype(np.float32)
    return jnp.asarray(ovl_t, BF16), jnp.asarray(expand, BF16)


def _merge_kernel(x_ref, p_ref, oa_ref, ob_ref, oc_ref, wm_ref, wa_ref, wb_ref, wc_ref, wo_ref, wpg_ref, wp_ref,
                  lg_ref, lb_ref, o_ref, *, alpha):
    d = x_ref.shape[1]
    x = x_ref[...]
    xb = _bf(x)
    mixed = None
    for idx, (br_ref, w_ref) in enumerate(((oa_ref, wa_ref), (ob_ref, wb_ref), (oc_ref, wc_ref))):
        gate = _sigmoid(jnp.dot(xb, wm_ref[:, idx * d:(idx + 1) * d], preferred_element_type=F32))
        term = gate * _mm(br_ref[...], w_ref[...])
        mixed = term if mixed is None else mixed + term
    sub = _mm(mixed, wo_ref[...])
    ple = _sigmoid(jnp.dot(xb, wpg_ref[...], preferred_element_type=F32)) * _mm(p_ref[...], wp_ref[...])
    y = alpha * x + sub + ple
    mu = jnp.mean(y, axis=-1, keepdims=True)
    yc = y - mu
    var = jnp.mean(yc * yc, axis=-1, keepdims=True)
    o_ref[...] = yc * lax.rsqrt(var + 1e-5) * lg_ref[...] + lb_ref[...]


def _merge(x2d, p2d, oa, ob, oc, wm, wa, wb, wc, wo, wpg, wp, lg, lb, alpha, tm=256):
    m, d = x2d.shape
    tile = lambda wd: pl.BlockSpec((tm, wd), lambda i: (i, 0))
    full = lambda a: pl.BlockSpec(a.shape, lambda i: (0, 0))
    weights = (wm, wa, wb, wc, wo, wpg, wp, lg, lb)
    return pl.pallas_call(
        functools.partial(_merge_kernel, alpha=alpha),
        out_shape=jax.ShapeDtypeStruct((m, d), F32),
        grid=(m // tm,),
        in_specs=[tile(d), tile(p2d.shape[1]), tile(oa.shape[1]), tile(ob.shape[1]), tile(oc.shape[1])]
        + [full(a) for a in weights],
        out_specs=tile(d),
        compiler_params=_params("parallel"),
        name="merge_out_norm",
    )(x2d, p2d, oa, ob, oc, *weights)


def _layer_weights(w_in, conv_w, a_log, dt_bias, norm_w, pe_k, pe_v, w1k, w2k, w1v, w2v, d_model):
    sizes = (RET_W, RET_W, RET_W, RET_W, 3 * GDN_W, GDN_HEADS, GDN_HEADS, GDN_W,
             NSA_QW, NSA_KW, NSA_KW, NSA_KW, NSA_KW, NSA_KW, NSA_KW, 3 * NSA_HEADS, NSA_QW, 3 * d_model)
    offs = np.concatenate([[0], np.cumsum(sizes)])
    col = lambda i: w_in[:, offs[i]:offs[i + 1]]
    perm = np.concatenate([hd * RET_D + np.concatenate([np.arange(0, RET_D, 2), np.arange(1, RET_D, 2)])
                           for hd in range(RET_HEADS)])
    w_ret = _bf(jnp.concatenate([col(0)[:, perm], col(1)[:, perm], col(2), col(3)], axis=1))
    small = jnp.concatenate([col(5), col(6), col(15)], axis=1)
    small = jnp.pad(small, ((0, 0), (0, LANES - small.shape[1])))
    w_gdn = _bf(jnp.concatenate([col(4), col(7), small], axis=1))
    w_gdn_t = _bf(small[:, :8].T)
    w_nsa = _bf(jnp.concatenate([col(8), col(16), col(9), col(10), col(11), col(12), col(13), col(14)], axis=1))
    w_merge = _bf(col(17))
    lane_ids = jnp.arange(LANES)
    pick = lambda vec: jnp.where((lane_ids >= GDN_HEADS) & (lane_ids < 2 * GDN_HEADS),
                                 jnp.pad(vec, (GDN_HEADS, LANES - 2 * GDN_HEADS)), 0.0)[None, :]
    alog_row = pick(a_log.astype(F32))
    dtb_row = pick(dt_bias.astype(F32))
    half = CMP_LEN // 2

    def big_w1(w1):
        w = w1.reshape(2, half, NSA_DK, CMP_HIDDEN)
        out = jnp.zeros((2, half, NSA_GROUPS, NSA_DK, NSA_GROUPS, CMP_HIDDEN), F32)
        for g in range(NSA_GROUPS):
            out = out.at[:, :, g, :, g, :].set(w)
        return _bf(out.reshape(2, half * NSA_GROUPS * NSA_DK, NSA_GROUPS * CMP_HIDDEN))

    def pe_rows(pe):
        t = jnp.broadcast_to(pe.reshape(2, half, 1, NSA_DK), (2, half, NSA_GROUPS, NSA_DK))
        return t.reshape(2, half * NSA_GROUPS * NSA_DK).astype(F32)

    return dict(w_ret=w_ret, w_gdn=w_gdn, w_gdn_t=w_gdn_t, w_nsa=w_nsa, w_merge=w_merge, alog_row=alog_row,
                dtb_row=dtb_row, norm_w=norm_w.astype(F32)[None, :], conv_w=conv_w.astype(F32),
                pek=pe_rows(pe_k), pev=pe_rows(pe_v), w1k=big_w1(w1k), w1v=big_w1(w1v), w2k=_bf(w2k), w2v=_bf(w2v))


def _layer(x, p_i, lw, w_br_a, w_br_b, w_br_c, w_out, ln_g, ln_b, w_ple, w_ple_gate, ret_tabs, nsa_tabs, alpha):
    b, s, d = x.shape
    m = b * s
    x2d = x.reshape(m, d)
    r3 = lambda a: a.reshape(b, s, a.shape[-1])

    hq, hk, hv, hz = _proj(x2d, lw["w_ret"], (RET_W,) * 4, name="proj_ret")
    o_a = _retention(r3(hq), r3(hk), r3(hv), r3(hz), ret_tabs)

    qkv, gz, sm, smt = _proj_gdn(x2d, lw["w_gdn"], lw["w_gdn_t"])
    nchunk = s // GDN_CHUNK
    smt = smt.reshape(2, GDN_HEADS, b, nchunk, GDN_CHUNK).transpose(2, 1, 3, 0, 4)
    smt = jnp.pad(smt, ((0, 0), (0, 0), (0, 0), (0, 6), (0, 0)))
    o_b = _gdn(r3(qkv), r3(gz), r3(sm), smt, lw["conv_w"], lw["alog_row"], lw["dtb_row"], lw["norm_w"])

    nq, nz, kc, vc, ks, vs, kw, vw = _proj(x2d, lw["w_nsa"], (NSA_QW, NSA_QW) + (NSA_KW,) * 6, name="proj_nsa")
    slab = lambda a: a.reshape(b, s // CMP_STRIDE, CMP_STRIDE * NSA_KW)
    kcb, vcb = _compress(slab(kc), slab(vc), lw["pek"], lw["pev"], lw["w1k"], lw["w1v"], lw["w2k"], lw["w2v"])
    o_c = _nsa_attention(r3(nq), r3(nz), r3(sm), kcb, vcb, r3(ks), r3(vs), r3(kw), r3(vw), *nsa_tabs)

    out = _merge(x2d, p_i.reshape(m, -1), o_a.reshape(m, -1), o_b.reshape(m, -1), o_c.reshape(m, -1),
                 lw["w_merge"], _bf(w_br_a), _bf(w_br_b), _bf(w_br_c), _bf(w_out), _bf(w_ple_gate), _bf(w_ple),
                 ln_g.astype(F32)[None, :], ln_b.astype(F32)[None, :], alpha)
    return out.reshape(b, s, d)


def kernel(x, p, w_in, gdn_conv_w, gdn_a_log, gdn_dt_bias, gdn_norm_w, nsa_pe_k, nsa_pe_v, nsa_cmp_w1k, nsa_cmp_w2k, nsa_cmp_w1v, nsa_cmp_w2v, w_branch_a, w_branch_b, w_branch_c, w_out, ln_g, ln_b, w_ple, w_ple_gate):
    depth = w_in.shape[0]
    s, d = x.shape[1], x.shape[2]
    alpha = (2 * depth) ** 0.25
    ret_tabs = _ret_tables(s)
    nsa_tabs = _nsa_tables(s)
    for i in range(depth):
        lw = _layer_weights(w_in[i], gdn_conv_w[i], gdn_a_log[i], gdn_dt_bias[i], gdn_norm_w[i], nsa_pe_k[i],
                            nsa_pe_v[i], nsa_cmp_w1k[i], nsa_cmp_w2k[i], nsa_cmp_w1v[i], nsa_cmp_w2v[i], d)
        x = _layer(x, p[i], lw, w_branch_a[i], w_branch_b[i], w_branch_c[i], w_out[i], ln_g[i], ln_b[i],
                   w_ple[i], w_ple_gate[i], ret_tabs, nsa_tabs, alpha)
    return x
```

```python
import functools

import numpy as np
import jax
import jax.numpy as jnp
from jax import lax
from jax.experimental import pallas as pl
from jax.experimental.pallas import tpu as pltpu

F32 = jnp.float32
BF16 = jnp.bfloat16

RET_HEADS, RET_D, RET_CHUNK = 4, 128, 128
ROPE_BASE = 10000.0
GDN_HEADS, GDN_D, GDN_CHUNK, CONV_K = 4, 128, 64, 4
NSA_HEADS, NSA_GROUPS, NSA_DK = 8, 2, 64
NSA_HPG = NSA_HEADS // NSA_GROUPS
CMP_LEN, CMP_STRIDE, CMP_HIDDEN = 32, 16, 128
SLC_BLOCK, N_SEL, WIN = 64, 8, 256
NEG, BIG, EPS = -1e30, 1e30, 1e-6
LANES = 128
Q_TILE = 128
GDN_UNROLL = 2
VMEM_LIMIT = 56 * 1024 * 1024

RET_W = RET_HEADS * RET_D
GDN_W = GDN_HEADS * GDN_D
NSA_QW = NSA_HEADS * NSA_DK
NSA_KW = NSA_GROUPS * NSA_DK
NSA_GATE_ROWS = 32
NSA_T_ROWS = NSA_QW + 2 * NSA_KW + NSA_GATE_ROWS


def _bf(x):
    return x.astype(BF16)


def _mm(a, b):
    return jnp.dot(_bf(a), _bf(b), preferred_element_type=F32)


def _mm_nt(a, b):
    return lax.dot_general(_bf(a), _bf(b), (((1,), (1,)), ((), ())), preferred_element_type=F32)


def _split3(x):
    hi = _bf(x)
    r1 = x - hi.astype(F32)
    mid = _bf(r1)
    lo = _bf(r1 - mid.astype(F32))
    return hi, mid, lo


def _sigmoid(x):
    return 1.0 / (1.0 + jnp.exp(-x))


def _silu(x):
    return x * _sigmoid(x)


def _softplus(x):
    return jnp.maximum(x, 0.0) + jnp.log(1.0 + jnp.exp(-jnp.abs(x)))


def _params(*sem):
    return pltpu.CompilerParams(dimension_semantics=sem, vmem_limit_bytes=VMEM_LIMIT)


def _proj_kernel(x_ref, w_ref, *o_refs, widths):
    x = _bf(x_ref[...])
    off = 0
    for o_ref, wd in zip(o_refs, widths):
        o_ref[...] = jnp.dot(x, w_ref[:, off:off + wd], preferred_element_type=F32).astype(o_ref.dtype)
        off += wd


def _proj(x2d, w, widths, tm=512, name="proj"):
    m, k = x2d.shape
    n = w.shape[1]
    assert sum(widths) == n and m % tm == 0
    return pl.pallas_call(
        functools.partial(_proj_kernel, widths=tuple(widths)),
        out_shape=[jax.ShapeDtypeStruct((m, wd), F32) for wd in widths],
        grid=(m // tm,),
        in_specs=[pl.BlockSpec((tm, k), lambda i: (i, 0)), pl.BlockSpec((k, n), lambda i: (0, 0))],
        out_specs=[pl.BlockSpec((tm, wd), lambda i: (i, 0)) for wd in widths],
        compiler_params=_params("parallel"),
        name=name,
    )(x2d, w)


def _proj_t_kernel(x_ref, w_ref, wt_ref, *o_refs, widths):
    x = _bf(x_ref[...])
    off = 0
    for o_ref, wd in zip(o_refs[:-1], widths):
        o_ref[...] = jnp.dot(x, w_ref[:, off:off + wd], preferred_element_type=F32)
        off += wd
    o_refs[-1][...] = lax.dot_general(wt_ref[...], x, (((1,), (1,)), ((), ())), preferred_element_type=F32)


def _proj_t(x2d, w, wt, widths, tm=512, name="proj_t"):
    m, k = x2d.shape
    n = w.shape[1]
    nt = wt.shape[0]
    assert sum(widths) == n and m % tm == 0
    return pl.pallas_call(
        functools.partial(_proj_t_kernel, widths=tuple(widths)),
        out_shape=[jax.ShapeDtypeStruct((m, wd), F32) for wd in widths] + [jax.ShapeDtypeStruct((nt, m), F32)],
        grid=(m // tm,),
        in_specs=[pl.BlockSpec((tm, k), lambda i: (i, 0)), pl.BlockSpec((k, n), lambda i: (0, 0)),
                  pl.BlockSpec((nt, k), lambda i: (0, 0))],
        out_specs=[pl.BlockSpec((tm, wd), lambda i: (i, 0)) for wd in widths]
        + [pl.BlockSpec((nt, tm), lambda i: (0, i))],
        compiler_params=_params("parallel"),
        name=name,
    )(x2d, w, wt)


def _ret_kernel(q_ref, k_ref, v_ref, z_ref, cos_ref, sin_ref, dmask_ref, qdec_ref, kdec_ref, cdec_ref, o_ref,
                state_s):
    ts = q_ref.shape[0]
    c = RET_CHUNK
    half = RET_D // 2
    scale = RET_D ** -0.5

    @pl.when(pl.program_id(1) == 0)
    def _():
        state_s[...] = jnp.zeros_like(state_s)

    def body(n, carry):
        sl = pl.ds(pl.multiple_of(n * c, c), c)
        cos = cos_ref[sl, :]
        sin = sin_ref[sl, :]
        for h in range(RET_HEADS):
            hs = slice(h * RET_D, (h + 1) * RET_D)
            q = q_ref[sl, hs]
            k = k_ref[sl, hs]
            v = v_ref[sl, hs]
            state = state_s[h]
            q = q * cos + pltpu.roll(q, half, 1) * sin
            k = (k * cos + pltpu.roll(k, half, 1) * sin) * scale
            scores = _mm_nt(q, k) * dmask_ref[h]
            o = _mm(scores, v) + _mm(q * qdec_ref[h], state)
            state_s[h] = cdec_ref[h] * state + _mm((k * kdec_ref[h]).T, v)
            mu = jnp.mean(o, axis=-1, keepdims=True)
            d = o - mu
            var = jnp.mean(d * d, axis=-1, keepdims=True)
            o_ref[sl, hs] = d * lax.rsqrt(var + 1e-5) * _silu(z_ref[sl, hs])
        return carry

    lax.fori_loop(0, ts // c, body, 0)


def _retention(hq, hk, hv, hz, tabs, ts=512):
    b, s, _ = hq.shape
    cos, sin, dmask, qdec, kdec, cdec = tabs
    seq = lambda: pl.BlockSpec((None, ts, RET_W), lambda i, t: (i, t, 0))
    tab = lambda: pl.BlockSpec((RET_HEADS, RET_CHUNK, RET_D), lambda i, t: (0, 0, 0))
    rot = lambda: pl.BlockSpec((ts, RET_D), lambda i, t: (t, 0))
    return pl.pallas_call(
        _ret_kernel,
        out_shape=jax.ShapeDtypeStruct((b, s, RET_W), F32),
        grid=(b, s // ts),
        in_specs=[seq(), seq(), seq(), seq(), rot(), rot(), tab(), tab(), tab(), tab()],
        out_specs=seq(),
        scratch_shapes=[pltpu.VMEM((RET_HEADS, RET_D, RET_D), F32)],
        compiler_params=_params("parallel", "arbitrary"),
        name="retention",
    )(hq, hk, hv, hz, cos, sin, dmask, qdec, kdec, cdec)


def _ret_tables(s):
    inv = ROPE_BASE ** (-jnp.arange(0, RET_D, 2, dtype=F32) / RET_D)
    ang = jnp.arange(s, dtype=F32)[:, None] * inv[None, :]
    cos = jnp.concatenate([jnp.cos(ang), jnp.cos(ang)], axis=-1)
    sin = jnp.concatenate([-jnp.sin(ang), jnp.sin(ang)], axis=-1)
    c = RET_CHUNK
    log_g = jnp.log1p(-jnp.exp2(-5.0 - jnp.arange(RET_HEADS, dtype=F32)))
    pos = jnp.arange(c, dtype=F32)
    diff = pos[:, None] - pos[None, :]
    causal = diff >= 0
    dmask = jnp.where(causal[None], jnp.exp(jnp.where(causal, diff, 0.0)[None] * log_g[:, None, None]), 0.0)
    ones = jnp.ones((1, 1, RET_D), F32)
    kdec = jnp.exp((c - 1 - pos)[None, :, None] * log_g[:, None, None]) * ones
    qdec = jnp.exp((pos + 1.0)[None, :, None] * log_g[:, None, None]) * ones
    cdec = jnp.exp(c * log_g)[:, None, None] * jnp.ones((1, c, RET_D), F32)
    return cos, sin, dmask, qdec, kdec, cdec


def _gdn_kernel(qkv_ref, z_ref, sm_ref, smt_ref, cw_ref, alog_ref, dtb_ref, alog_l_ref, dtb_l_ref, nw_ref, o_ref,
                state_s, xs_s, lhs_s, ou_s, psi_s, dec_s):
    ts = qkv_ref.shape[0]
    c, d, nh = GDN_CHUNK, GDN_D, GDN_HEADS
    hc = nh * c
    nct = ts // c
    t = pl.program_id(1)
    ri = lax.broadcasted_iota(jnp.int32, (hc, hc), 0)
    ci = lax.broadcasted_iota(jnp.int32, (hc, hc), 1)
    same = (ri // c) == (ci // c)
    incl = same & (ri >= ci)
    strict = same & (ri > ci)
    upper = same & (ri <= ci)
    neg_a = -jnp.exp(alog_ref[...])
    dtb = dtb_ref[...]
    neg_a_l = -jnp.exp(alog_l_ref[...])
    dtb_l = dtb_l_ref[...]
    nw = nw_ref[...]

    @pl.when(t == 0)
    def _():
        state_s[...] = jnp.zeros_like(state_s)
        xs_s[0:8, :] = jnp.zeros((8, xs_s.shape[1]), F32)

    xs_s[8:8 + ts, :] = qkv_ref[...]

    def conv_silu_stack(n, part):
        start = pl.multiple_of(n * c, c)
        cols = slice(part * nh * d, (part + 1) * nh * d)
        w = cw_ref[:, cols]
        ext = xs_s[pl.ds(start, c + 8), cols]
        y = ext[8:8 + c, :] * w[CONV_K - 1:CONV_K, :]
        for j in range(1, CONV_K):
            y = y + ext[8 - j:8 - j + c, :] * w[CONV_K - 1 - j:CONV_K - j, :]
        y = _silu(y)
        return [y[:, h * d:(h + 1) * d] for h in range(nh)]

    def l2n(xs, mult):
        out = [x * (lax.rsqrt(jnp.sum(x * x, axis=-1, keepdims=True) + EPS) * mult) for x in xs]
        return jnp.concatenate(out, axis=0)

    def prep(it, carry):
        chunks = [it * GDN_UNROLL + u for u in range(GDN_UNROLL)]
        st = []
        for n in chunks:
            sl = pl.ds(pl.multiple_of(n * c, c), c)
            q = l2n(conv_silu_stack(n, 0), d ** -0.5)
            k = l2n(conv_silu_stack(n, 1), 1.0)
            v = jnp.concatenate(conv_silu_stack(n, 2), axis=0)
            sm = sm_ref[sl, :]
            beta_col = jnp.concatenate([_sigmoid(sm[:, h:h + 1]) for h in range(nh)], axis=0)
            g_col = jnp.concatenate(
                [neg_a[:, nh + h:nh + h + 1] * _softplus(sm[:, nh + h:nh + h + 1] + dtb[:, nh + h:nh + h + 1])
                 for h in range(nh)], axis=0)
            smt = smt_ref[t * nct + n]
            g_row = neg_a_l * _softplus(smt[1:2, :] + dtb_l)
            g_row_b = jnp.broadcast_to(g_row, (hc, hc))
            gc_col = jnp.sum(jnp.where(incl, g_row_b, 0.0), axis=1, keepdims=True)
            gc_row = jnp.sum(jnp.where(upper, jnp.broadcast_to(g_col, (hc, hc)), 0.0), axis=0, keepdims=True)
            g_last = jnp.sum(jnp.where(same, g_row_b, 0.0), axis=1, keepdims=True)
            decay = jnp.where(incl, jnp.exp(jnp.where(incl, gc_col - gc_row, 0.0)), 0.0)
            egc = jnp.exp(gc_col)
            st.append(dict(q=q, k=k, v=v, beta=beta_col, gc_col=gc_col, g_last=g_last, decay=decay, egc=egc))
        for s_ in st:
            qk_kk = _mm_nt(jnp.concatenate([s_["q"], s_["k"]], axis=0), s_["k"])
            s_["attn"] = qk_kk[:hc] * s_["decay"]
            s_["mp"] = _bf(-jnp.where(strict, s_["beta"] * qk_kk[hc:] * s_["decay"], 0.0))
            s_["r"] = jnp.concatenate([s_["v"] * s_["beta"], s_["k"] * (s_["beta"] * s_["egc"])], axis=1)
        for s_ in st:
            s_["r"] = s_["r"] + jnp.dot(s_["mp"], _bf(s_["r"]), preferred_element_type=F32)
        for _ in range(5):
            for s_ in st:
                s_["mp"] = _bf(jnp.dot(s_["mp"], s_["mp"], preferred_element_type=F32))
            for s_ in st:
                s_["r"] = s_["r"] + jnp.dot(s_["mp"], _bf(s_["r"]), preferred_element_type=F32)
        for n, s_ in zip(chunks, st):
            rb = _bf(s_["r"])
            ar = jnp.dot(_bf(s_["attn"]), rb, preferred_element_type=F32)
            k_dec = s_["k"] * jnp.exp(s_["g_last"] - s_["gc_col"])
            qt = s_["q"] * s_["egc"] - ar[:, d:]
            for h in range(nh):
                rows = slice(h * c, (h + 1) * c)
                kr = jnp.dot(_bf(k_dec[rows].T), rb[rows], preferred_element_type=F32)
                lhs_s[h, n, 0:c, :] = _bf(qt[rows])
                lhs_s[h, n, c:c + d, :] = _bf(-kr[:, d:])
                ou_s[h, n] = ar[rows, :d]
                psi_s[h, n] = kr[:, :d]
                dec_s[h, n] = jnp.broadcast_to(jnp.exp(s_["g_last"][h * c:h * c + 1, :]), (8, d))
        return carry

    lax.fori_loop(0, nct // GDN_UNROLL, prep, 0)

    def scan(n, carry):
        sl = pl.ds(pl.multiple_of(n * c, c), c)
        for h in range(nh):
            hs = slice(h * d, (h + 1) * d)
            state = state_s[h]
            res = jnp.dot(lhs_s[h, n], _bf(state), preferred_element_type=F32)
            o = res[:c] + ou_s[h, n]
            state_s[h] = dec_s[h, n][0:1, :] * state + res[c:] + psi_s[h, n]
            o = o * lax.rsqrt(jnp.mean(o * o, axis=-1, keepdims=True) + EPS) * nw
            o_ref[sl, hs] = o * _silu(z_ref[sl, hs])
        return carry

    lax.fori_loop(0, nct, scan, 0)
    xs_s[0:8, :] = qkv_ref[ts - 8:ts, :]


def _gdn(qkv, z, sm, smt, conv_w, alog_row, dtb_row, alog_lanes, dtb_lanes, norm_w, ts=512):
    b, s, wq = qkv.shape
    nh, c, d = GDN_HEADS, GDN_CHUNK, GDN_D
    nct = ts // c
    assert nct % GDN_UNROLL == 0
    tile = lambda wd: pl.BlockSpec((None, ts, wd), lambda i, t: (i, t, 0))
    full = lambda a: pl.BlockSpec(a.shape, lambda i, t, n=a.ndim: (0,) * n)
    return pl.pallas_call(
        _gdn_kernel,
        out_shape=jax.ShapeDtypeStruct((b, s, GDN_W), F32),
        grid=(b, s // ts),
        in_specs=[tile(wq), tile(GDN_W), tile(LANES),
                  pl.BlockSpec((None, s // c, 8, nh * c), lambda i, t: (i, 0, 0, 0)),
                  full(conv_w), full(alog_row), full(dtb_row), full(alog_lanes), full(dtb_lanes), full(norm_w)],
        out_specs=tile(GDN_W),
        scratch_shapes=[pltpu.VMEM((nh, d, d), F32),
                        pltpu.VMEM((ts + 8, wq), F32),
                        pltpu.VMEM((nh, nct, c + d, d), BF16),
                        pltpu.VMEM((nh, nct, c, d), F32),
                        pltpu.VMEM((nh, nct, d, d), F32),
                        pltpu.VMEM((nh, nct, 8, d), F32)],
        compiler_params=_params("parallel", "arbitrary"),
        name="gated_delta_net",
    )(qkv, z, sm, smt, conv_w, alog_row, dtb_row, alog_lanes, dtb_lanes, norm_w)


def _cmp_kernel(yk_ref, yv_ref, pek_ref, pev_ref, w1k_ref, w1v_ref, w2k_ref, w2vt_ref, kc_ref, vct_ref):
    nrow = yk_ref.shape[0]

    def hidden(y_ref, pe_ref, w1_ref):
        y = y_ref[...]
        first = _mm(y + pe_ref[0:1, :], w1_ref[0])
        second = _mm(y + pe_ref[1:2, :], w1_ref[1])
        pre = first + pltpu.roll(second, nrow - 1, 0)
        return jax.nn.gelu(pre)

    hk = hidden(yk_ref, pek_ref, w1k_ref)
    kc_ref[...] = jnp.concatenate(
        [_mm(hk[:, g * CMP_HIDDEN:(g + 1) * CMP_HIDDEN], w2k_ref[...]) for g in range(NSA_GROUPS)], axis=1)
    hv = hidden(yv_ref, pev_ref, w1v_ref)
    vct_ref[...] = jnp.concatenate(
        [_mm_nt(w2vt_ref[...], hv[:, g * CMP_HIDDEN:(g + 1) * CMP_HIDDEN]) for g in range(NSA_GROUPS)], axis=0)


def _compress(yk, yv, pek, pev, w1k, w1v, w2k, w2vt):
    b, nrow, width = yk.shape
    hid = NSA_GROUPS * CMP_HIDDEN
    full = lambda shape: pl.BlockSpec(shape, lambda i, n=len(shape): (0,) * n)
    return pl.pallas_call(
        _cmp_kernel,
        out_shape=[jax.ShapeDtypeStruct((b, nrow, NSA_KW), F32), jax.ShapeDtypeStruct((b, NSA_KW, nrow), F32)],
        grid=(b,),
        in_specs=[pl.BlockSpec((None, nrow, width), lambda i: (i, 0, 0)),
                  pl.BlockSpec((None, nrow, width), lambda i: (i, 0, 0)),
                  full((2, width)), full((2, width)), full((2, width, hid)), full((2, width, hid)),
                  full((CMP_HIDDEN, NSA_DK)), full((NSA_DK, CMP_HIDDEN))],
        out_specs=[pl.BlockSpec((None, nrow, NSA_KW), lambda i: (i, 0, 0)),
                   pl.BlockSpec((None, NSA_KW, nrow), lambda i: (i, 0, 0))],
        compiler_params=_params("parallel"),
        name="nsa_compress",
    )(yk, yv, pek, pev, w1k, w1v, w2k, w2vt)


def _softmax_cols(s):
    m = jnp.max(s, axis=0, keepdims=True)
    e = jnp.exp(s - m)
    return e * (1.0 / jnp.sum(e, axis=0, keepdims=True))


def _nsa_kernel(xt_ref, z_ref, kc_ref, vct_ref, ks_ref, vst_ref, kw_ref, vwt_ref, ovl_ref, o_ref, sel_s):
    tq = z_ref.shape[0]
    s_len = ks_ref.shape[0]
    ncmp = kc_ref.shape[0]
    nslc = s_len // SLC_BLOCK
    qt = pl.program_id(1)
    t0 = qt * tq
    scale = NSA_DK ** -0.5
    ng, hp, nh = NSA_GROUPS, NSA_HPG, NSA_HEADS
    cols = nh * tq
    t_row = t0 + lax.broadcasted_iota(jnp.int32, (1, tq), 1)
    sub = lax.broadcasted_iota(jnp.int32, (LANES, tq), 0)
    low = sub < NSA_DK
    gates = _sigmoid(xt_ref[NSA_QW + 2 * NSA_KW:NSA_QW + 2 * NSA_KW + NSA_GATE_ROWS, :])
    tile = lambda a, n: jnp.concatenate([a] * n, axis=1)
    zeros_half = jnp.zeros((NSA_DK, tq), F32)

    qs = []
    for head in range(nh):
        qh = xt_ref[head * NSA_DK:(head + 1) * NSA_DK, :] * scale
        qs.append(jnp.concatenate([qh, zeros_half] if head // hp == 0 else [zeros_half, qh], axis=0))
    q8 = _bf(jnp.concatenate(qs, axis=1))

    c_end = lax.broadcasted_iota(jnp.int32, (ncmp, 1), 0) * CMP_STRIDE + (CMP_LEN - 1)
    cmask8 = tile(c_end <= t_row, nh)
    s_c = jnp.dot(_bf(kc_ref[...]), q8, preferred_element_type=F32)
    p_c = jnp.where(cmask8, _softmax_cols(jnp.where(cmask8, s_c, NEG)), 0.0)
    o_cmp = jnp.dot(_bf(vct_ref[...]), _bf(p_c), preferred_element_type=F32)

    ovl = ovl_ref[...]
    jb = lax.broadcasted_iota(jnp.int32, (nslc, tq), 0)
    cur = t_row // SLC_BLOCK
    forced = (jb == 0) | (jb == cur) | (jb == cur - 1)
    for g in range(ng):
        p_sum = p_c[:, g * hp * tq:(g * hp + 1) * tq]
        for hh in range(1, hp):
            p_sum = p_sum + p_c[:, (g * hp + hh) * tq:(g * hp + hh + 1) * tq]
        imp = sum(jnp.dot(ovl, part, preferred_element_type=F32) for part in _split3(p_sum))
        rank = jnp.where(jb <= cur, jnp.where(forced, BIG, imp), -BIG)
        cnt = jnp.zeros((nslc, tq), F32)
        for i in range(nslc):
            ri = rank[i:i + 1, :]
            beats = (ri > rank) | ((ri == rank) & (jb > i))
            cnt = cnt + jnp.where(beats, 1.0, 0.0)
        sel = jnp.where(cnt < float(min(N_SEL, nslc)), 1.0, 0.0)
        for kt in range(s_len // LANES):
            sel_s[g, kt, 0:2, :] = sel[2 * kt:2 * kt + 2, :]

    def slc_body(kt, carry):
        m_run, l_run, acc = carry
        k0 = pl.multiple_of(kt * LANES, LANES)
        causal = (k0 + sub) <= t_row
        biases = []
        for g in range(ng):
            pair = sel_s[g, kt, 0:2, :]
            keep = (jnp.where(low, pair[0:1, :], pair[1:2, :]) > 0.5) & causal
            biases.append(tile(jnp.where(keep, 0.0, NEG), hp))
        s_s = jnp.dot(_bf(ks_ref[pl.ds(k0, LANES), :]), q8, preferred_element_type=F32)
        s_s = s_s + jnp.concatenate(biases, axis=1)
        m_new = jnp.maximum(m_run, jnp.max(s_s, axis=0, keepdims=True))
        alpha = jnp.exp(m_run - m_new)
        p = jnp.exp(s_s - m_new)
        l_run = alpha * l_run + jnp.sum(p, axis=0, keepdims=True)
        acc = alpha * acc + jnp.dot(_bf(vst_ref[:, pl.ds(k0, LANES)]), _bf(p), preferred_element_type=F32)
        return m_new, l_run, acc

    init = (jnp.full((1, cols), NEG, F32), jnp.zeros((1, cols), F32), jnp.zeros((LANES, cols), F32))
    _, l_s, acc_s = lax.fori_loop(0, qt + 1, slc_body, init)
    o_slc = acc_s * (1.0 / l_s)

    nwin = WIN + tq
    w0 = pl.multiple_of(jnp.maximum(t0 - WIN, 0), LANES)
    kpos = w0 + lax.broadcasted_iota(jnp.int32, (nwin, 1), 0)
    dq = t_row - kpos
    wmask8 = tile((dq >= 0) & (dq < WIN), nh)
    s_w = jnp.dot(_bf(kw_ref[pl.ds(w0, nwin), :]), q8, preferred_element_type=F32)
    p_w = _softmax_cols(jnp.where(wmask8, s_w, NEG))
    o_win = jnp.dot(_bf(vwt_ref[:, pl.ds(w0, nwin)]), _bf(p_w), preferred_element_type=F32)

    def head_out(head):
        cs = slice(head * tq, (head + 1) * tq)
        row = 3 * head
        return (gates[row:row + 1, :] * o_cmp[:, cs] + gates[row + 1:row + 2, :] * o_slc[:, cs]
                + gates[row + 2:row + 3, :] * o_win[:, cs])

    for hh in range(hp):
        both = jnp.where(low, head_out(hh), head_out(hp + hh))
        cs = slice(hh * LANES, (hh + 1) * LANES)
        o_ref[:, cs] = both.T * _silu(z_ref[:, cs])


def _nsa_attention(xt, z, kcb, vcbt, ks, kw, ovl, b, s):
    tq = Q_TILE
    ncmp = kcb.shape[1]
    nqt = s // tq
    nslc = s // SLC_BLOCK
    vrow = NSA_QW // NSA_KW
    return pl.pallas_call(
        _nsa_kernel,
        out_shape=jax.ShapeDtypeStruct((b, s, NSA_QW), F32),
        grid=(b, nqt),
        in_specs=[pl.BlockSpec((NSA_T_ROWS, tq), lambda i, t: (0, i * nqt + t)),
                  pl.BlockSpec((None, tq, NSA_QW), lambda i, t: (i, t, 0)),
                  pl.BlockSpec((None, ncmp, NSA_KW), lambda i, t: (i, 0, 0)),
                  pl.BlockSpec((None, NSA_KW, ncmp), lambda i, t: (i, 0, 0)),
                  pl.BlockSpec((None, s, NSA_KW), lambda i, t: (i, 0, 0)),
                  pl.BlockSpec((NSA_KW, s), lambda i, t: (vrow, i)),
                  pl.BlockSpec((None, s, NSA_KW), lambda i, t: (i, 0, 0)),
                  pl.BlockSpec((NSA_KW, s), lambda i, t: (vrow + 1, i)),
                  pl.BlockSpec(ovl.shape, lambda i, t: (0, 0))],
        out_specs=pl.BlockSpec((None, tq, NSA_QW), lambda i, t: (i, t, 0)),
        scratch_shapes=[pltpu.VMEM((NSA_GROUPS, s // LANES, 8, tq), F32)],
        compiler_params=_params("parallel", "arbitrary"),
        name="nsa_attention",
    )(xt, z, kcb, vcbt, ks, xt, kw, xt, ovl)


def _nsa_tables(s):
    ncmp_rows = s // CMP_STRIDE
    nslc = s // SLC_BLOCK
    ci = np.arange(ncmp_rows)[None, :] * CMP_STRIDE
    sj = np.arange(nslc)[:, None] * SLC_BLOCK
    ovl = ((ci < sj + SLC_BLOCK) & (ci + CMP_LEN > sj)).astype(np.float32)
    return jnp.asarray(ovl, BF16)


def _merge_kernel(x_ref, p_ref, oa_ref, ob_ref, oc_ref, wm_ref, wa_ref, wb_ref, wc_ref, wo_ref, wpg_ref, wp_ref,
                  lg_ref, lb_ref, o_ref, *, alpha):
    d = x_ref.shape[1]
    x = x_ref[...]
    xb = _bf(x)
    mixed = None
    for idx, (br_ref, w_ref) in enumerate(((oa_ref, wa_ref), (ob_ref, wb_ref), (oc_ref, wc_ref))):
        gate = _sigmoid(jnp.dot(xb, wm_ref[:, idx * d:(idx + 1) * d], preferred_element_type=F32))
        term = gate * _mm(br_ref[...], w_ref[...])
        mixed = term if mixed is None else mixed + term
    sub = _mm(mixed, wo_ref[...])
    ple = _sigmoid(jnp.dot(xb, wpg_ref[...], preferred_element_type=F32)) * _mm(p_ref[...], wp_ref[...])
    y = alpha * x + sub + ple
    mu = jnp.mean(y, axis=-1, keepdims=True)
    yc = y - mu
    var = jnp.mean(yc * yc, axis=-1, keepdims=True)
    o_ref[...] = yc * lax.rsqrt(var + 1e-5) * lg_ref[...] + lb_ref[...]


def _merge(x2d, p2d, oa, ob, oc, wm, wa, wb, wc, wo, wpg, wp, lg, lb, alpha, tm=256):
    m, d = x2d.shape
    tile = lambda wd: pl.BlockSpec((tm, wd), lambda i: (i, 0))
    full = lambda a: pl.BlockSpec(a.shape, lambda i: (0, 0))
    weights = (wm, wa, wb, wc, wo, wpg, wp, lg, lb)
    return pl.pallas_call(
        functools.partial(_merge_kernel, alpha=alpha),
        out_shape=jax.ShapeDtypeStruct((m, d), F32),
        grid=(m // tm,),
        in_specs=[tile(d), tile(p2d.shape[1]), tile(oa.shape[1]), tile(ob.shape[1]), tile(oc.shape[1])]
        + [full(a) for a in weights],
        out_specs=tile(d),
        compiler_params=_params("parallel"),
        name="merge_out_norm",
    )(x2d, p2d, oa, ob, oc, *weights)


def _nsa_out_perm():
    idx = np.arange(NSA_QW).reshape(NSA_GROUPS, NSA_HPG, NSA_DK)
    return idx.transpose(1, 0, 2).reshape(-1)


def _layer_weights(w_in, conv_w, a_log, dt_bias, norm_w, pe_k, pe_v, w1k, w2k, w1v, w2v, w_br_c, d_model):
    sizes = (RET_W, RET_W, RET_W, RET_W, 3 * GDN_W, GDN_HEADS, GDN_HEADS, GDN_W,
             NSA_QW, NSA_KW, NSA_KW, NSA_KW, NSA_KW, NSA_KW, NSA_KW, 3 * NSA_HEADS, NSA_QW, 3 * d_model)
    offs = np.concatenate([[0], np.cumsum(sizes)])
    col = lambda i: w_in[:, offs[i]:offs[i + 1]]
    perm = np.concatenate([hd * RET_D + np.concatenate([np.arange(0, RET_D, 2), np.arange(1, RET_D, 2)])
                           for hd in range(RET_HEADS)])
    w_ret = _bf(jnp.concatenate([col(0)[:, perm], col(1)[:, perm], col(2), col(3)], axis=1))
    small = jnp.concatenate([col(5), col(6)], axis=1)
    w_gdn = _bf(jnp.concatenate([col(4), col(7), jnp.pad(small, ((0, 0), (0, LANES - small.shape[1])))], axis=1))
    w_gdn_t = _bf(small.T)
    operm = _nsa_out_perm()
    w_nsa = _bf(jnp.concatenate([col(16)[:, operm], col(9), col(10), col(11), col(13)], axis=1))
    gates_t = jnp.pad(col(15).T, ((0, NSA_GATE_ROWS - 3 * NSA_HEADS), (0, 0)))
    w_nsa_t = _bf(jnp.concatenate([col(8).T, col(12).T, col(14).T, gates_t], axis=0))
    w_merge = _bf(col(17))
    lane_ids = jnp.arange(LANES)
    pick = lambda vec: jnp.where((lane_ids >= GDN_HEADS) & (lane_ids < 2 * GDN_HEADS),
                                 jnp.pad(vec, (GDN_HEADS, LANES - 2 * GDN_HEADS)), 0.0)[None, :]
    alog_row = pick(a_log.astype(F32))
    dtb_row = pick(dt_bias.astype(F32))
    alog_lanes = jnp.repeat(a_log.astype(F32), GDN_CHUNK)[None, :]
    dtb_lanes = jnp.repeat(dt_bias.astype(F32), GDN_CHUNK)[None, :]
    half = CMP_LEN // 2

    def big_w1(w1):
        w = w1.reshape(2, half, NSA_DK, CMP_HIDDEN)
        out = jnp.zeros((2, half, NSA_GROUPS, NSA_DK, NSA_GROUPS, CMP_HIDDEN), F32)
        for g in range(NSA_GROUPS):
            out = out.at[:, :, g, :, g, :].set(w)
        return _bf(out.reshape(2, half * NSA_GROUPS * NSA_DK, NSA_GROUPS * CMP_HIDDEN))

    def pe_rows(pe):
        t = jnp.broadcast_to(pe.reshape(2, half, 1, NSA_DK), (2, half, NSA_GROUPS, NSA_DK))
        return t.reshape(2, half * NSA_GROUPS * NSA_DK).astype(F32)

    return dict(w_ret=w_ret, w_gdn=w_gdn, w_gdn_t=w_gdn_t, w_nsa=w_nsa, w_nsa_t=w_nsa_t, w_merge=w_merge,
                alog_row=alog_row, dtb_row=dtb_row, alog_lanes=alog_lanes, dtb_lanes=dtb_lanes,
                norm_w=norm_w.astype(F32)[None, :], conv_w=conv_w.astype(F32),
                pek=pe_rows(pe_k), pev=pe_rows(pe_v), w1k=big_w1(w1k), w1v=big_w1(w1v), w2k=_bf(w2k),
                w2vt=_bf(w2v.T), w_br_c=_bf(w_br_c[operm, :]))


def _layer(x, p_i, lw, w_br_a, w_br_b, w_out, ln_g, ln_b, w_ple, w_ple_gate, ret_tabs, nsa_ovl, alpha):
    b, s, d = x.shape
    m = b * s
    x2d = x.reshape(m, d)
    r3 = lambda a: a.reshape(b, s, a.shape[-1])

    hq, hk, hv, hz = _proj(x2d, lw["w_ret"], (RET_W,) * 4, name="proj_ret")
    o_a = _retention(r3(hq), r3(hk), r3(hv), r3(hz), ret_tabs)

    qkv, gz, sm, smt = _proj_t(x2d, lw["w_gdn"], lw["w_gdn_t"], (3 * GDN_W, GDN_W, LANES), name="proj_gdn")
    nchunk = s // GDN_CHUNK
    smt = smt.reshape(2, GDN_HEADS, b, nchunk, GDN_CHUNK).transpose(2, 3, 0, 1, 4)
    smt = jnp.pad(smt.reshape(b, nchunk, 2, GDN_HEADS * GDN_CHUNK), ((0, 0), (0, 0), (0, 6), (0, 0)))
    o_b = _gdn(r3(qkv), r3(gz), r3(sm), smt, lw["conv_w"], lw["alog_row"], lw["dtb_row"], lw["alog_lanes"],
               lw["dtb_lanes"], lw["norm_w"])

    nz, kc, vc, ks, kw, xt = _proj_t(x2d, lw["w_nsa"], lw["w_nsa_t"], (NSA_QW,) + (NSA_KW,) * 4, name="proj_nsa")
    slab = lambda a: a.reshape(b, s // CMP_STRIDE, CMP_STRIDE * NSA_KW)
    kcb, vcbt = _compress(slab(kc), slab(vc), lw["pek"], lw["pev"], lw["w1k"], lw["w1v"], lw["w2k"], lw["w2vt"])
    o_c = _nsa_attention(xt, r3(nz), kcb, vcbt, r3(ks), r3(kw), nsa_ovl, b, s)

    out = _merge(x2d, p_i.reshape(m, -1), o_a.reshape(m, -1), o_b.reshape(m, -1), o_c.reshape(m, -1),
                 lw["w_merge"], _bf(w_br_a), _bf(w_br_b), lw["w_br_c"], _bf(w_out), _bf(w_ple_gate), _bf(w_ple),
                 ln_g.astype(F32)[None, :], ln_b.astype(F32)[None, :], alpha)
    return out.reshape(b, s, d)


def kernel(x, p, w_in, gdn_conv_w, gdn_a_log, gdn_dt_bias, gdn_norm_w, nsa_pe_k, nsa_pe_v, nsa_cmp_w1k, nsa_cmp_w2k, nsa_cmp_w1v, nsa_cmp_w2v, w_branch_a, w_branch_b, w_branch_c, w_out, ln_g, ln_b, w_ple, w_ple_gate):
    depth = w_in.shape[0]
    s, d = x.shape[1], x.shape[2]
    alpha = (2 * depth) ** 0.25
    ret_tabs = _ret_tables(s)
    nsa_ovl = _nsa_tables(s)
    for i in range(depth):
        lw = _layer_weights(w_in[i], gdn_conv_w[i], gdn_a_log[i], gdn_dt_bias[i], gdn_norm_w[i], nsa_pe_k[i],
                            nsa_pe_v[i], nsa_cmp_w1k[i], nsa_cmp_w2k[i], nsa_cmp_w1v[i], nsa_cmp_w2v[i],
                            w_branch_c[i], d)
        x = _layer(x, p[i], lw, w_branch_a[i], w_branch_b[i], w_out[i], ln_g[i], ln_b[i],
                   w_ple[i], w_ple_gate[i], ret_tabs, nsa_ovl, alpha)
    return x
```

```python
import functools

import numpy as np
import jax
import jax.numpy as jnp
from jax import lax
from jax.experimental import pallas as pl
from jax.experimental.pallas import tpu as pltpu

F32 = jnp.float32
BF16 = jnp.bfloat16

RET_HEADS, RET_D, RET_CHUNK = 4, 128, 128
ROPE_BASE = 10000.0
GDN_HEADS, GDN_D, GDN_CHUNK, CONV_K = 4, 128, 64, 4
NSA_HEADS, NSA_GROUPS, NSA_DK = 8, 2, 64
NSA_HPG = NSA_HEADS // NSA_GROUPS
CMP_LEN, CMP_STRIDE, CMP_HIDDEN = 32, 16, 128
SLC_BLOCK, N_SEL, WIN = 64, 8, 256
NEG, BIG, EPS = -1e30, 1e30, 1e-6
LANES = 128
Q_TILE = 128
GDN_UNROLL = 2
VMEM_LIMIT = 56 * 1024 * 1024

RET_W = RET_HEADS * RET_D
GDN_W = GDN_HEADS * GDN_D
NSA_QW = NSA_HEADS * NSA_DK
NSA_KW = NSA_GROUPS * NSA_DK
NSA_GATE_ROWS = 32
NSA_T_ROWS = NSA_QW + 2 * NSA_KW + NSA_GATE_ROWS


def _bf(x):
    return x.astype(BF16)


def _mm(a, b):
    return jnp.dot(_bf(a), _bf(b), preferred_element_type=F32)


def _mm_nt(a, b):
    return lax.dot_general(_bf(a), _bf(b), (((1,), (1,)), ((), ())), preferred_element_type=F32)


def _split3(x):
    hi = _bf(x)
    r1 = x - hi.astype(F32)
    mid = _bf(r1)
    lo = _bf(r1 - mid.astype(F32))
    return hi, mid, lo


def _sigmoid(x):
    return 0.5 * jnp.tanh(0.5 * x) + 0.5


def _silu(x):
    return x * _sigmoid(x)


def _softplus(x):
    return jnp.maximum(x, 0.0) + jnp.log(1.0 + jnp.exp(-jnp.abs(x)))


def _params(*sem):
    return pltpu.CompilerParams(dimension_semantics=sem, vmem_limit_bytes=VMEM_LIMIT)


def _proj_kernel(x_ref, w_ref, *o_refs, widths):
    x = _bf(x_ref[...])
    off = 0
    for o_ref, wd in zip(o_refs, widths):
        o_ref[...] = jnp.dot(x, w_ref[:, off:off + wd], preferred_element_type=F32).astype(o_ref.dtype)
        off += wd


def _proj(x2d, w, widths, tm=512, name="proj"):
    m, k = x2d.shape
    n = w.shape[1]
    assert sum(widths) == n and m % tm == 0
    return pl.pallas_call(
        functools.partial(_proj_kernel, widths=tuple(widths)),
        out_shape=[jax.ShapeDtypeStruct((m, wd), F32) for wd in widths],
        grid=(m // tm,),
        in_specs=[pl.BlockSpec((tm, k), lambda i: (i, 0)), pl.BlockSpec((k, n), lambda i: (0, 0))],
        out_specs=[pl.BlockSpec((tm, wd), lambda i: (i, 0)) for wd in widths],
        compiler_params=_params("parallel"),
        name=name,
    )(x2d, w)


def _proj_t_kernel(x_ref, w_ref, wt_ref, *o_refs, widths):
    x = _bf(x_ref[...])
    off = 0
    for o_ref, wd in zip(o_refs[:-1], widths):
        o_ref[...] = jnp.dot(x, w_ref[:, off:off + wd], preferred_element_type=F32)
        off += wd
    o_refs[-1][...] = lax.dot_general(wt_ref[...], x, (((1,), (1,)), ((), ())), preferred_element_type=F32)


def _proj_t(x2d, w, wt, widths, tm=512, name="proj_t"):
    m, k = x2d.shape
    n = w.shape[1]
    nt = wt.shape[0]
    assert sum(widths) == n and m % tm == 0
    return pl.pallas_call(
        functools.partial(_proj_t_kernel, widths=tuple(widths)),
        out_shape=[jax.ShapeDtypeStruct((m, wd), F32) for wd in widths] + [jax.ShapeDtypeStruct((nt, m), F32)],
        grid=(m // tm,),
        in_specs=[pl.BlockSpec((tm, k), lambda i: (i, 0)), pl.BlockSpec((k, n), lambda i: (0, 0)),
                  pl.BlockSpec((nt, k), lambda i: (0, 0))],
        out_specs=[pl.BlockSpec((tm, wd), lambda i: (i, 0)) for wd in widths]
        + [pl.BlockSpec((nt, tm), lambda i: (0, i))],
        compiler_params=_params("parallel"),
        name=name,
    )(x2d, w, wt)


def _ret_kernel(q_ref, k_ref, v_ref, z_ref, cos_ref, sin_ref, dmask_ref, qdec_ref, kdec_ref, cdec_ref, o_ref,
                state_s):
    ts = q_ref.shape[0]
    c = RET_CHUNK
    half = RET_D // 2
    scale = RET_D ** -0.5

    @pl.when(pl.program_id(1) == 0)
    def _():
        state_s[...] = jnp.zeros_like(state_s)

    def body(n, carry):
        sl = pl.ds(pl.multiple_of(n * c, c), c)
        cos = cos_ref[sl, :]
        sin = sin_ref[sl, :]
        for h in range(RET_HEADS):
            hs = slice(h * RET_D, (h + 1) * RET_D)
            q = q_ref[sl, hs]
            k = k_ref[sl, hs]
            v = v_ref[sl, hs]
            state = state_s[h]
            q = q * cos + pltpu.roll(q, half, 1) * sin
            k = (k * cos + pltpu.roll(k, half, 1) * sin) * scale
            scores = _mm_nt(q, k) * dmask_ref[h]
            o = _mm(scores, v) + _mm(q * qdec_ref[h], state)
            state_s[h] = cdec_ref[h] * state + _mm((k * kdec_ref[h]).T, v)
            mu = jnp.mean(o, axis=-1, keepdims=True)
            d = o - mu
            var = jnp.mean(d * d, axis=-1, keepdims=True)
            o_ref[sl, hs] = d * lax.rsqrt(var + 1e-5) * _silu(z_ref[sl, hs])
        return carry

    lax.fori_loop(0, ts // c, body, 0)


def _retention(hq, hk, hv, hz, tabs, ts=512):
    b, s, _ = hq.shape
    cos, sin, dmask, qdec, kdec, cdec = tabs
    seq = lambda: pl.BlockSpec((None, ts, RET_W), lambda i, t: (i, t, 0))
    tab = lambda: pl.BlockSpec((RET_HEADS, RET_CHUNK, RET_D), lambda i, t: (0, 0, 0))
    rot = lambda: pl.BlockSpec((ts, RET_D), lambda i, t: (t, 0))
    return pl.pallas_call(
        _ret_kernel,
        out_shape=jax.ShapeDtypeStruct((b, s, RET_W), F32),
        grid=(b, s // ts),
        in_specs=[seq(), seq(), seq(), seq(), rot(), rot(), tab(), tab(), tab(), tab()],
        out_specs=seq(),
        scratch_shapes=[pltpu.VMEM((RET_HEADS, RET_D, RET_D), F32)],
        compiler_params=_params("parallel", "arbitrary"),
        name="retention",
    )(hq, hk, hv, hz, cos, sin, dmask, qdec, kdec, cdec)


def _ret_tables(s):
    inv = ROPE_BASE ** (-jnp.arange(0, RET_D, 2, dtype=F32) / RET_D)
    ang = jnp.arange(s, dtype=F32)[:, None] * inv[None, :]
    cos = jnp.concatenate([jnp.cos(ang), jnp.cos(ang)], axis=-1)
    sin = jnp.concatenate([-jnp.sin(ang), jnp.sin(ang)], axis=-1)
    c = RET_CHUNK
    log_g = jnp.log1p(-jnp.exp2(-5.0 - jnp.arange(RET_HEADS, dtype=F32)))
    pos = jnp.arange(c, dtype=F32)
    diff = pos[:, None] - pos[None, :]
    causal = diff >= 0
    dmask = jnp.where(causal[None], jnp.exp(jnp.where(causal, diff, 0.0)[None] * log_g[:, None, None]), 0.0)
    ones = jnp.ones((1, 1, RET_D), F32)
    kdec = jnp.exp((c - 1 - pos)[None, :, None] * log_g[:, None, None]) * ones
    qdec = jnp.exp((pos + 1.0)[None, :, None] * log_g[:, None, None]) * ones
    cdec = jnp.exp(c * log_g)[:, None, None] * jnp.ones((1, c, RET_D), F32)
    return cos, sin, dmask, qdec, kdec, cdec


def _gdn_kernel(qkv_ref, z_ref, sm_ref, smt_ref, cw_ref, alog_ref, dtb_ref, alog_l_ref, dtb_l_ref, nw_ref, o_ref,
                state_s, xs_s, lhs_s, ou_s, psi_s, dec_s):
    ts = qkv_ref.shape[0]
    c, d, nh = GDN_CHUNK, GDN_D, GDN_HEADS
    hc = nh * c
    nct = ts // c
    t = pl.program_id(1)
    ri = lax.broadcasted_iota(jnp.int32, (hc, hc), 0)
    ci = lax.broadcasted_iota(jnp.int32, (hc, hc), 1)
    same = (ri // c) == (ci // c)
    incl = same & (ri >= ci)
    strict = same & (ri > ci)
    upper = same & (ri <= ci)
    neg_a = -jnp.exp(alog_ref[...])
    dtb = dtb_ref[...]
    neg_a_l = -jnp.exp(alog_l_ref[...])
    dtb_l = dtb_l_ref[...]
    nw = nw_ref[...]

    @pl.when(t == 0)
    def _():
        state_s[...] = jnp.zeros_like(state_s)
        xs_s[0:8, :] = jnp.zeros((8, xs_s.shape[1]), F32)

    xs_s[8:8 + ts, :] = qkv_ref[...]

    def conv_silu_stack(n, part):
        start = pl.multiple_of(n * c, c)
        cols = slice(part * nh * d, (part + 1) * nh * d)
        w = cw_ref[:, cols]
        ext = xs_s[pl.ds(start, c + 8), cols]
        y = ext[8:8 + c, :] * w[CONV_K - 1:CONV_K, :]
        for j in range(1, CONV_K):
            y = y + ext[8 - j:8 - j + c, :] * w[CONV_K - 1 - j:CONV_K - j, :]
        y = _silu(y)
        return [y[:, h * d:(h + 1) * d] for h in range(nh)]

    def l2n(xs, mult):
        out = [x * (lax.rsqrt(jnp.sum(x * x, axis=-1, keepdims=True) + EPS) * mult) for x in xs]
        return jnp.concatenate(out, axis=0)

    def prep(it, carry):
        chunks = [it * GDN_UNROLL + u for u in range(GDN_UNROLL)]
        st = []
        for n in chunks:
            sl = pl.ds(pl.multiple_of(n * c, c), c)
            q = l2n(conv_silu_stack(n, 0), d ** -0.5)
            k = l2n(conv_silu_stack(n, 1), 1.0)
            v = jnp.concatenate(conv_silu_stack(n, 2), axis=0)
            sm = sm_ref[sl, :]
            beta_col = jnp.concatenate([_sigmoid(sm[:, h:h + 1]) for h in range(nh)], axis=0)
            g_col = jnp.concatenate(
                [neg_a[:, nh + h:nh + h + 1] * _softplus(sm[:, nh + h:nh + h + 1] + dtb[:, nh + h:nh + h + 1])
                 for h in range(nh)], axis=0)
            smt = smt_ref[t * nct + n]
            g_row = neg_a_l * _softplus(smt[1:2, :] + dtb_l)
            g_row_b = jnp.broadcast_to(g_row, (hc, hc))
            gc_col = jnp.sum(jnp.where(incl, g_row_b, 0.0), axis=1, keepdims=True)
            gc_row = jnp.sum(jnp.where(upper, jnp.broadcast_to(g_col, (hc, hc)), 0.0), axis=0, keepdims=True)
            g_last = jnp.sum(jnp.where(same, g_row_b, 0.0), axis=1, keepdims=True)
            decay = jnp.where(incl, jnp.exp(jnp.where(incl, gc_col - gc_row, 0.0)), 0.0)
            egc = jnp.exp(gc_col)
            st.append(dict(q=q, k=k, v=v, beta=beta_col, gc_col=gc_col, g_last=g_last, decay=decay, egc=egc))
        for s_ in st:
            qk_kk = _mm_nt(jnp.concatenate([s_["q"], s_["k"]], axis=0), s_["k"])
            s_["attn"] = qk_kk[:hc] * s_["decay"]
            s_["mp"] = _bf(-jnp.where(strict, s_["beta"] * qk_kk[hc:] * s_["decay"], 0.0))
            s_["r"] = jnp.concatenate([s_["v"] * s_["beta"], s_["k"] * (s_["beta"] * s_["egc"])], axis=1)
        for s_ in st:
            s_["r"] = s_["r"] + jnp.dot(s_["mp"], _bf(s_["r"]), preferred_element_type=F32)
        for _ in range(5):
            for s_ in st:
                s_["mp"] = _bf(jnp.dot(s_["mp"], s_["mp"], preferred_element_type=F32))
            for s_ in st:
                s_["r"] = s_["r"] + jnp.dot(s_["mp"], _bf(s_["r"]), preferred_element_type=F32)
        for n, s_ in zip(chunks, st):
            rb = _bf(s_["r"])
            ar = jnp.dot(_bf(s_["attn"]), rb, preferred_element_type=F32)
            k_dec = s_["k"] * jnp.exp(s_["g_last"] - s_["gc_col"])
            qt = s_["q"] * s_["egc"] - ar[:, d:]
            for h in range(nh):
                rows = slice(h * c, (h + 1) * c)
                kr = jnp.dot(_bf(k_dec[rows].T), rb[rows], preferred_element_type=F32)
                lhs_s[h, n, 0:c, :] = _bf(qt[rows])
                lhs_s[h, n, c:c + d, :] = _bf(-kr[:, d:])
                ou_s[h, n] = ar[rows, :d]
                psi_s[h, n] = kr[:, :d]
                dec_s[h, n] = jnp.broadcast_to(jnp.exp(s_["g_last"][h * c:h * c + 1, :]), (8, d))
        return carry

    lax.fori_loop(0, nct // GDN_UNROLL, prep, 0)

    def scan(n, carry):
        sl = pl.ds(pl.multiple_of(n * c, c), c)
        for h in range(nh):
            hs = slice(h * d, (h + 1) * d)
            state = state_s[h]
            res = jnp.dot(lhs_s[h, n], _bf(state), preferred_element_type=F32)
            o = res[:c] + ou_s[h, n]
            state_s[h] = dec_s[h, n][0:1, :] * state + res[c:] + psi_s[h, n]
            o = o * lax.rsqrt(jnp.mean(o * o, axis=-1, keepdims=True) + EPS) * nw
            o_ref[sl, hs] = o * _silu(z_ref[sl, hs])
        return carry

    lax.fori_loop(0, nct, scan, 0)
    xs_s[0:8, :] = qkv_ref[ts - 8:ts, :]


def _gdn(qkv, z, sm, smt, conv_w, alog_row, dtb_row, alog_lanes, dtb_lanes, norm_w, ts=512):
    b, s, wq = qkv.shape
    nh, c, d = GDN_HEADS, GDN_CHUNK, GDN_D
    nct = ts // c
    assert nct % GDN_UNROLL == 0
    tile = lambda wd: pl.BlockSpec((None, ts, wd), lambda i, t: (i, t, 0))
    full = lambda a: pl.BlockSpec(a.shape, lambda i, t, n=a.ndim: (0,) * n)
    return pl.pallas_call(
        _gdn_kernel,
        out_shape=jax.ShapeDtypeStruct((b, s, GDN_W), F32),
        grid=(b, s // ts),
        in_specs=[tile(wq), tile(GDN_W), tile(LANES),
                  pl.BlockSpec((None, s // c, 8, nh * c), lambda i, t: (i, 0, 0, 0)),
                  full(conv_w), full(alog_row), full(dtb_row), full(alog_lanes), full(dtb_lanes), full(norm_w)],
        out_specs=tile(GDN_W),
        scratch_shapes=[pltpu.VMEM((nh, d, d), F32),
                        pltpu.VMEM((ts + 8, wq), F32),
                        pltpu.VMEM((nh, nct, c + d, d), BF16),
                        pltpu.VMEM((nh, nct, c, d), F32),
                        pltpu.VMEM((nh, nct, d, d), F32),
                        pltpu.VMEM((nh, nct, 8, d), F32)],
        compiler_params=_params("parallel", "arbitrary"),
        name="gated_delta_net",
    )(qkv, z, sm, smt, conv_w, alog_row, dtb_row, alog_lanes, dtb_lanes, norm_w)


def _cmp_kernel(yk_ref, yv_ref, pek_ref, pev_ref, w1k_ref, w1v_ref, w2k_ref, w2vt_ref, kc_ref, vct_ref):
    nrow = yk_ref.shape[0]

    def hidden(y_ref, pe_ref, w1_ref):
        y = y_ref[...]
        first = _mm(y + pe_ref[0:1, :], w1_ref[0])
        second = _mm(y + pe_ref[1:2, :], w1_ref[1])
        pre = first + pltpu.roll(second, nrow - 1, 0)
        return jax.nn.gelu(pre)

    hk = hidden(yk_ref, pek_ref, w1k_ref)
    kc_ref[...] = jnp.concatenate(
        [_mm(hk[:, g * CMP_HIDDEN:(g + 1) * CMP_HIDDEN], w2k_ref[...]) for g in range(NSA_GROUPS)], axis=1)
    hv = hidden(yv_ref, pev_ref, w1v_ref)
    vct_ref[...] = jnp.concatenate(
        [_mm_nt(w2vt_ref[...], hv[:, g * CMP_HIDDEN:(g + 1) * CMP_HIDDEN]) for g in range(NSA_GROUPS)], axis=0)


def _compress(yk, yv, pek, pev, w1k, w1v, w2k, w2vt):
    b, nrow, width = yk.shape
    hid = NSA_GROUPS * CMP_HIDDEN
    full = lambda shape: pl.BlockSpec(shape, lambda i, n=len(shape): (0,) * n)
    return pl.pallas_call(
        _cmp_kernel,
        out_shape=[jax.ShapeDtypeStruct((b, nrow, NSA_KW), F32), jax.ShapeDtypeStruct((b, NSA_KW, nrow), F32)],
        grid=(b,),
        in_specs=[pl.BlockSpec((None, nrow, width), lambda i: (i, 0, 0)),
                  pl.BlockSpec((None, nrow, width), lambda i: (i, 0, 0)),
                  full((2, width)), full((2, width)), full((2, width, hid)), full((2, width, hid)),
                  full((CMP_HIDDEN, NSA_DK)), full((NSA_DK, CMP_HIDDEN))],
        out_specs=[pl.BlockSpec((None, nrow, NSA_KW), lambda i: (i, 0, 0)),
                   pl.BlockSpec((None, NSA_KW, nrow), lambda i: (i, 0, 0))],
        compiler_params=_params("parallel"),
        name="nsa_compress",
    )(yk, yv, pek, pev, w1k, w1v, w2k, w2vt)


LOG2E = 1.4426950408889634


def _exp2_cols(s):
    e = jnp.exp2(s - jnp.max(s, axis=0, keepdims=True))
    return e, 1.0 / jnp.sum(e, axis=0, keepdims=True)


def _nsa_kernel(xt_ref, z_ref, kc_ref, vct_ref, ks_ref, vst_ref, kw_ref, vwt_ref, ovl_ref, o_ref, sel_s,
                s_even, s_odd, p_even, p_odd, acc_s):
    tq = z_ref.shape[0]
    s_len = ks_ref.shape[0]
    ncmp = kc_ref.shape[0]
    nslc = s_len // SLC_BLOCK
    qt = pl.program_id(1)
    t0 = qt * tq
    scale = NSA_DK ** -0.5 * LOG2E
    ng, hp, nh = NSA_GROUPS, NSA_HPG, NSA_HEADS
    cols = nh * tq
    t_row = t0 + lax.broadcasted_iota(jnp.int32, (1, tq), 1)
    sub = lax.broadcasted_iota(jnp.int32, (LANES, tq), 0)
    low = sub < NSA_DK
    gates = _sigmoid(xt_ref[NSA_QW + 2 * NSA_KW:NSA_QW + 2 * NSA_KW + NSA_GATE_ROWS, :])
    tile = lambda a, n: jnp.concatenate([a] * n, axis=1)
    zeros_half = jnp.zeros((NSA_DK, tq), F32)

    qs = []
    for head in range(nh):
        qh = xt_ref[head * NSA_DK:(head + 1) * NSA_DK, :] * scale
        qs.append(jnp.concatenate([qh, zeros_half] if head // hp == 0 else [zeros_half, qh], axis=0))
    q8 = _bf(jnp.concatenate(qs, axis=1))

    c_end = lax.broadcasted_iota(jnp.int32, (ncmp, 1), 0) * CMP_STRIDE + (CMP_LEN - 1)
    cmask8 = tile(c_end <= t_row, nh)
    s_c = jnp.dot(_bf(kc_ref[...]), q8, preferred_element_type=F32)
    e_c, r_c = _exp2_cols(jnp.where(cmask8, s_c, NEG))
    p_c = jnp.where(cmask8, e_c * r_c, 0.0)
    o_cmp = jnp.dot(_bf(vct_ref[...]), _bf(p_c), preferred_element_type=F32)

    ovl = ovl_ref[...]
    jb = lax.broadcasted_iota(jnp.int32, (nslc, tq), 0)
    cur = t_row // SLC_BLOCK
    forced = (jb == 0) | (jb == cur) | (jb == cur - 1)
    for g in range(ng):
        p_sum = p_c[:, g * hp * tq:(g * hp + 1) * tq]
        for hh in range(1, hp):
            p_sum = p_sum + p_c[:, (g * hp + hh) * tq:(g * hp + hh + 1) * tq]
        imp = sum(jnp.dot(ovl, part, preferred_element_type=F32) for part in _split3(p_sum))
        rank = jnp.where(jb <= cur, jnp.where(forced, BIG, imp), -BIG)
        cnt = jnp.zeros((nslc, tq), F32)
        for i in range(nslc):
            ri = rank[i:i + 1, :]
            beats = (ri > rank) | ((ri == rank) & (jb > i))
            cnt = cnt + jnp.where(beats, 1.0, 0.0)
        sel = jnp.where(cnt < float(min(N_SEL, nslc)), 1.0, 0.0)
        for kt in range(s_len // LANES):
            sel_s[g, kt, 0:2, :] = sel[2 * kt:2 * kt + 2, :]

    def qk(kt):
        return jnp.dot(_bf(ks_ref[pl.ds(pl.multiple_of(kt * LANES, LANES), LANES), :]), q8,
                       preferred_element_type=F32)

    def pv(kt, p):
        return jnp.dot(_bf(vst_ref[:, pl.ds(pl.multiple_of(kt * LANES, LANES), LANES)]), p,
                       preferred_element_type=F32)

    nkt = s_len // LANES

    def one_tile(kt, m_run, l_run, s_rd, s_wr, p_rd, p_wr):
        s_wr[...] = qk(jnp.minimum(kt + 1, nkt - 1))
        pv_prev = pv(jnp.maximum(kt - 1, 0), p_rd[...])
        causal = (kt * LANES + sub) <= t_row
        biases = []
        for g in range(ng):
            pair = sel_s[g, kt, 0:2, :]
            keep = (jnp.where(low, pair[0:1, :], pair[1:2, :]) > 0.5) & causal
            biases.append(tile(jnp.where(keep, 0.0, NEG), hp))
        s_s = s_rd[...] + jnp.concatenate(biases, axis=1)
        m_new = jnp.maximum(m_run, jnp.max(s_s, axis=0, keepdims=True))
        alpha = jnp.exp2(m_run - m_new)
        p = jnp.exp2(s_s - m_new)
        p_wr[...] = _bf(p)
        acc_s[...] = alpha * (acc_s[...] + pv_prev)
        return m_new, alpha * l_run + jnp.sum(p, axis=0, keepdims=True)

    def slc_body(j, carry):
        m_run, l_run = one_tile(2 * j, *carry, s_even, s_odd, p_odd, p_even)
        return one_tile(2 * j + 1, m_run, l_run, s_odd, s_even, p_even, p_odd)

    s_even[...] = qk(0)
    p_odd[...] = jnp.zeros((LANES, cols), BF16)
    acc_s[...] = jnp.zeros((LANES, cols), F32)
    npair = qt // 2 + 1
    _, l_s = lax.fori_loop(0, npair, slc_body, (jnp.full((1, cols), NEG, F32), jnp.zeros((1, cols), F32)))
    o_slc = (acc_s[...] + pv(2 * npair - 1, p_odd[...])) * (1.0 / l_s)

    nwin = WIN + tq
    w0 = pl.multiple_of(jnp.maximum(t0 - WIN, 0), LANES)
    kpos = w0 + lax.broadcasted_iota(jnp.int32, (nwin, 1), 0)
    dq = t_row - kpos
    wmask8 = tile((dq >= 0) & (dq < WIN), nh)
    s_w = jnp.dot(_bf(kw_ref[pl.ds(w0, nwin), :]), q8, preferred_element_type=F32)
    e_w, r_w = _exp2_cols(jnp.where(wmask8, s_w, NEG))
    o_win = jnp.dot(_bf(vwt_ref[:, pl.ds(w0, nwin)]), _bf(e_w), preferred_element_type=F32) * r_w

    def head_out(head):
        cs = slice(head * tq, (head + 1) * tq)
        row = 3 * head
        return (gates[row:row + 1, :] * o_cmp[:, cs] + gates[row + 1:row + 2, :] * o_slc[:, cs]
                + gates[row + 2:row + 3, :] * o_win[:, cs])

    for hh in range(hp):
        both = jnp.where(low, head_out(hh), head_out(hp + hh))
        cs = slice(hh * LANES, (hh + 1) * LANES)
        o_ref[:, cs] = both.T * _silu(z_ref[:, cs])


def _nsa_attention(xt, z, kcb, vcbt, ks, kw, ovl, b, s):
    tq = Q_TILE
    ncmp = kcb.shape[1]
    nqt = s // tq
    nslc = s // SLC_BLOCK
    vrow = NSA_QW // NSA_KW
    return pl.pallas_call(
        _nsa_kernel,
        out_shape=jax.ShapeDtypeStruct((b, s, NSA_QW), F32),
        grid=(b, nqt),
        in_specs=[pl.BlockSpec((NSA_T_ROWS, tq), lambda i, t: (0, i * nqt + t)),
                  pl.BlockSpec((None, tq, NSA_QW), lambda i, t: (i, t, 0)),
                  pl.BlockSpec((None, ncmp, NSA_KW), lambda i, t: (i, 0, 0)),
                  pl.BlockSpec((None, NSA_KW, ncmp), lambda i, t: (i, 0, 0)),
                  pl.BlockSpec((None, s, NSA_KW), lambda i, t: (i, 0, 0)),
                  pl.BlockSpec((NSA_KW, s), lambda i, t: (vrow, i)),
                  pl.BlockSpec((None, s, NSA_KW), lambda i, t: (i, 0, 0)),
                  pl.BlockSpec((NSA_KW, s), lambda i, t: (vrow + 1, i)),
                  pl.BlockSpec(ovl.shape, lambda i, t: (0, 0))],
        out_specs=pl.BlockSpec((None, tq, NSA_QW), lambda i, t: (i, t, 0)),
        scratch_shapes=[pltpu.VMEM((NSA_GROUPS, s // LANES, 8, tq), F32),
                        pltpu.VMEM((LANES, NSA_HEADS * tq), F32),
                        pltpu.VMEM((LANES, NSA_HEADS * tq), F32),
                        pltpu.VMEM((LANES, NSA_HEADS * tq), BF16),
                        pltpu.VMEM((LANES, NSA_HEADS * tq), BF16),
                        pltpu.VMEM((LANES, NSA_HEADS * tq), F32)],
        compiler_params=_params("parallel", "arbitrary"),
        name="nsa_attention",
    )(xt, z, kcb, vcbt, ks, xt, kw, xt, ovl)


def _nsa_tables(s):
    ncmp_rows = s // CMP_STRIDE
    nslc = s // SLC_BLOCK
    ci = np.arange(ncmp_rows)[None, :] * CMP_STRIDE
    sj = np.arange(nslc)[:, None] * SLC_BLOCK
    ovl = ((ci < sj + SLC_BLOCK) & (ci + CMP_LEN > sj)).astype(np.float32)
    return jnp.asarray(ovl, BF16)


def _merge_kernel(x_ref, p_ref, oa_ref, ob_ref, oc_ref, wm_ref, wa_ref, wb_ref, wc_ref, wo_ref, wpg_ref, wp_ref,
                  lg_ref, lb_ref, o_ref, *, alpha):
    d = x_ref.shape[1]
    x = x_ref[...]
    xb = _bf(x)
    mixed = None
    for idx, (br_ref, w_ref) in enumerate(((oa_ref, wa_ref), (ob_ref, wb_ref), (oc_ref, wc_ref))):
        gate = _sigmoid(jnp.dot(xb, wm_ref[:, idx * d:(idx + 1) * d], preferred_element_type=F32))
        term = gate * _mm(br_ref[...], w_ref[...])
        mixed = term if mixed is None else mixed + term
    sub = _mm(mixed, wo_ref[...])
    ple = _sigmoid(jnp.dot(xb, wpg_ref[...], preferred_element_type=F32)) * _mm(p_ref[...], wp_ref[...])
    y = alpha * x + sub + ple
    mu = jnp.mean(y, axis=-1, keepdims=True)
    yc = y - mu
    var = jnp.mean(yc * yc, axis=-1, keepdims=True)
    o_ref[...] = yc * lax.rsqrt(var + 1e-5) * lg_ref[...] + lb_ref[...]


def _merge(x2d, p2d, oa, ob, oc, wm, wa, wb, wc, wo, wpg, wp, lg, lb, alpha, tm=256):
    m, d = x2d.shape
    tile = lambda wd: pl.BlockSpec((tm, wd), lambda i: (i, 0))
    full = lambda a: pl.BlockSpec(a.shape, lambda i: (0, 0))
    weights = (wm, wa, wb, wc, wo, wpg, wp, lg, lb)
    return pl.pallas_call(
        functools.partial(_merge_kernel, alpha=alpha),
        out_shape=jax.ShapeDtypeStruct((m, d), F32),
        grid=(m // tm,),
        in_specs=[tile(d), tile(p2d.shape[1]), tile(oa.shape[1]), tile(ob.shape[1]), tile(oc.shape[1])]
        + [full(a) for a in weights],
        out_specs=tile(d),
        compiler_params=_params("parallel"),
        name="merge_out_norm",
    )(x2d, p2d, oa, ob, oc, *weights)


def _nsa_out_perm():
    idx = np.arange(NSA_QW).reshape(NSA_GROUPS, NSA_HPG, NSA_DK)
    return idx.transpose(1, 0, 2).reshape(-1)


def _layer_weights(w_in, conv_w, a_log, dt_bias, norm_w, pe_k, pe_v, w1k, w2k, w1v, w2v, w_br_c, d_model):
    sizes = (RET_W, RET_W, RET_W, RET_W, 3 * GDN_W, GDN_HEADS, GDN_HEADS, GDN_W,
             NSA_QW, NSA_KW, NSA_KW, NSA_KW, NSA_KW, NSA_KW, NSA_KW, 3 * NSA_HEADS, NSA_QW, 3 * d_model)
    offs = np.concatenate([[0], np.cumsum(sizes)])
    col = lambda i: w_in[:, offs[i]:offs[i + 1]]
    perm = np.concatenate([hd * RET_D + np.concatenate([np.arange(0, RET_D, 2), np.arange(1, RET_D, 2)])
                           for hd in range(RET_HEADS)])
    w_ret = _bf(jnp.concatenate([col(0)[:, perm], col(1)[:, perm], col(2), col(3)], axis=1))
    small = jnp.concatenate([col(5), col(6)], axis=1)
    w_gdn = _bf(jnp.concatenate([col(4), col(7), jnp.pad(small, ((0, 0), (0, LANES - small.shape[1])))], axis=1))
    w_gdn_t = _bf(small.T)
    operm = _nsa_out_perm()
    w_nsa = _bf(jnp.concatenate([col(16)[:, operm], col(9), col(10), col(11), col(13)], axis=1))
    gates_t = jnp.pad(col(15).T, ((0, NSA_GATE_ROWS - 3 * NSA_HEADS), (0, 0)))
    w_nsa_t = _bf(jnp.concatenate([col(8).T, col(12).T, col(14).T, gates_t], axis=0))
    w_merge = _bf(col(17))
    lane_ids = jnp.arange(LANES)
    pick = lambda vec: jnp.where((lane_ids >= GDN_HEADS) & (lane_ids < 2 * GDN_HEADS),
                                 jnp.pad(vec, (GDN_HEADS, LANES - 2 * GDN_HEADS)), 0.0)[None, :]
    alog_row = pick(a_log.astype(F32))
    dtb_row = pick(dt_bias.astype(F32))
    alog_lanes = jnp.repeat(a_log.astype(F32), GDN_CHUNK)[None, :]
    dtb_lanes = jnp.repeat(dt_bias.astype(F32), GDN_CHUNK)[None, :]
    half = CMP_LEN // 2

    def big_w1(w1):
        w = w1.reshape(2, half, NSA_DK, CMP_HIDDEN)
        out = jnp.zeros((2, half, NSA_GROUPS, NSA_DK, NSA_GROUPS, CMP_HIDDEN), F32)
        for g in range(NSA_GROUPS):
            out = out.at[:, :, g, :, g, :].set(w)
        return _bf(out.reshape(2, half * NSA_GROUPS * NSA_DK, NSA_GROUPS * CMP_HIDDEN))

    def pe_rows(pe):
        t = jnp.broadcast_to(pe.reshape(2, half, 1, NSA_DK), (2, half, NSA_GROUPS, NSA_DK))
        return t.reshape(2, half * NSA_GROUPS * NSA_DK).astype(F32)

    return dict(w_ret=w_ret, w_gdn=w_gdn, w_gdn_t=w_gdn_t, w_nsa=w_nsa, w_nsa_t=w_nsa_t, w_merge=w_merge,
                alog_row=alog_row, dtb_row=dtb_row, alog_lanes=alog_lanes, dtb_lanes=dtb_lanes,
                norm_w=norm_w.astype(F32)[None, :], conv_w=conv_w.astype(F32),
                pek=pe_rows(pe_k), pev=pe_rows(pe_v), w1k=big_w1(w1k), w1v=big_w1(w1v), w2k=_bf(w2k),
                w2vt=_bf(w2v.T), w_br_c=_bf(w_br_c[operm, :]))


def _layer(x, p_i, lw, w_br_a, w_br_b, w_out, ln_g, ln_b, w_ple, w_ple_gate, ret_tabs, nsa_ovl, alpha):
    b, s, d = x.shape
    m = b * s
    x2d = x.reshape(m, d)
    r3 = lambda a: a.reshape(b, s, a.shape[-1])

    hq, hk, hv, hz = _proj(x2d, lw["w_ret"], (RET_W,) * 4, name="proj_ret")
    o_a = _retention(r3(hq), r3(hk), r3(hv), r3(hz), ret_tabs)

    qkv, gz, sm, smt = _proj_t(x2d, lw["w_gdn"], lw["w_gdn_t"], (3 * GDN_W, GDN_W, LANES), name="proj_gdn")
    nchunk = s // GDN_CHUNK
    smt = smt.reshape(2, GDN_HEADS, b, nchunk, GDN_CHUNK).transpose(2, 3, 0, 1, 4)
    smt = jnp.pad(smt.reshape(b, nchunk, 2, GDN_HEADS * GDN_CHUNK), ((0, 0), (0, 0), (0, 6), (0, 0)))
    o_b = _gdn(r3(qkv), r3(gz), r3(sm), smt, lw["conv_w"], lw["alog_row"], lw["dtb_row"], lw["alog_lanes"],
               lw["dtb_lanes"], lw["norm_w"])

    nz, kc, vc, ks, kw, xt = _proj_t(x2d, lw["w_nsa"], lw["w_nsa_t"], (NSA_QW,) + (NSA_KW,) * 4, name="proj_nsa")
    slab = lambda a: a.reshape(b, s // CMP_STRIDE, CMP_STRIDE * NSA_KW)
    kcb, vcbt = _compress(slab(kc), slab(vc), lw["pek"], lw["pev"], lw["w1k"], lw["w1v"], lw["w2k"], lw["w2vt"])
    o_c = _nsa_attention(xt, r3(nz), kcb, vcbt, r3(ks), r3(kw), nsa_ovl, b, s)

    out = _merge(x2d, p_i.reshape(m, -1), o_a.reshape(m, -1), o_b.reshape(m, -1), o_c.reshape(m, -1),
                 lw["w_merge"], _bf(w_br_a), _bf(w_br_b), lw["w_br_c"], _bf(w_out), _bf(w_ple_gate), _bf(w_ple),
                 ln_g.astype(F32)[None, :], ln_b.astype(F32)[None, :], alpha)
    return out.reshape(b, s, d)


def kernel(x, p, w_in, gdn_conv_w, gdn_a_log, gdn_dt_bias, gdn_norm_w, nsa_pe_k, nsa_pe_v, nsa_cmp_w1k, nsa_cmp_w2k, nsa_cmp_w1v, nsa_cmp_w2v, w_branch_a, w_branch_b, w_branch_c, w_out, ln_g, ln_b, w_ple, w_ple_gate):
    depth = w_in.shape[0]
    s, d = x.shape[1], x.shape[2]
    alpha = (2 * depth) ** 0.25
    ret_tabs = _ret_tables(s)
    nsa_ovl = _nsa_tables(s)
    for i in range(depth):
        lw = _layer_weights(w_in[i], gdn_conv_w[i], gdn_a_log[i], gdn_dt_bias[i], gdn_norm_w[i], nsa_pe_k[i],
                            nsa_pe_v[i], nsa_cmp_w1k[i], nsa_cmp_w2k[i], nsa_cmp_w1v[i], nsa_cmp_w2v[i],
                            w_branch_c[i], d)
        x = _layer(x, p[i], lw, w_branch_a[i], w_branch_b[i], w_out[i], ln_g[i], ln_b[i],
                   w_ple[i], w_ple_gate[i], ret_tabs, nsa_ovl, alpha)
    return x
```

```python
import functools

import numpy as np
import jax
import jax.numpy as jnp
from jax import lax
from jax.experimental import pallas as pl
from jax.experimental.pallas import tpu as pltpu

F32 = jnp.float32
BF16 = jnp.bfloat16

RET_HEADS, RET_D, RET_CHUNK = 4, 128, 128
ROPE_BASE = 10000.0
GDN_HEADS, GDN_D, GDN_CHUNK, CONV_K = 4, 128, 64, 4
NSA_HEADS, NSA_GROUPS, NSA_DK = 8, 2, 64
NSA_HPG = NSA_HEADS // NSA_GROUPS
CMP_LEN, CMP_STRIDE, CMP_HIDDEN = 32, 16, 128
SLC_BLOCK, N_SEL, WIN = 64, 8, 256
NEG, BIG, EPS = -1e30, 1e30, 1e-6
LANES = 128
Q_TILE = 128
GDN_UNROLL = 4
VMEM_LIMIT = 56 * 1024 * 1024

RET_W = RET_HEADS * RET_D
GDN_W = GDN_HEADS * GDN_D
NSA_QW = NSA_HEADS * NSA_DK
NSA_KW = NSA_GROUPS * NSA_DK
NSA_GATE_ROWS = 32
NSA_T_ROWS = NSA_QW + 2 * NSA_KW + NSA_GATE_ROWS


def _bf(x):
    return x.astype(BF16)


def _mm(a, b):
    return jnp.dot(_bf(a), _bf(b), preferred_element_type=F32)


def _mm_nt(a, b):
    return lax.dot_general(_bf(a), _bf(b), (((1,), (1,)), ((), ())), preferred_element_type=F32)


def _split3(x):
    hi = _bf(x)
    r1 = x - hi.astype(F32)
    mid = _bf(r1)
    lo = _bf(r1 - mid.astype(F32))
    return hi, mid, lo


def _sigmoid(x):
    return 0.5 * jnp.tanh(0.5 * x) + 0.5


def _silu(x):
    return x * _sigmoid(x)


def _softplus(x):
    return jnp.maximum(x, 0.0) + jnp.log(1.0 + jnp.exp(-jnp.abs(x)))


def _params(*sem):
    return pltpu.CompilerParams(dimension_semantics=sem, vmem_limit_bytes=VMEM_LIMIT)


def _proj_kernel(x_ref, w_ref, *o_refs, widths):
    x = _bf(x_ref[...])
    off = 0
    for o_ref, wd in zip(o_refs, widths):
        o_ref[...] = jnp.dot(x, w_ref[:, off:off + wd], preferred_element_type=F32).astype(o_ref.dtype)
        off += wd


def _proj(x2d, w, widths, tm=512, name="proj"):
    m, k = x2d.shape
    n = w.shape[1]
    assert sum(widths) == n and m % tm == 0
    return pl.pallas_call(
        functools.partial(_proj_kernel, widths=tuple(widths)),
        out_shape=[jax.ShapeDtypeStruct((m, wd), F32) for wd in widths],
        grid=(m // tm,),
        in_specs=[pl.BlockSpec((tm, k), lambda i: (i, 0)), pl.BlockSpec((k, n), lambda i: (0, 0))],
        out_specs=[pl.BlockSpec((tm, wd), lambda i: (i, 0)) for wd in widths],
        compiler_params=_params("parallel"),
        name=name,
    )(x2d, w)


def _proj_t_kernel(x_ref, w_ref, wt_ref, *o_refs, widths):
    x = _bf(x_ref[...])
    off = 0
    for o_ref, wd in zip(o_refs[:-1], widths):
        o_ref[...] = jnp.dot(x, w_ref[:, off:off + wd], preferred_element_type=F32)
        off += wd
    o_refs[-1][...] = lax.dot_general(wt_ref[...], x, (((1,), (1,)), ((), ())), preferred_element_type=F32)


def _proj_t(x2d, w, wt, widths, tm=512, name="proj_t"):
    m, k = x2d.shape
    n = w.shape[1]
    nt = wt.shape[0]
    assert sum(widths) == n and m % tm == 0
    return pl.pallas_call(
        functools.partial(_proj_t_kernel, widths=tuple(widths)),
        out_shape=[jax.ShapeDtypeStruct((m, wd), F32) for wd in widths] + [jax.ShapeDtypeStruct((nt, m), F32)],
        grid=(m // tm,),
        in_specs=[pl.BlockSpec((tm, k), lambda i: (i, 0)), pl.BlockSpec((k, n), lambda i: (0, 0)),
                  pl.BlockSpec((nt, k), lambda i: (0, 0))],
        out_specs=[pl.BlockSpec((tm, wd), lambda i: (i, 0)) for wd in widths]
        + [pl.BlockSpec((nt, tm), lambda i: (0, i))],
        compiler_params=_params("parallel"),
        name=name,
    )(x2d, w, wt)


def _ret_kernel(q_ref, k_ref, v_ref, z_ref, cos_ref, sin_ref, dmask_ref, qdec_ref, kdec_ref, cdec_ref, o_ref,
                state_s):
    ts = q_ref.shape[0]
    c = RET_CHUNK
    half = RET_D // 2
    scale = RET_D ** -0.5

    @pl.when(pl.program_id(1) == 0)
    def _():
        state_s[...] = jnp.zeros_like(state_s)

    def body(n, carry):
        sl = pl.ds(pl.multiple_of(n * c, c), c)
        cos = cos_ref[sl, :]
        sin = sin_ref[sl, :]
        for h in range(RET_HEADS):
            hs = slice(h * RET_D, (h + 1) * RET_D)
            q = q_ref[sl, hs]
            k = k_ref[sl, hs]
            v = v_ref[sl, hs]
            state = state_s[h]
            q = q * cos + pltpu.roll(q, half, 1) * sin
            k = (k * cos + pltpu.roll(k, half, 1) * sin) * scale
            scores = _mm_nt(q, k) * dmask_ref[h]
            o = _mm(scores, v) + _mm(q * qdec_ref[h], state)
            state_s[h] = cdec_ref[h] * state + _mm((k * kdec_ref[h]).T, v)
            mu = jnp.mean(o, axis=-1, keepdims=True)
            d = o - mu
            var = jnp.mean(d * d, axis=-1, keepdims=True)
            o_ref[sl, hs] = d * lax.rsqrt(var + 1e-5) * _silu(z_ref[sl, hs])
        return carry

    lax.fori_loop(0, ts // c, body, 0, unroll=True)


def _retention(hq, hk, hv, hz, tabs, ts=512):
    b, s, _ = hq.shape
    cos, sin, dmask, qdec, kdec, cdec = tabs
    seq = lambda: pl.BlockSpec((None, ts, RET_W), lambda i, t: (i, t, 0))
    tab = lambda: pl.BlockSpec((RET_HEADS, RET_CHUNK, RET_D), lambda i, t: (0, 0, 0))
    rot = lambda: pl.BlockSpec((ts, RET_D), lambda i, t: (t, 0))
    return pl.pallas_call(
        _ret_kernel,
        out_shape=jax.ShapeDtypeStruct((b, s, RET_W), F32),
        grid=(b, s // ts),
        in_specs=[seq(), seq(), seq(), seq(), rot(), rot(), tab(), tab(), tab(), tab()],
        out_specs=seq(),
        scratch_shapes=[pltpu.VMEM((RET_HEADS, RET_D, RET_D), F32)],
        compiler_params=_params("parallel", "arbitrary"),
        name="retention",
    )(hq, hk, hv, hz, cos, sin, dmask, qdec, kdec, cdec)


def _ret_tables(s):
    inv = ROPE_BASE ** (-jnp.arange(0, RET_D, 2, dtype=F32) / RET_D)
    ang = jnp.arange(s, dtype=F32)[:, None] * inv[None, :]
    cos = jnp.concatenate([jnp.cos(ang), jnp.cos(ang)], axis=-1)
    sin = jnp.concatenate([-jnp.sin(ang), jnp.sin(ang)], axis=-1)
    c = RET_CHUNK
    log_g = jnp.log1p(-jnp.exp2(-5.0 - jnp.arange(RET_HEADS, dtype=F32)))
    pos = jnp.arange(c, dtype=F32)
    diff = pos[:, None] - pos[None, :]
    causal = diff >= 0
    dmask = jnp.where(causal[None], jnp.exp(jnp.where(causal, diff, 0.0)[None] * log_g[:, None, None]), 0.0)
    ones = jnp.ones((1, 1, RET_D), F32)
    kdec = jnp.exp((c - 1 - pos)[None, :, None] * log_g[:, None, None]) * ones
    qdec = jnp.exp((pos + 1.0)[None, :, None] * log_g[:, None, None]) * ones
    cdec = jnp.exp(c * log_g)[:, None, None] * jnp.ones((1, c, RET_D), F32)
    return cos, sin, dmask, qdec, kdec, cdec


def _gdn_kernel(qkv_ref, z_ref, sm_ref, smt_ref, cw_ref, alog_ref, dtb_ref, alog_l_ref, dtb_l_ref, nw_ref, o_ref,
                state_s, xs_s, h_qe, h_kd, h_mp, h_r, h_at, h_dec, c_qe, c_kd, c_mp, c_r, c_at, c_dec,
                lhs_s, ou_s, psi_s, dec_s, *, nt):
    ts = qkv_ref.shape[0]
    c, d, nh = GDN_CHUNK, GDN_D, GDN_HEADS
    hc = nh * c
    nct = ts // c
    g = pl.program_id(0)
    ri = lax.broadcasted_iota(jnp.int32, (hc, hc), 0)
    ci = lax.broadcasted_iota(jnp.int32, (hc, hc), 1)
    same = (ri // c) == (ci // c)
    incl = same & (ri >= ci)
    strict = same & (ri > ci)
    upper = same & (ri <= ci)
    neg_a = -jnp.exp(alog_ref[...])
    dtb = dtb_ref[...]
    neg_a_l = -jnp.exp(alog_l_ref[...])
    dtb_l = dtb_l_ref[...]
    nw = nw_ref[...]

    @pl.when(g == 0)
    def _():
        for ref in (state_s, h_qe, h_kd, h_mp, h_r, h_at, h_dec):
            ref[...] = jnp.zeros_like(ref)

    @pl.when(g % nt == 0)
    def _():
        xs_s[0:8, :] = jnp.zeros((8, xs_s.shape[1]), F32)

    xs_s[8:8 + ts, :] = qkv_ref[...]
    for dst, src in ((c_qe, h_qe), (c_kd, h_kd), (c_mp, h_mp), (c_r, h_r), (c_at, h_at), (c_dec, h_dec)):
        dst[...] = src[...]

    def conv_silu_stack(n, part):
        start = pl.multiple_of(n * c, c)
        cols = slice(part * nh * d, (part + 1) * nh * d)
        w = cw_ref[:, cols]
        ext = xs_s[pl.ds(start, c + 8), cols]
        y = ext[8:8 + c, :] * w[CONV_K - 1:CONV_K, :]
        for j in range(1, CONV_K):
            y = y + ext[8 - j:8 - j + c, :] * w[CONV_K - 1 - j:CONV_K - j, :]
        y = _silu(y)
        return [y[:, h * d:(h + 1) * d] for h in range(nh)]

    def l2n(xs, mult):
        out = [x * (lax.rsqrt(jnp.sum(x * x, axis=-1, keepdims=True) + EPS) * mult) for x in xs]
        return jnp.concatenate(out, axis=0)

    def prep(it, carry):
        chunks = [it * GDN_UNROLL + u for u in range(GDN_UNROLL)]
        prev = [dict(qe=c_qe[n], kd=c_kd[n], mp=c_mp[n], r=c_r[n], at=c_at[n], dec=c_dec[n]) for n in chunks]
        st = []
        for n in chunks:
            sl = pl.ds(pl.multiple_of(n * c, c), c)
            q = l2n(conv_silu_stack(n, 0), d ** -0.5)
            k = l2n(conv_silu_stack(n, 1), 1.0)
            v = jnp.concatenate(conv_silu_stack(n, 2), axis=0)
            sm = sm_ref[sl, :]
            beta_col = jnp.concatenate([_sigmoid(sm[:, h:h + 1]) for h in range(nh)], axis=0)
            g_col = jnp.concatenate(
                [neg_a[:, nh + h:nh + h + 1] * _softplus(sm[:, nh + h:nh + h + 1] + dtb[:, nh + h:nh + h + 1])
                 for h in range(nh)], axis=0)
            smt = smt_ref[n]
            g_row = neg_a_l * _softplus(smt[1:2, :] + dtb_l)
            g_row_b = jnp.broadcast_to(g_row, (hc, hc))
            gc_col = jnp.sum(jnp.where(incl, g_row_b, 0.0), axis=1, keepdims=True)
            gc_row = jnp.sum(jnp.where(upper, jnp.broadcast_to(g_col, (hc, hc)), 0.0), axis=0, keepdims=True)
            g_last = jnp.sum(jnp.where(same, g_row_b, 0.0), axis=1, keepdims=True)
            decay = jnp.where(incl, jnp.exp(jnp.where(incl, gc_col - gc_row, 0.0)), 0.0)
            egc = jnp.exp(gc_col)
            st.append(dict(q=q, k=k, v=v, beta=beta_col, gc_col=gc_col, g_last=g_last, decay=decay, egc=egc))
        for s_ in st:
            qk_kk = _mm_nt(jnp.concatenate([s_["q"], s_["k"]], axis=0), s_["k"])
            s_["attn"] = qk_kk[:hc] * s_["decay"]
            s_["mp"] = _bf(-jnp.where(strict, s_["beta"] * qk_kk[hc:] * s_["decay"], 0.0))
            s_["r"] = jnp.concatenate([s_["v"] * s_["beta"], s_["k"] * (s_["beta"] * s_["egc"])], axis=1)
        for p_ in prev:
            p_["r"] = p_["r"] + jnp.dot(p_["mp"], _bf(p_["r"]), preferred_element_type=F32)
        for _ in range(5):
            for p_ in prev:
                p_["mp"] = _bf(jnp.dot(p_["mp"], p_["mp"], preferred_element_type=F32))
            for p_ in prev:
                p_["r"] = p_["r"] + jnp.dot(p_["mp"], _bf(p_["r"]), preferred_element_type=F32)
        for n, p_ in zip(chunks, prev):
            rb = _bf(p_["r"])
            ar = jnp.dot(p_["at"], rb, preferred_element_type=F32)
            qt = p_["qe"] - ar[:, d:]
            for h in range(nh):
                rows = slice(h * c, (h + 1) * c)
                kr = jnp.dot(_bf(p_["kd"][rows].T), rb[rows], preferred_element_type=F32)
                lhs_s[h, n, 0:c, :] = _bf(qt[rows])
                lhs_s[h, n, c:c + d, :] = _bf(-kr[:, d:])
                ou_s[h, n] = ar[rows, :d]
                psi_s[h, n] = kr[:, :d]
                dec_s[h, n] = p_["dec"][h]
        for n, s_ in zip(chunks, st):
            h_qe[n] = s_["q"] * s_["egc"]
            h_kd[n] = s_["k"] * jnp.exp(s_["g_last"] - s_["gc_col"])
            h_mp[n] = s_["mp"]
            h_r[n] = s_["r"]
            h_at[n] = _bf(s_["attn"])
            for h in range(nh):
                h_dec[n, h] = jnp.broadcast_to(jnp.exp(s_["g_last"][h * c:h * c + 1, :]), (8, d))
        return carry

    lax.fori_loop(0, nct // GDN_UNROLL, prep, 0)

    @pl.when((g + nt - 1) % nt == 0)
    def _():
        state_s[...] = jnp.zeros_like(state_s)

    def scan(n, carry):
        sl = pl.ds(pl.multiple_of(n * c, c), c)
        for h in range(nh):
            hs = slice(h * d, (h + 1) * d)
            state = state_s[h]
            res = jnp.dot(lhs_s[h, n], _bf(state), preferred_element_type=F32)
            o = res[:c] + ou_s[h, n]
            state_s[h] = dec_s[h, n][0:1, :] * state + res[c:] + psi_s[h, n]
            o = o * lax.rsqrt(jnp.mean(o * o, axis=-1, keepdims=True) + EPS) * nw
            o_ref[sl, hs] = o * _silu(z_ref[sl, hs])
        return carry

    lax.fori_loop(0, nct, scan, 0)
    xs_s[0:8, :] = qkv_ref[ts - 8:ts, :]


def _gdn(qkv, z, sm, smt, conv_w, alog_row, dtb_row, alog_lanes, dtb_lanes, norm_w, s, ts=512):
    m, wq = qkv.shape
    nh, c, d = GDN_HEADS, GDN_CHUNK, GDN_D
    hc = nh * c
    nct = ts // c
    nt = s // ts
    ntile = m // ts
    assert nct % GDN_UNROLL == 0 and s % ts == 0
    front = lambda wd: pl.BlockSpec((ts, wd), lambda g: (jnp.minimum(g, ntile - 1), 0))
    back = lambda wd: pl.BlockSpec((ts, wd), lambda g: (jnp.maximum(g - 1, 0), 0))
    full = lambda a: pl.BlockSpec(a.shape, lambda g, n=a.ndim: (0,) * n)
    return pl.pallas_call(
        functools.partial(_gdn_kernel, nt=nt),
        out_shape=jax.ShapeDtypeStruct((m, GDN_W), F32),
        grid=(ntile + 1,),
        in_specs=[front(wq), back(GDN_W), front(LANES),
                  pl.BlockSpec((nct, 8, hc), lambda g: (jnp.minimum(g, ntile - 1), 0, 0)),
                  full(conv_w), full(alog_row), full(dtb_row), full(alog_lanes), full(dtb_lanes), full(norm_w)],
        out_specs=back(GDN_W),
        scratch_shapes=[pltpu.VMEM((nh, d, d), F32),
                        pltpu.VMEM((ts + 8, wq), F32),
                        pltpu.VMEM((nct, hc, d), F32),
                        pltpu.VMEM((nct, hc, d), F32),
                        pltpu.VMEM((nct, hc, hc), BF16),
                        pltpu.VMEM((nct, hc, 2 * d), F32),
                        pltpu.VMEM((nct, hc, hc), BF16),
                        pltpu.VMEM((nct, nh, 8, d), F32),
                        pltpu.VMEM((nct, hc, d), F32),
                        pltpu.VMEM((nct, hc, d), F32),
                        pltpu.VMEM((nct, hc, hc), BF16),
                        pltpu.VMEM((nct, hc, 2 * d), F32),
                        pltpu.VMEM((nct, hc, hc), BF16),
                        pltpu.VMEM((nct, nh, 8, d), F32),
                        pltpu.VMEM((nh, nct, c + d, d), BF16),
                        pltpu.VMEM((nh, nct, c, d), F32),
                        pltpu.VMEM((nh, nct, d, d), F32),
                        pltpu.VMEM((nh, nct, 8, d), F32)],
        compiler_params=_params("arbitrary"),
        name="gated_delta_net",
    )(qkv, z, sm, smt, conv_w, alog_row, dtb_row, alog_lanes, dtb_lanes, norm_w)


def _cmp_kernel(yk_ref, yv_ref, pek_ref, pev_ref, w1k_ref, w1v_ref, w2k_ref, w2vt_ref, kc_ref, vct_ref):
    nrow = yk_ref.shape[0]

    def hidden(y_ref, pe_ref, w1_ref):
        y = y_ref[...]
        first = _mm(y + pe_ref[0:1, :], w1_ref[0])
        second = _mm(y + pe_ref[1:2, :], w1_ref[1])
        pre = first + pltpu.roll(second, nrow - 1, 0)
        return jax.nn.gelu(pre)

    hk = hidden(yk_ref, pek_ref, w1k_ref)
    kc_ref[...] = jnp.concatenate(
        [_mm(hk[:, g * CMP_HIDDEN:(g + 1) * CMP_HIDDEN], w2k_ref[...]) for g in range(NSA_GROUPS)], axis=1)
    hv = hidden(yv_ref, pev_ref, w1v_ref)
    vct_ref[...] = jnp.concatenate(
        [_mm_nt(w2vt_ref[...], hv[:, g * CMP_HIDDEN:(g + 1) * CMP_HIDDEN]) for g in range(NSA_GROUPS)], axis=0)


def _compress(yk, yv, pek, pev, w1k, w1v, w2k, w2vt):
    b, nrow, width = yk.shape
    hid = NSA_GROUPS * CMP_HIDDEN
    full = lambda shape: pl.BlockSpec(shape, lambda i, n=len(shape): (0,) * n)
    return pl.pallas_call(
        _cmp_kernel,
        out_shape=[jax.ShapeDtypeStruct((b, nrow, NSA_KW), F32), jax.ShapeDtypeStruct((b, NSA_KW, nrow), F32)],
        grid=(b,),
        in_specs=[pl.BlockSpec((None, nrow, width), lambda i: (i, 0, 0)),
                  pl.BlockSpec((None, nrow, width), lambda i: (i, 0, 0)),
                  full((2, width)), full((2, width)), full((2, width, hid)), full((2, width, hid)),
                  full((CMP_HIDDEN, NSA_DK)), full((NSA_DK, CMP_HIDDEN))],
        out_specs=[pl.BlockSpec((None, nrow, NSA_KW), lambda i: (i, 0, 0)),
                   pl.BlockSpec((None, NSA_KW, nrow), lambda i: (i, 0, 0))],
        compiler_params=_params("parallel"),
        name="nsa_compress",
    )(yk, yv, pek, pev, w1k, w1v, w2k, w2vt)


LOG2E = 1.4426950408889634


def _exp2_cols(s):
    e = jnp.exp2(s - jnp.max(s, axis=0, keepdims=True))
    return e, 1.0 / jnp.sum(e, axis=0, keepdims=True)


def _nsa_kernel(xt_ref, z_ref, kc_ref, vct_ref, ks_ref, vst_ref, kw_ref, vwt_ref, ovl_ref, o_ref, sel_s,
                s_even, s_odd, p_even, p_odd, acc_s):
    tq = z_ref.shape[0]
    s_len = ks_ref.shape[0]
    ncmp = kc_ref.shape[0]
    nslc = s_len // SLC_BLOCK
    qt = pl.program_id(1)
    t0 = qt * tq
    scale = NSA_DK ** -0.5 * LOG2E
    ng, hp, nh = NSA_GROUPS, NSA_HPG, NSA_HEADS
    cols = nh * tq
    t_row = t0 + lax.broadcasted_iota(jnp.int32, (1, tq), 1)
    sub = lax.broadcasted_iota(jnp.int32, (LANES, tq), 0)
    low = sub < NSA_DK
    gates = _sigmoid(xt_ref[NSA_QW + 2 * NSA_KW:NSA_QW + 2 * NSA_KW + NSA_GATE_ROWS, :])
    tile = lambda a, n: jnp.concatenate([a] * n, axis=1)
    zeros_half = jnp.zeros((NSA_DK, tq), F32)

    qs = []
    for head in range(nh):
        qh = xt_ref[head * NSA_DK:(head + 1) * NSA_DK, :] * scale
        qs.append(jnp.concatenate([qh, zeros_half] if head // hp == 0 else [zeros_half, qh], axis=0))
    q8 = _bf(jnp.concatenate(qs, axis=1))

    nwin = WIN + tq
    w0 = pl.multiple_of(jnp.maximum(t0 - WIN, 0), LANES)
    kpos = w0 + lax.broadcasted_iota(jnp.int32, (nwin, 1), 0)
    dq = t_row - kpos
    wmask8 = tile((dq >= 0) & (dq < WIN), nh)
    s_w = jnp.dot(_bf(kw_ref[pl.ds(w0, nwin), :]), q8, preferred_element_type=F32)
    e_w, r_w = _exp2_cols(jnp.where(wmask8, s_w, NEG))
    o_win = jnp.dot(_bf(vwt_ref[:, pl.ds(w0, nwin)]), _bf(e_w), preferred_element_type=F32) * r_w

    c_end =lax.broadcasted_iota(jnp.int32, (ncmp, 1), 0) * CMP_STRIDE + (CMP_LEN - 1)
    cmask8 = tile(c_end <= t_row, nh)
    s_c = jnp.dot(_bf(kc_ref[...]), q8, preferred_element_type=F32)
    e_c, r_c = _exp2_cols(jnp.where(cmask8, s_c, NEG))
    p_c = jnp.where(cmask8, e_c * r_c, 0.0)
    o_cmp = jnp.dot(_bf(vct_ref[...]), _bf(p_c), preferred_element_type=F32)

    ovl = ovl_ref[...]
    jb = lax.broadcasted_iota(jnp.int32, (nslc, tq), 0)
    cur = t_row // SLC_BLOCK
    forced = (jb == 0) | (jb == cur) | (jb == cur - 1)
    for g in range(ng):
        p_sum = p_c[:, g * hp * tq:(g * hp + 1) * tq]
        for hh in range(1, hp):
            p_sum = p_sum + p_c[:, (g * hp + hh) * tq:(g * hp + hh + 1) * tq]
        imp = sum(jnp.dot(ovl, part, preferred_element_type=F32) for part in _split3(p_sum))
        rank = jnp.where(jb <= cur, jnp.where(forced, BIG, imp), -BIG)
        cnt = jnp.zeros((nslc, tq), F32)
        for i in range(nslc):
            ri = rank[i:i + 1, :]
            beats = (ri > rank) | ((ri == rank) & (jb > i))
            cnt = cnt + jnp.where(beats, 1.0, 0.0)
        sel = jnp.where(cnt < float(min(N_SEL, nslc)), 1.0, 0.0)
        for kt in range(s_len // LANES):
            sel_s[g, kt, 0:2, :] = sel[2 * kt:2 * kt + 2, :]

    def qk(kt):
        return jnp.dot(_bf(ks_ref[pl.ds(pl.multiple_of(kt * LANES, LANES), LANES), :]), q8,
                       preferred_element_type=F32)

    def pv(kt, p):
        return jnp.dot(_bf(vst_ref[:, pl.ds(pl.multiple_of(kt * LANES, LANES), LANES)]), p,
                       preferred_element_type=F32)

    nkt = s_len // LANES

    def one_tile(kt, m_run, l_run, s_rd, s_wr, p_rd, p_wr):
        s_wr[...] = qk(jnp.minimum(kt + 1, nkt - 1))
        pv_prev = pv(jnp.maximum(kt - 1, 0), p_rd[...])
        causal = (kt * LANES + sub) <= t_row
        biases = []
        for g in range(ng):
            pair = sel_s[g, kt, 0:2, :]
            keep = (jnp.where(low, pair[0:1, :], pair[1:2, :]) > 0.5) & causal
            biases.append(tile(jnp.where(keep, 0.0, NEG), hp))
        s_s = s_rd[...] + jnp.concatenate(biases, axis=1)
        m_new = jnp.maximum(m_run, jnp.max(s_s, axis=0, keepdims=True))
        alpha = jnp.exp2(m_run - m_new)
        p = jnp.exp2(s_s - m_new)
        p_wr[...] = _bf(p)
        acc_s[...] = alpha * (acc_s[...] + pv_prev)
        return m_new, alpha * l_run + jnp.sum(p, axis=0, keepdims=True)

    def slc_body(j, carry):
        m_run, l_run = one_tile(2 * j, *carry, s_even, s_odd, p_odd, p_even)
        return one_tile(2 * j + 1, m_run, l_run, s_odd, s_even, p_even, p_odd)

    s_even[...] = qk(0)
    p_odd[...] = jnp.zeros((LANES, cols), BF16)
    acc_s[...] = jnp.zeros((LANES, cols), F32)
    npair = qt // 2 + 1
    _, l_s = lax.fori_loop(0, npair, slc_body, (jnp.full((1, cols), NEG, F32), jnp.zeros((1, cols), F32)))
    o_slc = (acc_s[...] + pv(2 * npair - 1, p_odd[...])) * (1.0 / l_s)

    def head_out(head):
        cs = slice(head * tq, (head + 1) * tq)
        row = 3 * head
        return (gates[row:row + 1, :] * o_cmp[:, cs] + gates[row + 1:row + 2, :] * o_slc[:, cs]
                + gates[row + 2:row + 3, :] * o_win[:, cs])

    for hh in range(hp):
        both = jnp.where(low, head_out(hh), head_out(hp + hh))
        cs = slice(hh * LANES, (hh + 1) * LANES)
        o_ref[:, cs] = both.T * _silu(z_ref[:, cs])


def _nsa_attention(xt, z, kcb, vcbt, ks, kw, ovl, b, s):
    tq = Q_TILE
    ncmp = kcb.shape[1]
    nqt = s // tq
    nslc = s // SLC_BLOCK
    vrow = NSA_QW // NSA_KW
    return pl.pallas_call(
        _nsa_kernel,
        out_shape=jax.ShapeDtypeStruct((b, s, NSA_QW), F32),
        grid=(b, nqt),
        in_specs=[pl.BlockSpec((NSA_T_ROWS, tq), lambda i, t: (0, i * nqt + t)),
                  pl.BlockSpec((None, tq, NSA_QW), lambda i, t: (i, t, 0)),
                  pl.BlockSpec((None, ncmp, NSA_KW), lambda i, t: (i, 0, 0)),
                  pl.BlockSpec((None, NSA_KW, ncmp), lambda i, t: (i, 0, 0)),
                  pl.BlockSpec((None, s, NSA_KW), lambda i, t: (i, 0, 0)),
                  pl.BlockSpec((NSA_KW, s), lambda i, t: (vrow, i)),
                  pl.BlockSpec((None, s, NSA_KW), lambda i, t: (i, 0, 0)),
                  pl.BlockSpec((NSA_KW, s), lambda i, t: (vrow + 1, i)),
                  pl.BlockSpec(ovl.shape, lambda i, t: (0, 0))],
        out_specs=pl.BlockSpec((None, tq, NSA_QW), lambda i, t: (i, t, 0)),
        scratch_shapes=[pltpu.VMEM((NSA_GROUPS, s // LANES, 8, tq), F32),
                        pltpu.VMEM((LANES, NSA_HEADS * tq), F32),
                        pltpu.VMEM((LANES, NSA_HEADS * tq), F32),
                        pltpu.VMEM((LANES, NSA_HEADS * tq), BF16),
                        pltpu.VMEM((LANES, NSA_HEADS * tq), BF16),
                        pltpu.VMEM((LANES, NSA_HEADS * tq), F32)],
        compiler_params=_params("parallel", "arbitrary"),
        name="nsa_attention",
    )(xt, z, kcb, vcbt, ks, xt, kw, xt, ovl)


def _nsa_tables(s):
    ncmp_rows = s // CMP_STRIDE
    nslc = s // SLC_BLOCK
    ci = np.arange(ncmp_rows)[None, :] * CMP_STRIDE
    sj = np.arange(nslc)[:, None] * SLC_BLOCK
    ovl = ((ci < sj + SLC_BLOCK) & (ci + CMP_LEN > sj)).astype(np.float32)
    return jnp.asarray(ovl, BF16)


def _merge_kernel(x_ref, p_ref, oa_ref, ob_ref, oc_ref, wm_ref, wa_ref, wb_ref, wc_ref, wo_ref, wpg_ref, wp_ref,
                  lg_ref, lb_ref, o_ref, *, alpha):
    d = x_ref.shape[1]
    x = x_ref[...]
    xb = _bf(x)
    mixed = None
    for idx, (br_ref, w_ref) in enumerate(((oa_ref, wa_ref), (ob_ref, wb_ref), (oc_ref, wc_ref))):
        gate = _sigmoid(jnp.dot(xb, wm_ref[:, idx * d:(idx + 1) * d], preferred_element_type=F32))
        term = gate * _mm(br_ref[...], w_ref[...])
        mixed = term if mixed is None else mixed + term
    sub = _mm(mixed, wo_ref[...])
    ple = _sigmoid(jnp.dot(xb, wpg_ref[...], preferred_element_type=F32)) * _mm(p_ref[...], wp_ref[...])
    y = alpha * x + sub + ple
    mu = jnp.mean(y, axis=-1, keepdims=True)
    yc = y - mu
    var = jnp.mean(yc * yc, axis=-1, keepdims=True)
    o_ref[...] = yc * lax.rsqrt(var + 1e-5) * lg_ref[...] + lb_ref[...]


def _merge(x2d, p2d, oa, ob, oc, wm, wa, wb, wc, wo, wpg, wp, lg, lb, alpha, tm=256):
    m, d = x2d.shape
    tile = lambda wd: pl.BlockSpec((tm, wd), lambda i: (i, 0))
    full = lambda a: pl.BlockSpec(a.shape, lambda i: (0, 0))
    weights = (wm, wa, wb, wc, wo, wpg, wp, lg, lb)
    return pl.pallas_call(
        functools.partial(_merge_kernel, alpha=alpha),
        out_shape=jax.ShapeDtypeStruct((m, d), F32),
        grid=(m // tm,),
        in_specs=[tile(d), tile(p2d.shape[1]), tile(oa.shape[1]), tile(ob.shape[1]), tile(oc.shape[1])]
        + [full(a) for a in weights],
        out_specs=tile(d),
        compiler_params=_params("parallel"),
        name="merge_out_norm",
    )(x2d, p2d, oa, ob, oc, *weights)


def _nsa_out_perm():
    idx = np.arange(NSA_QW).reshape(NSA_GROUPS, NSA_HPG, NSA_DK)
    return idx.transpose(1, 0, 2).reshape(-1)


def _layer_weights(w_in, conv_w, a_log, dt_bias, norm_w, pe_k, pe_v, w1k, w2k, w1v, w2v, w_br_c, d_model):
    sizes = (RET_W, RET_W, RET_W, RET_W, 3 * GDN_W, GDN_HEADS, GDN_HEADS, GDN_W,
             NSA_QW, NSA_KW, NSA_KW, NSA_KW, NSA_KW, NSA_KW, NSA_KW, 3 * NSA_HEADS, NSA_QW, 3 * d_model)
    offs = np.concatenate([[0], np.cumsum(sizes)])
    col = lambda i: w_in[:, offs[i]:offs[i + 1]]
    perm = np.concatenate([hd * RET_D + np.concatenate([np.arange(0, RET_D, 2), np.arange(1, RET_D, 2)])
                           for hd in range(RET_HEADS)])
    w_ret = _bf(jnp.concatenate([col(0)[:, perm], col(1)[:, perm], col(2), col(3)], axis=1))
    small = jnp.concatenate([col(5), col(6)], axis=1)
    w_gdn = _bf(jnp.concatenate([col(4), col(7), jnp.pad(small, ((0, 0), (0, LANES - small.shape[1])))], axis=1))
    w_gdn_t = _bf(small.T)
    operm = _nsa_out_perm()
    w_nsa = _bf(jnp.concatenate([col(16)[:, operm], col(9), col(10), col(11), col(13)], axis=1))
    gates_t = jnp.pad(col(15).T, ((0, NSA_GATE_ROWS - 3 * NSA_HEADS), (0, 0)))
    w_nsa_t = _bf(jnp.concatenate([col(8).T, col(12).T, col(14).T, gates_t], axis=0))
    w_merge = _bf(col(17))
    lane_ids = jnp.arange(LANES)
    pick = lambda vec: jnp.where((lane_ids >= GDN_HEADS) & (lane_ids < 2 * GDN_HEADS),
                                 jnp.pad(vec, (GDN_HEADS, LANES - 2 * GDN_HEADS)), 0.0)[None, :]
    alog_row = pick(a_log.astype(F32))
    dtb_row = pick(dt_bias.astype(F32))
    alog_lanes = jnp.repeat(a_log.astype(F32), GDN_CHUNK)[None, :]
    dtb_lanes = jnp.repeat(dt_bias.astype(F32), GDN_CHUNK)[None, :]
    half = CMP_LEN // 2

    def big_w1(w1):
        w = w1.reshape(2, half, 1, NSA_DK, 1, CMP_HIDDEN)
        eye = jnp.eye(NSA_GROUPS, dtype=F32).reshape(1, 1, NSA_GROUPS, 1, NSA_GROUPS, 1)
        return _bf((w * eye).reshape(2, half * NSA_GROUPS * NSA_DK, NSA_GROUPS * CMP_HIDDEN))

    def pe_rows(pe):
        t = jnp.broadcast_to(pe.reshape(2, half, 1, NSA_DK), (2, half, NSA_GROUPS, NSA_DK))
        return t.reshape(2, half * NSA_GROUPS * NSA_DK).astype(F32)

    return dict(w_ret=w_ret, w_gdn=w_gdn, w_gdn_t=w_gdn_t, w_nsa=w_nsa, w_nsa_t=w_nsa_t, w_merge=w_merge,
                alog_row=alog_row, dtb_row=dtb_row, alog_lanes=alog_lanes, dtb_lanes=dtb_lanes,
                norm_w=norm_w.astype(F32)[None, :], conv_w=conv_w.astype(F32),
                pek=pe_rows(pe_k), pev=pe_rows(pe_v), w1k=big_w1(w1k), w1v=big_w1(w1v), w2k=_bf(w2k),
                w2vt=_bf(w2v.T), w_br_c=_bf(w_br_c[operm, :]))


def _layer(x, p_i, lw, w_br_a, w_br_b, w_out, ln_g, ln_b, w_ple, w_ple_gate, ret_tabs, nsa_ovl, alpha):
    b, s, d = x.shape
    m = b * s
    x2d = x.reshape(m, d)
    r3 = lambda a: a.reshape(b, s, a.shape[-1])

    hq, hk, hv, hz = _proj(x2d, lw["w_ret"], (RET_W,) * 4, name="proj_ret")
    o_a = _retention(r3(hq), r3(hk), r3(hv), r3(hz), ret_tabs)

    qkv, gz, sm, smt = _proj_t(x2d, lw["w_gdn"], lw["w_gdn_t"], (3 * GDN_W, GDN_W, LANES), name="proj_gdn")
    nchunk = s // GDN_CHUNK
    smt = smt.reshape(2, GDN_HEADS, b, nchunk, GDN_CHUNK).transpose(2, 3, 0, 1, 4)
    smt = jnp.pad(smt.reshape(b * nchunk, 2, GDN_HEADS * GDN_CHUNK), ((0, 0), (0, 6), (0, 0)))
    o_b = _gdn(qkv, gz, sm, smt, lw["conv_w"], lw["alog_row"], lw["dtb_row"], lw["alog_lanes"],
               lw["dtb_lanes"], lw["norm_w"], s)

    nz, kc, vc, ks, kw, xt = _proj_t(x2d, lw["w_nsa"], lw["w_nsa_t"], (NSA_QW,) + (NSA_KW,) * 4, name="proj_nsa")
    slab = lambda a: a.reshape(b, s // CMP_STRIDE, CMP_STRIDE * NSA_KW)
    kcb, vcbt = _compress(slab(kc), slab(vc), lw["pek"], lw["pev"], lw["w1k"], lw["w1v"], lw["w2k"], lw["w2vt"])
    o_c = _nsa_attention(xt, r3(nz), kcb, vcbt, r3(ks), r3(kw), nsa_ovl, b, s)

    out = _merge(x2d, p_i.reshape(m, -1), o_a.reshape(m, -1), o_b.reshape(m, -1), o_c.reshape(m, -1),
                 lw["w_merge"], _bf(w_br_a), _bf(w_br_b), lw["w_br_c"], _bf(w_out), _bf(w_ple_gate), _bf(w_ple),
                 ln_g.astype(F32)[None, :], ln_b.astype(F32)[None, :], alpha)
    return out.reshape(b, s, d)


def kernel(x, p, w_in, gdn_conv_w, gdn_a_log, gdn_dt_bias, gdn_norm_w, nsa_pe_k, nsa_pe_v, nsa_cmp_w1k, nsa_cmp_w2k, nsa_cmp_w1v, nsa_cmp_w2v, w_branch_a, w_branch_b, w_branch_c, w_out, ln_g, ln_b, w_ple, w_ple_gate):
    depth = w_in.shape[0]
    s, d = x.shape[1], x.shape[2]
    alpha = (2 * depth) ** 0.25
    ret_tabs = _ret_tables(s)
    nsa_ovl = _nsa_tables(s)
    for i in range(depth):
        lw = _layer_weights(w_in[i], gdn_conv_w[i], gdn_a_log[i], gdn_dt_bias[i], gdn_norm_w[i], nsa_pe_k[i],
                            nsa_pe_v[i], nsa_cmp_w1k[i], nsa_cmp_w2k[i], nsa_cmp_w1v[i], nsa_cmp_w2v[i],
                            w_branch_c[i], d)
        x = _layer(x, p[i], lw, w_branch_a[i], w_branch_b[i], w_out[i], ln_g[i], ln_b[i],
                   w_ple[i], w_ple_gate[i], ret_tabs, nsa_ovl, alpha)
    return x
```

```python
import functools

import numpy as np
import jax
import jax.numpy as jnp
from jax import lax
from jax.experimental import pallas as pl
from jax.experimental.pallas import tpu as pltpu

F32 = jnp.float32
BF16 = jnp.bfloat16

RET_HEADS, RET_D, RET_CHUNK = 4, 128, 128
ROPE_BASE = 10000.0
GDN_HEADS, GDN_D, GDN_CHUNK, CONV_K = 4, 128, 64, 4
NSA_HEADS, NSA_GROUPS, NSA_DK = 8, 2, 64
NSA_HPG = NSA_HEADS // NSA_GROUPS
CMP_LEN, CMP_STRIDE, CMP_HIDDEN = 32, 16, 128
SLC_BLOCK, N_SEL, WIN = 64, 8, 256
NEG, BIG, EPS = -1e30, 1e30, 1e-6
LANES = 128
Q_TILE = 256
GDN_UNROLL = 4
VMEM_LIMIT = 56 * 1024 * 1024

RET_W = RET_HEADS * RET_D
GDN_W = GDN_HEADS * GDN_D
NSA_QW = NSA_HEADS * NSA_DK
NSA_KW = NSA_GROUPS * NSA_DK
NSA_GATE_ROWS = 32
NSA_T_ROWS = NSA_QW + 2 * NSA_KW + NSA_GATE_ROWS


def _bf(x):
    return x.astype(BF16)


def _mm(a, b):
    return jnp.dot(_bf(a), _bf(b), preferred_element_type=F32)


def _mm_nt(a, b):
    return lax.dot_general(_bf(a), _bf(b), (((1,), (1,)), ((), ())), preferred_element_type=F32)


def _split3(x):
    hi = _bf(x)
    r1 = x - hi.astype(F32)
    mid = _bf(r1)
    lo = _bf(r1 - mid.astype(F32))
    return hi, mid, lo


def _sigmoid(x):
    return 0.5 * jnp.tanh(0.5 * x) + 0.5


def _silu(x):
    return x * _sigmoid(x)


def _softplus(x):
    return jnp.maximum(x, 0.0) + jnp.log(1.0 + jnp.exp(-jnp.abs(x)))


def _params(*sem):
    return pltpu.CompilerParams(dimension_semantics=sem, vmem_limit_bytes=VMEM_LIMIT)


def _proj_kernel(x_ref, w_ref, *o_refs, widths):
    x = _bf(x_ref[...])
    off = 0
    for o_ref, wd in zip(o_refs, widths):
        o_ref[...] = jnp.dot(x, w_ref[:, off:off + wd], preferred_element_type=F32).astype(o_ref.dtype)
        off += wd


def _proj(x2d, w, widths, tm=512, name="proj"):
    m, k = x2d.shape
    n = w.shape[1]
    assert sum(widths) == n and m % tm == 0
    return pl.pallas_call(
        functools.partial(_proj_kernel, widths=tuple(widths)),
        out_shape=[jax.ShapeDtypeStruct((m, wd), F32) for wd in widths],
        grid=(m // tm,),
        in_specs=[pl.BlockSpec((tm, k), lambda i: (i, 0)), pl.BlockSpec((k, n), lambda i: (0, 0))],
        out_specs=[pl.BlockSpec((tm, wd), lambda i: (i, 0)) for wd in widths],
        compiler_params=_params("parallel"),
        name=name,
    )(x2d, w)


def _proj_t_kernel(x_ref, w_ref, wt_ref, *o_refs, widths):
    x = _bf(x_ref[...])
    off = 0
    for o_ref, wd in zip(o_refs[:-1], widths):
        o_ref[...] = jnp.dot(x, w_ref[:, off:off + wd], preferred_element_type=F32)
        off += wd
    o_refs[-1][...] = lax.dot_general(wt_ref[...], x, (((1,), (1,)), ((), ())), preferred_element_type=F32)


def _proj_t(x2d, w, wt, widths, tm=512, name="proj_t"):
    m, k = x2d.shape
    n = w.shape[1]
    nt = wt.shape[0]
    assert sum(widths) == n and m % tm == 0
    return pl.pallas_call(
        functools.partial(_proj_t_kernel, widths=tuple(widths)),
        out_shape=[jax.ShapeDtypeStruct((m, wd), F32) for wd in widths] + [jax.ShapeDtypeStruct((nt, m), F32)],
        grid=(m // tm,),
        in_specs=[pl.BlockSpec((tm, k), lambda i: (i, 0)), pl.BlockSpec((k, n), lambda i: (0, 0)),
                  pl.BlockSpec((nt, k), lambda i: (0, 0))],
        out_specs=[pl.BlockSpec((tm, wd), lambda i: (i, 0)) for wd in widths]
        + [pl.BlockSpec((nt, tm), lambda i: (0, i))],
        compiler_params=_params("parallel"),
        name=name,
    )(x2d, w, wt)


def _ret_kernel(q_ref, k_ref, v_ref, z_ref, cos_ref, sin_ref, dmask_ref, qdec_ref, kdec_ref, cdec_ref, o_ref,
                state_s):
    ts = q_ref.shape[0]
    c = RET_CHUNK
    half = RET_D // 2
    scale = RET_D ** -0.5

    @pl.when(pl.program_id(1) == 0)
    def _():
        state_s[...] = jnp.zeros_like(state_s)

    def body(n, carry):
        sl = pl.ds(pl.multiple_of(n * c, c), c)
        cos = cos_ref[sl, :]
        sin = sin_ref[sl, :]
        for h in range(RET_HEADS):
            hs = slice(h * RET_D, (h + 1) * RET_D)
            q = q_ref[sl, hs]
            k = k_ref[sl, hs]
            v = v_ref[sl, hs]
            state = state_s[h]
            q = q * cos + pltpu.roll(q, half, 1) * sin
            k = (k * cos + pltpu.roll(k, half, 1) * sin) * scale
            scores = _mm_nt(q, k) * dmask_ref[h]
            o = _mm(scores, v) + _mm(q * qdec_ref[h], state)
            state_s[h] = cdec_ref[h] * state + _mm((k * kdec_ref[h]).T, v)
            mu = jnp.mean(o, axis=-1, keepdims=True)
            d = o - mu
            var = jnp.mean(d * d, axis=-1, keepdims=True)
            o_ref[sl, hs] = d * lax.rsqrt(var + 1e-5) * _silu(z_ref[sl, hs])
        return carry

    lax.fori_loop(0, ts // c, body, 0, unroll=True)


def _retention(hq, hk, hv, hz, tabs, ts=512):
    b, s, _ = hq.shape
    cos, sin, dmask, qdec, kdec, cdec = tabs
    seq = lambda: pl.BlockSpec((None, ts, RET_W), lambda i, t: (i, t, 0))
    tab = lambda: pl.BlockSpec((RET_HEADS, RET_CHUNK, RET_D), lambda i, t: (0, 0, 0))
    rot = lambda: pl.BlockSpec((ts, RET_D), lambda i, t: (t, 0))
    return pl.pallas_call(
        _ret_kernel,
        out_shape=jax.ShapeDtypeStruct((b, s, RET_W), F32),
        grid=(b, s // ts),
        in_specs=[seq(), seq(), seq(), seq(), rot(), rot(), tab(), tab(), tab(), tab()],
        out_specs=seq(),
        scratch_shapes=[pltpu.VMEM((RET_HEADS, RET_D, RET_D), F32)],
        compiler_params=_params("parallel", "arbitrary"),
        name="retention",
    )(hq, hk, hv, hz, cos, sin, dmask, qdec, kdec, cdec)


def _ret_tables(s):
    inv = ROPE_BASE ** (-jnp.arange(0, RET_D, 2, dtype=F32) / RET_D)
    ang = jnp.arange(s, dtype=F32)[:, None] * inv[None, :]
    cos = jnp.concatenate([jnp.cos(ang), jnp.cos(ang)], axis=-1)
    sin = jnp.concatenate([-jnp.sin(ang), jnp.sin(ang)], axis=-1)
    c = RET_CHUNK
    log_g = jnp.log1p(-jnp.exp2(-5.0 - jnp.arange(RET_HEADS, dtype=F32)))
    pos = jnp.arange(c, dtype=F32)
    diff = pos[:, None] - pos[None, :]
    causal = diff >= 0
    dmask = jnp.where(causal[None], jnp.exp(jnp.where(causal, diff, 0.0)[None] * log_g[:, None, None]), 0.0)
    ones = jnp.ones((1, 1, RET_D), F32)
    kdec = jnp.exp((c - 1 - pos)[None, :, None] * log_g[:, None, None]) * ones
    qdec = jnp.exp((pos + 1.0)[None, :, None] * log_g[:, None, None]) * ones
    cdec = jnp.exp(c * log_g)[:, None, None] * jnp.ones((1, c, RET_D), F32)
    return cos, sin, dmask, qdec, kdec, cdec


def _gdn_kernel(qkv_ref, z_ref, sm_ref, smt_ref, cw_ref, alog_ref, dtb_ref, alog_l_ref, dtb_l_ref, nw_ref, o_ref,
                state_s, xs_s, h_qe, h_kd, h_mp, h_r, h_at, h_dec, c_qe, c_kd, c_mp, c_r, c_at, c_dec,
                lhs_s, ou_s, psi_s, dec_s, *, nt):
    ts = qkv_ref.shape[0]
    c, d, nh = GDN_CHUNK, GDN_D, GDN_HEADS
    hc = nh * c
    nct = ts // c
    g = pl.program_id(0)
    ri = lax.broadcasted_iota(jnp.int32, (hc, hc), 0)
    ci = lax.broadcasted_iota(jnp.int32, (hc, hc), 1)
    same = (ri // c) == (ci // c)
    incl = same & (ri >= ci)
    strict = same & (ri > ci)
    upper = same & (ri <= ci)
    neg_a = -jnp.exp(alog_ref[...])
    dtb = dtb_ref[...]
    neg_a_l = -jnp.exp(alog_l_ref[...])
    dtb_l = dtb_l_ref[...]
    nw = nw_ref[...]

    @pl.when(g == 0)
    def _():
        for ref in (state_s, h_qe, h_kd, h_mp, h_r, h_at, h_dec):
            ref[...] = jnp.zeros_like(ref)

    @pl.when(g % nt == 0)
    def _():
        xs_s[0:8, :] = jnp.zeros((8, xs_s.shape[1]), F32)

    xs_s[8:8 + ts, :] = qkv_ref[...]
    for dst, src in ((c_qe, h_qe), (c_kd, h_kd), (c_mp, h_mp), (c_r, h_r), (c_at, h_at), (c_dec, h_dec)):
        dst[...] = src[...]

    def conv_silu_stack(n, part):
        start = pl.multiple_of(n * c, c)
        cols = slice(part * nh * d, (part + 1) * nh * d)
        w = cw_ref[:, cols]
        ext = xs_s[pl.ds(start, c + 8), cols]
        y = ext[8:8 + c, :] * w[CONV_K - 1:CONV_K, :]
        for j in range(1, CONV_K):
            y = y + ext[8 - j:8 - j + c, :] * w[CONV_K - 1 - j:CONV_K - j, :]
        y = _silu(y)
        return [y[:, h * d:(h + 1) * d] for h in range(nh)]

    def l2n(xs, mult):
        out = [x * (lax.rsqrt(jnp.sum(x * x, axis=-1, keepdims=True) + EPS) * mult) for x in xs]
        return jnp.concatenate(out, axis=0)

    def prep(it, carry):
        chunks = [it * GDN_UNROLL + u for u in range(GDN_UNROLL)]
        prev = [dict(qe=c_qe[n], kd=c_kd[n], mp=c_mp[n], r=c_r[n], at=c_at[n], dec=c_dec[n]) for n in chunks]
        st = []
        for n in chunks:
            sl = pl.ds(pl.multiple_of(n * c, c), c)
            q = l2n(conv_silu_stack(n, 0), d ** -0.5)
            k = l2n(conv_silu_stack(n, 1), 1.0)
            v = jnp.concatenate(conv_silu_stack(n, 2), axis=0)
            sm = sm_ref[sl, :]
            beta_col = jnp.concatenate([_sigmoid(sm[:, h:h + 1]) for h in range(nh)], axis=0)
            g_col = jnp.concatenate(
                [neg_a[:, nh + h:nh + h + 1] * _softplus(sm[:, nh + h:nh + h + 1] + dtb[:, nh + h:nh + h + 1])
                 for h in range(nh)], axis=0)
            smt = smt_ref[n]
            g_row = neg_a_l * _softplus(smt[1:2, :] + dtb_l)
            g_row_b = jnp.broadcast_to(g_row, (hc, hc))
            gc_col = jnp.sum(jnp.where(incl, g_row_b, 0.0), axis=1, keepdims=True)
            gc_row = jnp.sum(jnp.where(upper, jnp.broadcast_to(g_col, (hc, hc)), 0.0), axis=0, keepdims=True)
            g_last = jnp.sum(jnp.where(same, g_row_b, 0.0), axis=1, keepdims=True)
            decay = jnp.where(incl, jnp.exp(jnp.where(incl, gc_col - gc_row, 0.0)), 0.0)
            egc = jnp.exp(gc_col)
            st.append(dict(q=q, k=k, v=v, beta=beta_col, gc_col=gc_col, g_last=g_last, decay=decay, egc=egc))
        for s_ in st:
            qk_kk = _mm_nt(jnp.concatenate([s_["q"], s_["k"]], axis=0), s_["k"])
            s_["attn"] = qk_kk[:hc] * s_["decay"]
            s_["mp"] = _bf(-jnp.where(strict, s_["beta"] * qk_kk[hc:] * s_["decay"], 0.0))
            s_["r"] = jnp.concatenate([s_["v"] * s_["beta"], s_["k"] * (s_["beta"] * s_["egc"])], axis=1)
        for p_ in prev:
            p_["r"] = p_["r"] + jnp.dot(p_["mp"], _bf(p_["r"]), preferred_element_type=F32)
        for _ in range(5):
            for p_ in prev:
                p_["mp"] = _bf(jnp.dot(p_["mp"], p_["mp"], preferred_element_type=F32))
            for p_ in prev:
                p_["r"] = p_["r"] + jnp.dot(p_["mp"], _bf(p_["r"]), preferred_element_type=F32)
        for n, p_ in zip(chunks, prev):
            rb = _bf(p_["r"])
            ar = jnp.dot(p_["at"], rb, preferred_element_type=F32)
            qt = p_["qe"] - ar[:, d:]
            for h in range(nh):
                rows = slice(h * c, (h + 1) * c)
                kr = jnp.dot(_bf(p_["kd"][rows].T), rb[rows], preferred_element_type=F32)
                lhs_s[h, n, 0:c, :] = _bf(qt[rows])
                lhs_s[h, n, c:c + d, :] = _bf(-kr[:, d:])
                ou_s[h, n] = ar[rows, :d]
                psi_s[h, n] = kr[:, :d]
                dec_s[h, n] = p_["dec"][h]
        for n, s_ in zip(chunks, st):
            h_qe[n] = s_["q"] * s_["egc"]
            h_kd[n] = s_["k"] * jnp.exp(s_["g_last"] - s_["gc_col"])
            h_mp[n] = s_["mp"]
            h_r[n] = s_["r"]
            h_at[n] = _bf(s_["attn"])
            for h in range(nh):
                h_dec[n, h] = jnp.broadcast_to(jnp.exp(s_["g_last"][h * c:h * c + 1, :]), (8, d))
        return carry

    lax.fori_loop(0, nct // GDN_UNROLL, prep, 0)

    @pl.when((g + nt - 1) % nt == 0)
    def _():
        state_s[...] = jnp.zeros_like(state_s)

    def scan(n, carry):
        sl = pl.ds(pl.multiple_of(n * c, c), c)
        for h in range(nh):
            hs = slice(h * d, (h + 1) * d)
            state = state_s[h]
            res = jnp.dot(lhs_s[h, n], _bf(state), preferred_element_type=F32)
            o = res[:c] + ou_s[h, n]
            state_s[h] = dec_s[h, n][0:1, :] * state + res[c:] + psi_s[h, n]
            o = o * lax.rsqrt(jnp.mean(o * o, axis=-1, keepdims=True) + EPS) * nw
            o_ref[sl, hs] = o * _silu(z_ref[sl, hs])
        return carry

    lax.fori_loop(0, nct, scan, 0)
    xs_s[0:8, :] = qkv_ref[ts - 8:ts, :]


def _gdn(qkv, z, sm, smt, conv_w, alog_row, dtb_row, alog_lanes, dtb_lanes, norm_w, s, ts=512):
    m, wq = qkv.shape
    nh, c, d = GDN_HEADS, GDN_CHUNK, GDN_D
    hc = nh * c
    nct = ts // c
    nt = s // ts
    ntile = m // ts
    assert nct % GDN_UNROLL == 0 and s % ts == 0
    front = lambda wd: pl.BlockSpec((ts, wd), lambda g: (jnp.minimum(g, ntile - 1), 0))
    back = lambda wd: pl.BlockSpec((ts, wd), lambda g: (jnp.maximum(g - 1, 0), 0))
    full = lambda a: pl.BlockSpec(a.shape, lambda g, n=a.ndim: (0,) * n)
    return pl.pallas_call(
        functools.partial(_gdn_kernel, nt=nt),
        out_shape=jax.ShapeDtypeStruct((m, GDN_W), F32),
        grid=(ntile + 1,),
        in_specs=[front(wq), back(GDN_W), front(LANES),
                  pl.BlockSpec((nct, 8, hc), lambda g: (jnp.minimum(g, ntile - 1), 0, 0)),
                  full(conv_w), full(alog_row), full(dtb_row), full(alog_lanes), full(dtb_lanes), full(norm_w)],
        out_specs=back(GDN_W),
        scratch_shapes=[pltpu.VMEM((nh, d, d), F32),
                        pltpu.VMEM((ts + 8, wq), F32),
                        pltpu.VMEM((nct, hc, d), F32),
                        pltpu.VMEM((nct, hc, d), F32),
                        pltpu.VMEM((nct, hc, hc), BF16),
                        pltpu.VMEM((nct, hc, 2 * d), F32),
                        pltpu.VMEM((nct, hc, hc), BF16),
                        pltpu.VMEM((nct, nh, 8, d), F32),
                        pltpu.VMEM((nct, hc, d), F32),
                        pltpu.VMEM((nct, hc, d), F32),
                        pltpu.VMEM((nct, hc, hc), BF16),
                        pltpu.VMEM((nct, hc, 2 * d), F32),
                        pltpu.VMEM((nct, hc, hc), BF16),
                        pltpu.VMEM((nct, nh, 8, d), F32),
                        pltpu.VMEM((nh, nct, c + d, d), BF16),
                        pltpu.VMEM((nh, nct, c, d), F32),
                        pltpu.VMEM((nh, nct, d, d), F32),
                        pltpu.VMEM((nh, nct, 8, d), F32)],
        compiler_params=_params("arbitrary"),
        name="gated_delta_net",
    )(qkv, z, sm, smt, conv_w, alog_row, dtb_row, alog_lanes, dtb_lanes, norm_w)


def _cmp_kernel(yk_ref, yv_ref, pek_ref, pev_ref, w1k_ref, w1v_ref, w2k_ref, w2vt_ref, kc_ref, vct_ref):
    nrow = yk_ref.shape[0]

    def hidden(y_ref, pe_ref, w1_ref):
        y = y_ref[...]
        first = _mm(y + pe_ref[0:1, :], w1_ref[0])
        second = _mm(y + pe_ref[1:2, :], w1_ref[1])
        pre = first + pltpu.roll(second, nrow - 1, 0)
        return jax.nn.gelu(pre)

    hk = hidden(yk_ref, pek_ref, w1k_ref)
    kc_ref[...] = jnp.concatenate(
        [_mm(hk[:, g * CMP_HIDDEN:(g + 1) * CMP_HIDDEN], w2k_ref[...]) for g in range(NSA_GROUPS)], axis=1)
    hv = hidden(yv_ref, pev_ref, w1v_ref)
    vct_ref[...] = jnp.concatenate(
        [_mm_nt(w2vt_ref[...], hv[:, g * CMP_HIDDEN:(g + 1) * CMP_HIDDEN]) for g in range(NSA_GROUPS)], axis=0)


def _compress(yk, yv, pek, pev, w1k, w1v, w2k, w2vt):
    b, nrow, width = yk.shape
    hid = NSA_GROUPS * CMP_HIDDEN
    full = lambda shape: pl.BlockSpec(shape, lambda i, n=len(shape): (0,) * n)
    return pl.pallas_call(
        _cmp_kernel,
        out_shape=[jax.ShapeDtypeStruct((b, nrow, NSA_KW), F32), jax.ShapeDtypeStruct((b, NSA_KW, nrow), F32)],
        grid=(b,),
        in_specs=[pl.BlockSpec((None, nrow, width), lambda i: (i, 0, 0)),
                  pl.BlockSpec((None, nrow, width), lambda i: (i, 0, 0)),
                  full((2, width)), full((2, width)), full((2, width, hid)), full((2, width, hid)),
                  full((CMP_HIDDEN, NSA_DK)), full((NSA_DK, CMP_HIDDEN))],
        out_specs=[pl.BlockSpec((None, nrow, NSA_KW), lambda i: (i, 0, 0)),
                   pl.BlockSpec((None, NSA_KW, nrow), lambda i: (i, 0, 0))],
        compiler_params=_params("parallel"),
        name="nsa_compress",
    )(yk, yv, pek, pev, w1k, w1v, w2k, w2vt)


LOG2E = 1.4426950408889634


def _exp2_cols(s):
    return jnp.exp2(s - jnp.max(s, axis=0, keepdims=True))


def _with_ones(vt):
    return jnp.concatenate([vt, jnp.ones((16, vt.shape[1]), BF16)], axis=0)


def _nsa_kernel(xt_ref, z_ref, kc_ref, vct_ref, ks_ref, vst_ref, kw_ref, vwt_ref, ovl_ref, o_ref, sel_s,
                s_even, s_odd, p_even, p_odd, acc_s):
    tq = z_ref.shape[0]
    s_len = ks_ref.shape[0]
    ncmp = kc_ref.shape[0]
    nslc = s_len // SLC_BLOCK
    qt = pl.program_id(1)
    t0 = qt * tq
    scale = NSA_DK ** -0.5 * LOG2E
    ng, hp, nh = NSA_GROUPS, NSA_HPG, NSA_HEADS
    cols = nh * tq
    t_row = t0 + lax.broadcasted_iota(jnp.int32, (1, tq), 1)
    sub = lax.broadcasted_iota(jnp.int32, (LANES, tq), 0)
    low = sub < NSA_DK
    gates = _sigmoid(xt_ref[NSA_QW + 2 * NSA_KW:NSA_QW + 2 * NSA_KW + NSA_GATE_ROWS, :])
    tile = lambda a, n: jnp.concatenate([a] * n, axis=1)
    zeros_half = jnp.zeros((NSA_DK, tq), F32)
    dk = NSA_DK

    def pv_groups(prep, vt, p):
        return jnp.concatenate(
            [jnp.dot(prep(vt[g * dk:(g + 1) * dk]), p[:, g * hp * tq:(g + 1) * hp * tq], preferred_element_type=F32)
             for g in range(ng)], axis=1)

    qs = []
    for head in range(nh):
        qh = xt_ref[head * NSA_DK:(head + 1) * NSA_DK, :] * scale
        qs.append(jnp.concatenate([qh, zeros_half] if head // hp == 0 else [zeros_half, qh], axis=0))
    q8 = _bf(jnp.concatenate(qs, axis=1))

    nwin = WIN + tq
    w0 = pl.multiple_of(jnp.maximum(t0 - WIN, 0), LANES)
    kpos = w0 + lax.broadcasted_iota(jnp.int32, (nwin, 1), 0)
    dq = t_row - kpos
    wmask8 = tile((dq >= 0) & (dq < WIN), nh)
    s_w = jnp.dot(_bf(kw_ref[pl.ds(w0, nwin), :]), q8, preferred_element_type=F32)
    e_w = _exp2_cols(jnp.where(wmask8, s_w, NEG))
    o_win = pv_groups(_with_ones, _bf(vwt_ref[:, pl.ds(w0, nwin)]), _bf(e_w))
    o_win = o_win[:dk] * (1.0 / o_win[dk:dk + 1])

    c_end =lax.broadcasted_iota(jnp.int32, (ncmp, 1), 0) * CMP_STRIDE + (CMP_LEN - 1)
    cmask8 = tile(c_end <= t_row, nh)
    s_c = jnp.dot(_bf(kc_ref[...]), q8, preferred_element_type=F32)
    e_c = _exp2_cols(jnp.where(cmask8, s_c, NEG))
    p_c = jnp.where(cmask8, e_c * (1.0 / jnp.sum(e_c, axis=0, keepdims=True)), 0.0)
    o_cmp = pv_groups(lambda v: v, _bf(vct_ref[...]), _bf(p_c))

    ovl = ovl_ref[...]
    jb = lax.broadcasted_iota(jnp.int32, (nslc, tq), 0)
    cur = t_row // SLC_BLOCK
    forced = (jb == 0) | (jb == cur) | (jb == cur - 1)
    for g in range(ng):
        p_sum = p_c[:, g * hp * tq:(g * hp + 1) * tq]
        for hh in range(1, hp):
            p_sum = p_sum + p_c[:, (g * hp + hh) * tq:(g * hp + hh + 1) * tq]
        imp = sum(jnp.dot(ovl, part, preferred_element_type=F32) for part in _split3(p_sum))
        rank = jnp.where(jb <= cur, jnp.where(forced, BIG, imp), -BIG)
        cnt = jnp.zeros((nslc, tq), F32)
        for i in range(nslc):
            ri = rank[i:i + 1, :]
            beats = (ri > rank) | ((ri == rank) & (jb > i))
            cnt = cnt + jnp.where(beats, 1.0, 0.0)
        sel = jnp.where(cnt < float(min(N_SEL, nslc)), 1.0, 0.0)
        for kt in range(s_len // LANES):
            sel_s[g, kt, 0:2, :] = sel[2 * kt:2 * kt + 2, :]

    def qk(kt):
        return jnp.dot(_bf(ks_ref[pl.ds(pl.multiple_of(kt * LANES, LANES), LANES), :]), q8,
                       preferred_element_type=F32)

    def pv(kt, p):
        return pv_groups(_with_ones, _bf(vst_ref[:, pl.ds(pl.multiple_of(kt * LANES, LANES), LANES)]), p)

    nkt = s_len // LANES

    def one_tile(kt, m_run, s_rd, s_wr, p_rd, p_wr):
        s_wr[...] = qk(jnp.minimum(kt + 1, nkt - 1))
        pv_prev = pv(jnp.maximum(kt - 1, 0), p_rd[...])
        causal = (kt * LANES + sub) <= t_row
        biases = []
        for g in range(ng):
            pair = sel_s[g, kt, 0:2, :]
            keep = (jnp.where(low, pair[0:1, :], pair[1:2, :]) > 0.5) & causal
            biases.append(tile(jnp.where(keep, 0.0, NEG), hp))
        s_s = s_rd[...] + jnp.concatenate(biases, axis=1)
        m_new = jnp.maximum(m_run, jnp.max(s_s, axis=0, keepdims=True))
        alpha = jnp.exp2(m_run - m_new)
        p = jnp.exp2(s_s - m_new)
        p_wr[...] = _bf(p)
        acc_s[...] = alpha * (acc_s[...] + pv_prev)
        return m_new

    def slc_body(j, m_run):
        m_run = one_tile(2 * j, m_run, s_even, s_odd, p_odd, p_even)
        return one_tile(2 * j + 1, m_run, s_odd, s_even, p_even, p_odd)

    s_even[...] = qk(0)
    p_odd[...] = jnp.zeros((LANES, cols), BF16)
    acc_s[...] = jnp.zeros(acc_s.shape, F32)
    npair = ((qt + 1) * tq // LANES + 1) // 2
    lax.fori_loop(0, npair, slc_body, jnp.full((1, cols), NEG, F32))
    o_slc = acc_s[...] + pv(2 * npair - 1, p_odd[...])
    o_slc = o_slc[:dk] * (1.0 / o_slc[dk:dk + 1])

    def head_out(head):
        cs = slice(head * tq, (head + 1) * tq)
        row = 3 * head
        return (gates[row:row + 1, :] * o_cmp[:, cs] + gates[row + 1:row + 2, :] * o_slc[:, cs]
                + gates[row + 2:row + 3, :] * o_win[:, cs])

    for hh in range(hp):
        both = jnp.concatenate([head_out(hh), head_out(hp + hh)], axis=0)
        cs = slice(hh * LANES, (hh + 1) * LANES)
        o_ref[:, cs] = both.T * _silu(z_ref[:, cs])


def _nsa_attention(xt, z, kcb, vcbt, ks, kw, ovl, b, s):
    tq = Q_TILE
    ncmp = kcb.shape[1]
    nqt = s // tq
    nslc = s // SLC_BLOCK
    vrow = NSA_QW // NSA_KW
    return pl.pallas_call(
        _nsa_kernel,
        out_shape=jax.ShapeDtypeStruct((b, s, NSA_QW), F32),
        grid=(b, nqt),
        in_specs=[pl.BlockSpec((NSA_T_ROWS, tq), lambda i, t: (0, i * nqt + t)),
                  pl.BlockSpec((None, tq, NSA_QW), lambda i, t: (i, t, 0)),
                  pl.BlockSpec((None, ncmp, NSA_KW), lambda i, t: (i, 0, 0)),
                  pl.BlockSpec((None, NSA_KW, ncmp), lambda i, t: (i, 0, 0)),
                  pl.BlockSpec((None, s, NSA_KW), lambda i, t: (i, 0, 0)),
                  pl.BlockSpec((NSA_KW, s), lambda i, t: (vrow, i)),
                  pl.BlockSpec((None, s, NSA_KW), lambda i, t: (i, 0, 0)),
                  pl.BlockSpec((NSA_KW, s), lambda i, t: (vrow + 1, i)),
                  pl.BlockSpec(ovl.shape, lambda i, t: (0, 0))],
        out_specs=pl.BlockSpec((None, tq, NSA_QW), lambda i, t: (i, t, 0)),
        scratch_shapes=[pltpu.VMEM((NSA_GROUPS, s // LANES, 8, tq), F32),
                        pltpu.VMEM((LANES, NSA_HEADS * tq), F32),
                        pltpu.VMEM((LANES, NSA_HEADS * tq), F32),
                        pltpu.VMEM((LANES, NSA_HEADS * tq), BF16),
                        pltpu.VMEM((LANES, NSA_HEADS * tq), BF16),
                        pltpu.VMEM((NSA_DK + 16, NSA_HEADS * tq), F32)],
        compiler_params=_params("parallel", "arbitrary"),
        name="nsa_attention",
    )(xt, z, kcb, vcbt, ks, xt, kw, xt, ovl)


def _nsa_tables(s):
    ncmp_rows = s // CMP_STRIDE
    nslc = s // SLC_BLOCK
    ci = np.arange(ncmp_rows)[None, :] * CMP_STRIDE
    sj = np.arange(nslc)[:, None] * SLC_BLOCK
    ovl = ((ci < sj + SLC_BLOCK) & (ci + CMP_LEN > sj)).astype(np.float32)
    return jnp.asarray(ovl, BF16)


def _merge_kernel(x_ref, p_ref, oa_ref, ob_ref, oc_ref, wm_ref, wa_ref, wb_ref, wc_ref, wo_ref, wpg_ref, wp_ref,
                  lg_ref, lb_ref, o_ref, *, alpha):
    d = x_ref.shape[1]
    x = x_ref[...]
    xb = _bf(x)
    mixed = None
    for idx, (br_ref, w_ref) in enumerate(((oa_ref, wa_ref), (ob_ref, wb_ref), (oc_ref, wc_ref))):
        gate = _sigmoid(jnp.dot(xb, wm_ref[:, idx * d:(idx + 1) * d], preferred_element_type=F32))
        term = gate * _mm(br_ref[...], w_ref[...])
        mixed = term if mixed is None else mixed + term
    sub = _mm(mixed, wo_ref[...])
    ple = _sigmoid(jnp.dot(xb, wpg_ref[...], preferred_element_type=F32)) * _mm(p_ref[...], wp_ref[...])
    y = alpha * x + sub + ple
    mu = jnp.mean(y, axis=-1, keepdims=True)
    yc = y - mu
    var = jnp.mean(yc * yc, axis=-1, keepdims=True)
    o_ref[...] = yc * lax.rsqrt(var + 1e-5) * lg_ref[...] + lb_ref[...]


def _merge(x2d, p2d, oa, ob, oc, wm, wa, wb, wc, wo, wpg, wp, lg, lb, alpha, tm=256):
    m, d = x2d.shape
    tile = lambda wd: pl.BlockSpec((tm, wd), lambda i: (i, 0))
    full = lambda a: pl.BlockSpec(a.shape, lambda i: (0, 0))
    weights = (wm, wa, wb, wc, wo, wpg, wp, lg, lb)
    return pl.pallas_call(
        functools.partial(_merge_kernel, alpha=alpha),
        out_shape=jax.ShapeDtypeStruct((m, d), F32),
        grid=(m // tm,),
        in_specs=[tile(d), tile(p2d.shape[1]), tile(oa.shape[1]), tile(ob.shape[1]), tile(oc.shape[1])]
        + [full(a) for a in weights],
        out_specs=tile(d),
        compiler_params=_params("parallel"),
        name="merge_out_norm",
    )(x2d, p2d, oa, ob, oc, *weights)


def _nsa_out_perm():
    idx = np.arange(NSA_QW).reshape(NSA_GROUPS, NSA_HPG, NSA_DK)
    return idx.transpose(1, 0, 2).reshape(-1)


def _layer_weights(w_in, conv_w, a_log, dt_bias, norm_w, pe_k, pe_v, w1k, w2k, w1v, w2v, w_br_c, d_model):
    sizes = (RET_W, RET_W, RET_W, RET_W, 3 * GDN_W, GDN_HEADS, GDN_HEADS, GDN_W,
             NSA_QW, NSA_KW, NSA_KW, NSA_KW, NSA_KW, NSA_KW, NSA_KW, 3 * NSA_HEADS, NSA_QW, 3 * d_model)
    offs = np.concatenate([[0], np.cumsum(sizes)])
    col = lambda i: w_in[:, offs[i]:offs[i + 1]]
    perm = np.concatenate([hd * RET_D + np.concatenate([np.arange(0, RET_D, 2), np.arange(1, RET_D, 2)])
                           for hd in range(RET_HEADS)])
    w_ret = _bf(jnp.concatenate([col(0)[:, perm], col(1)[:, perm], col(2), col(3)], axis=1))
    small = jnp.concatenate([col(5), col(6)], axis=1)
    w_gdn = _bf(jnp.concatenate([col(4), col(7), jnp.pad(small, ((0, 0), (0, LANES - small.shape[1])))], axis=1))
    w_gdn_t = _bf(small.T)
    operm = _nsa_out_perm()
    w_nsa = _bf(jnp.concatenate([col(16)[:, operm], col(9), col(10), col(11), col(13)], axis=1))
    gates_t = jnp.pad(col(15).T, ((0, NSA_GATE_ROWS - 3 * NSA_HEADS), (0, 0)))
    w_nsa_t = _bf(jnp.concatenate([col(8).T, col(12).T, col(14).T, gates_t], axis=0))
    w_merge = _bf(col(17))
    lane_ids = jnp.arange(LANES)
    pick = lambda vec: jnp.where((lane_ids >= GDN_HEADS) & (lane_ids < 2 * GDN_HEADS),
                                 jnp.pad(vec, (GDN_HEADS, LANES - 2 * GDN_HEADS)), 0.0)[None, :]
    alog_row = pick(a_log.astype(F32))
    dtb_row = pick(dt_bias.astype(F32))
    alog_lanes = jnp.repeat(a_log.astype(F32), GDN_CHUNK)[None, :]
    dtb_lanes = jnp.repeat(dt_bias.astype(F32), GDN_CHUNK)[None, :]
    half = CMP_LEN // 2

    def big_w1(w1):
        w = w1.reshape(2, half, 1, NSA_DK, 1, CMP_HIDDEN)
        eye = jnp.eye(NSA_GROUPS, dtype=F32).reshape(1, 1, NSA_GROUPS, 1, NSA_GROUPS, 1)
        return _bf((w * eye).reshape(2, half * NSA_GROUPS * NSA_DK, NSA_GROUPS * CMP_HIDDEN))

    def pe_rows(pe):
        t = jnp.broadcast_to(pe.reshape(2, half, 1, NSA_DK), (2, half, NSA_GROUPS, NSA_DK))
        return t.reshape(2, half * NSA_GROUPS * NSA_DK).astype(F32)

    return dict(w_ret=w_ret, w_gdn=w_gdn, w_gdn_t=w_gdn_t, w_nsa=w_nsa, w_nsa_t=w_nsa_t, w_merge=w_merge,
                alog_row=alog_row, dtb_row=dtb_row, alog_lanes=alog_lanes, dtb_lanes=dtb_lanes,
                norm_w=norm_w.astype(F32)[None, :], conv_w=conv_w.astype(F32),
                pek=pe_rows(pe_k), pev=pe_rows(pe_v), w1k=big_w1(w1k), w1v=big_w1(w1v), w2k=_bf(w2k),
                w2vt=_bf(w2v.T), w_br_c=_bf(w_br_c[operm, :]))


def _layer(x, p_i, lw, w_br_a, w_br_b, w_out, ln_g, ln_b, w_ple, w_ple_gate, ret_tabs, nsa_ovl, alpha):
    b, s, d = x.shape
    m = b * s
    x2d = x.reshape(m, d)
    r3 = lambda a: a.reshape(b, s, a.shape[-1])

    hq, hk, hv, hz = _proj(x2d, lw["w_ret"], (RET_W,) * 4, name="proj_ret")
    o_a = _retention(r3(hq), r3(hk), r3(hv), r3(hz), ret_tabs)

    qkv, gz, sm, smt = _proj_t(x2d, lw["w_gdn"], lw["w_gdn_t"], (3 * GDN_W, GDN_W, LANES), name="proj_gdn")
    nchunk = s // GDN_CHUNK
    smt = smt.reshape(2, GDN_HEADS, b, nchunk, GDN_CHUNK).transpose(2, 3, 0, 1, 4)
    smt = jnp.pad(smt.reshape(b * nchunk, 2, GDN_HEADS * GDN_CHUNK), ((0, 0), (0, 6), (0, 0)))
    o_b = _gdn(qkv, gz, sm, smt, lw["conv_w"], lw["alog_row"], lw["dtb_row"], lw["alog_lanes"],
               lw["dtb_lanes"], lw["norm_w"], s)

    nz, kc, vc, ks, kw, xt = _proj_t(x2d, lw["w_nsa"], lw["w_nsa_t"], (NSA_QW,) + (NSA_KW,) * 4, name="proj_nsa")
    slab = lambda a: a.reshape(b, s // CMP_STRIDE, CMP_STRIDE * NSA_KW)
    kcb, vcbt = _compress(slab(kc), slab(vc), lw["pek"], lw["pev"], lw["w1k"], lw["w1v"], lw["w2k"], lw["w2vt"])
    o_c = _nsa_attention(xt, r3(nz), kcb, vcbt, r3(ks), r3(kw), nsa_ovl, b, s)

    out = _merge(x2d, p_i.reshape(m, -1), o_a.reshape(m, -1), o_b.reshape(m, -1), o_c.reshape(m, -1),
                 lw["w_merge"], _bf(w_br_a), _bf(w_br_b), lw["w_br_c"], _bf(w_out), _bf(w_ple_gate), _bf(w_ple),
                 ln_g.astype(F32)[None, :], ln_b.astype(F32)[None, :], alpha)
    return out.reshape(b, s, d)


def kernel(x, p, w_in, gdn_conv_w, gdn_a_log, gdn_dt_bias, gdn_norm_w, nsa_pe_k, nsa_pe_v, nsa_cmp_w1k, nsa_cmp_w2k, nsa_cmp_w1v, nsa_cmp_w2v, w_branch_a, w_branch_b, w_branch_c, w_out, ln_g, ln_b, w_ple, w_ple_gate):
    depth = w_in.shape[0]
    s, d = x.shape[1], x.shape[2]
    alpha = (2 * depth) ** 0.25
    ret_tabs = _ret_tables(s)
    nsa_ovl = _nsa_tables(s)
    for i in range(depth):
        lw = _layer_weights(w_in[i], gdn_conv_w[i], gdn_a_log[i], gdn_dt_bias[i], gdn_norm_w[i], nsa_pe_k[i],
                            nsa_pe_v[i], nsa_cmp_w1k[i], nsa_cmp_w2k[i], nsa_cmp_w1v[i], nsa_cmp_w2v[i],
                            w_branch_c[i], d)
        x = _layer(x, p[i], lw, w_branch_a[i], w_branch_b[i], w_out[i], ln_g[i], ln_b[i],
                   w_ple[i], w_ple_gate[i], ret_tabs, nsa_ovl, alpha)
    return x
```

```python
import functools

import numpy as np
import jax
import jax.numpy as jnp
from jax import lax
from jax.experimental import pallas as pl
from jax.experimental.pallas import tpu as pltpu

F32 = jnp.float32
BF16 = jnp.bfloat16

RET_HEADS, RET_D, RET_CHUNK = 4, 128, 128
ROPE_BASE = 10000.0
GDN_HEADS, GDN_D, GDN_CHUNK, CONV_K = 4, 128, 64, 4
NSA_HEADS, NSA_GROUPS, NSA_DK = 8, 2, 64
NSA_HPG = NSA_HEADS // NSA_GROUPS
CMP_LEN, CMP_STRIDE, CMP_HIDDEN = 32, 16, 128
SLC_BLOCK, N_SEL, WIN = 64, 8, 256
NEG, BIG, EPS = -1e30, 1e30, 1e-6
LANES = 128
Q_TILE = 256
GDN_UNROLL = 4
VMEM_LIMIT = 56 * 1024 * 1024

RET_W = RET_HEADS * RET_D
GDN_W = GDN_HEADS * GDN_D
NSA_QW = NSA_HEADS * NSA_DK
NSA_KW = NSA_GROUPS * NSA_DK
NSA_GATE_ROWS = 32
NSA_T_ROWS = NSA_QW + 2 * NSA_KW + NSA_GATE_ROWS


def _bf(x):
    return x.astype(BF16)


def _mm(a, b):
    return jnp.dot(_bf(a), _bf(b), preferred_element_type=F32)


def _mm_nt(a, b):
    return lax.dot_general(_bf(a), _bf(b), (((1,), (1,)), ((), ())), preferred_element_type=F32)


def _split3(x):
    hi = _bf(x)
    r1 = x - hi.astype(F32)
    mid = _bf(r1)
    lo = _bf(r1 - mid.astype(F32))
    return hi, mid, lo


def _sigmoid(x):
    return 0.5 * jnp.tanh(0.5 * x) + 0.5


def _silu(x):
    return x * _sigmoid(x)


def _softplus(x):
    return jnp.maximum(x, 0.0) + jnp.log(1.0 + jnp.exp(-jnp.abs(x)))


def _params(*sem):
    return pltpu.CompilerParams(dimension_semantics=sem, vmem_limit_bytes=VMEM_LIMIT)


def _proj_kernel(x_ref, w_ref, *o_refs, widths):
    x = _bf(x_ref[...])
    off = 0
    for o_ref, wd in zip(o_refs, widths):
        o_ref[...] = jnp.dot(x, w_ref[:, off:off + wd], preferred_element_type=F32).astype(o_ref.dtype)
        off += wd


def _proj(x2d, w, widths, tm=512, name="proj"):
    m, k = x2d.shape
    n = w.shape[1]
    assert sum(widths) == n and m % tm == 0
    return pl.pallas_call(
        functools.partial(_proj_kernel, widths=tuple(widths)),
        out_shape=[jax.ShapeDtypeStruct((m, wd), F32) for wd in widths],
        grid=(m // tm,),
        in_specs=[pl.BlockSpec((tm, k), lambda i: (i, 0)), pl.BlockSpec((k, n), lambda i: (0, 0))],
        out_specs=[pl.BlockSpec((tm, wd), lambda i: (i, 0)) for wd in widths],
        compiler_params=_params("parallel"),
        name=name,
    )(x2d, w)


def _proj_t_kernel(x_ref, w_ref, wt_ref, *o_refs, widths):
    x = _bf(x_ref[...])
    off = 0
    for o_ref, wd in zip(o_refs[:-1], widths):
        o_ref[...] = jnp.dot(x, w_ref[:, off:off + wd], preferred_element_type=F32)
        off += wd
    o_refs[-1][...] = lax.dot_general(wt_ref[...], x, (((1,), (1,)), ((), ())), preferred_element_type=F32)


def _proj_t(x2d, w, wt, widths, tm=512, name="proj_t"):
    m, k = x2d.shape
    n = w.shape[1]
    nt = wt.shape[0]
    assert sum(widths) == n and m % tm == 0
    return pl.pallas_call(
        functools.partial(_proj_t_kernel, widths=tuple(widths)),
        out_shape=[jax.ShapeDtypeStruct((m, wd), F32) for wd in widths] + [jax.ShapeDtypeStruct((nt, m), F32)],
        grid=(m // tm,),
        in_specs=[pl.BlockSpec((tm, k), lambda i: (i, 0)), pl.BlockSpec((k, n), lambda i: (0, 0)),
                  pl.BlockSpec((nt, k), lambda i: (0, 0))],
        out_specs=[pl.BlockSpec((tm, wd), lambda i: (i, 0)) for wd in widths]
        + [pl.BlockSpec((nt, tm), lambda i: (0, i))],
        compiler_params=_params("parallel"),
        name=name,
    )(x2d, w, wt)


def _ret_kernel(x_ref, w_ref, cos_ref, sin_ref, dmask_ref, qdec_ref, kdec_ref, cdec_ref, o_ref, state_s):
    ts = x_ref.shape[0]
    c = RET_CHUNK
    half = RET_D // 2
    scale = RET_D ** -0.5

    @pl.when(pl.program_id(1) == 0)
    def _():
        state_s[...] = jnp.zeros_like(state_s)

    def project(n):
        return jnp.dot(_bf(x_ref[n * c:(n + 1) * c, :]), w_ref[...], preferred_element_type=F32)

    def chunk(n, hx):
        sl = slice(n * c, (n + 1) * c)
        cos = cos_ref[sl, :]
        sin = sin_ref[sl, :]
        for h in range(RET_HEADS):
            hs = slice(h * RET_D, (h + 1) * RET_D)
            q = hx[:, h * RET_D:(h + 1) * RET_D]
            k = hx[:, RET_W + h * RET_D:RET_W + (h + 1) * RET_D]
            v = hx[:, 2 * RET_W + h * RET_D:2 * RET_W + (h + 1) * RET_D]
            state = state_s[h]
            q = q * cos + pltpu.roll(q, half, 1) * sin
            k = (k * cos + pltpu.roll(k, half, 1) * sin) * scale
            scores = _mm_nt(q, k) * dmask_ref[h]
            o = _mm(scores, v) + _mm(q * qdec_ref[h], state)
            state_s[h] = cdec_ref[h] * state + _mm((k * kdec_ref[h]).T, v)
            mu = jnp.mean(o, axis=-1, keepdims=True)
            d = o - mu
            var = jnp.mean(d * d, axis=-1, keepdims=True)
            o_ref[sl, hs] = d * lax.rsqrt(var + 1e-5) * _silu(hx[:, 3 * RET_W + h * RET_D:3 * RET_W + (h + 1) * RET_D])

    hx = project(0)
    for n in range(ts // c):
        hx_next = project(n + 1) if n + 1 < ts // c else None
        chunk(n, hx)
        hx = hx_next


def _retention(x, w, tabs, ts=512):
    b, s, dm = x.shape
    cos, sin, dmask, qdec, kdec, cdec = tabs
    seq = lambda: pl.BlockSpec((None, ts, RET_W), lambda i, t: (i, t, 0))
    tab = lambda: pl.BlockSpec((RET_HEADS, RET_CHUNK, RET_D), lambda i, t: (0, 0, 0))
    rot = lambda: pl.BlockSpec((ts, RET_D), lambda i, t: (t, 0))
    return pl.pallas_call(
        _ret_kernel,
        out_shape=jax.ShapeDtypeStruct((b, s, RET_W), F32),
        grid=(b, s // ts),
        in_specs=[pl.BlockSpec((None, ts, dm), lambda i, t: (i, t, 0)), pl.BlockSpec(w.shape, lambda i, t: (0, 0)),
                  rot(), rot(), tab(), tab(), tab(), tab()],
        out_specs=seq(),
        scratch_shapes=[pltpu.VMEM((RET_HEADS, RET_D, RET_D), F32)],
        compiler_params=_params("parallel", "arbitrary"),
        name="retention",
    )(x, w, cos, sin, dmask, qdec, kdec, cdec)


def _ret_tables(s):
    inv = ROPE_BASE ** (-jnp.arange(0, RET_D, 2, dtype=F32) / RET_D)
    ang = jnp.arange(s, dtype=F32)[:, None] * inv[None, :]
    cos = jnp.concatenate([jnp.cos(ang), jnp.cos(ang)], axis=-1)
    sin = jnp.concatenate([-jnp.sin(ang), jnp.sin(ang)], axis=-1)
    c = RET_CHUNK
    log_g = jnp.log1p(-jnp.exp2(-5.0 - jnp.arange(RET_HEADS, dtype=F32)))
    pos = jnp.arange(c, dtype=F32)
    diff = pos[:, None] - pos[None, :]
    causal = diff >= 0
    dmask = jnp.where(causal[None], jnp.exp(jnp.where(causal, diff, 0.0)[None] * log_g[:, None, None]), 0.0)
    ones = jnp.ones((1, 1, RET_D), F32)
    kdec = jnp.exp((c - 1 - pos)[None, :, None] * log_g[:, None, None]) * ones
    qdec = jnp.exp((pos + 1.0)[None, :, None] * log_g[:, None, None]) * ones
    cdec = jnp.exp(c * log_g)[:, None, None] * jnp.ones((1, c, RET_D), F32)
    return cos, sin, dmask, qdec, kdec, cdec


def _gdn_kernel(qkv_ref, z_ref, sm_ref, smt_ref, cw_ref, alog_ref, dtb_ref, alog_l_ref, dtb_l_ref, nw_ref, o_ref,
                state_s, xs_s, h_qe, h_kd, h_mp, h_r, h_at, h_dec, c_qe, c_kd, c_mp, c_r, c_at, c_dec,
                lhs_s, ou_s, psi_s, dec_s, *, nt):
    ts = qkv_ref.shape[0]
    c, d, nh = GDN_CHUNK, GDN_D, GDN_HEADS
    hc = nh * c
    nct = ts // c
    g = pl.program_id(0)
    ri = lax.broadcasted_iota(jnp.int32, (hc, hc), 0)
    ci = lax.broadcasted_iota(jnp.int32, (hc, hc), 1)
    same = (ri // c) == (ci // c)
    incl = same & (ri >= ci)
    strict = same & (ri > ci)
    upper = same & (ri <= ci)
    neg_a = -jnp.exp(alog_ref[...])
    dtb = dtb_ref[...]
    neg_a_l = -jnp.exp(alog_l_ref[...])
    dtb_l = dtb_l_ref[...]
    nw = nw_ref[...]

    @pl.when(g == 0)
    def _():
        for ref in (state_s, h_qe, h_kd, h_mp, h_r, h_at, h_dec):
            ref[...] = jnp.zeros_like(ref)

    @pl.when(g % nt == 0)
    def _():
        xs_s[0:8, :] = jnp.zeros((8, xs_s.shape[1]), F32)

    xs_s[8:8 + ts, :] = qkv_ref[...]
    for dst, src in ((c_qe, h_qe), (c_kd, h_kd), (c_mp, h_mp), (c_r, h_r), (c_at, h_at), (c_dec, h_dec)):
        dst[...] = src[...]

    def conv_silu_stack(n, part):
        start = pl.multiple_of(n * c, c)
        cols = slice(part * nh * d, (part + 1) * nh * d)
        w = cw_ref[:, cols]
        ext = xs_s[pl.ds(start, c + 8), cols]
        y = ext[8:8 + c, :] * w[CONV_K - 1:CONV_K, :]
        for j in range(1, CONV_K):
            y = y + ext[8 - j:8 - j + c, :] * w[CONV_K - 1 - j:CONV_K - j, :]
        y = _silu(y)
        return [y[:, h * d:(h + 1) * d] for h in range(nh)]

    def l2n(xs, mult):
        out = [x * (lax.rsqrt(jnp.sum(x * x, axis=-1, keepdims=True) + EPS) * mult) for x in xs]
        return jnp.concatenate(out, axis=0)

    def prep(it, carry):
        chunks = [it * GDN_UNROLL + u for u in range(GDN_UNROLL)]
        prev = [dict(qe=c_qe[n], kd=c_kd[n], mp=c_mp[n], r=c_r[n], at=c_at[n], dec=c_dec[n]) for n in chunks]
        st = []
        for n in chunks:
            sl = pl.ds(pl.multiple_of(n * c, c), c)
            q = l2n(conv_silu_stack(n, 0), d ** -0.5)
            k = l2n(conv_silu_stack(n, 1), 1.0)
            v = jnp.concatenate(conv_silu_stack(n, 2), axis=0)
            sm = sm_ref[sl, :]
            beta_col = jnp.concatenate([_sigmoid(sm[:, h:h + 1]) for h in range(nh)], axis=0)
            g_col = jnp.concatenate(
                [neg_a[:, nh + h:nh + h + 1] * _softplus(sm[:, nh + h:nh + h + 1] + dtb[:, nh + h:nh + h + 1])
                 for h in range(nh)], axis=0)
            smt = smt_ref[n]
            g_row = neg_a_l * _softplus(smt[1:2, :] + dtb_l)
            g_row_b = jnp.broadcast_to(g_row, (hc, hc))
            gc_col = jnp.sum(jnp.where(incl, g_row_b, 0.0), axis=1, keepdims=True)
            gc_row = jnp.sum(jnp.where(upper, jnp.broadcast_to(g_col, (hc, hc)), 0.0), axis=0, keepdims=True)
            g_last = jnp.sum(jnp.where(same, g_row_b, 0.0), axis=1, keepdims=True)
            decay = jnp.where(incl, jnp.exp(jnp.where(incl, gc_col - gc_row, 0.0)), 0.0)
            egc = jnp.exp(gc_col)
            st.append(dict(q=q, k=k, v=v, beta=beta_col, gc_col=gc_col, g_last=g_last, decay=decay, egc=egc))
        for s_ in st:
            qk_kk = _mm_nt(jnp.concatenate([s_["q"], s_["k"]], axis=0), s_["k"])
            s_["attn"] = qk_kk[:hc] * s_["decay"]
            s_["mp"] = _bf(-jnp.where(strict, s_["beta"] * qk_kk[hc:] * s_["decay"], 0.0))
            s_["r"] = jnp.concatenate([s_["v"] * s_["beta"], s_["k"] * (s_["beta"] * s_["egc"])], axis=1)
        for p_ in prev:
            p_["r"] = p_["r"] + jnp.dot(p_["mp"], _bf(p_["r"]), preferred_element_type=F32)
        for _ in range(5):
            for p_ in prev:
                p_["mp"] = _bf(jnp.dot(p_["mp"], p_["mp"], preferred_element_type=F32))
            for p_ in prev:
                p_["r"] = p_["r"] + jnp.dot(p_["mp"], _bf(p_["r"]), preferred_element_type=F32)
        for n, p_ in zip(chunks, prev):
            rb = _bf(p_["r"])
            ar = jnp.dot(p_["at"], rb, preferred_element_type=F32)
            qt = p_["qe"] - ar[:, d:]
            for h in range(nh):
                rows = slice(h * c, (h + 1) * c)
                kr = jnp.dot(_bf(p_["kd"][rows].T), rb[rows], preferred_element_type=F32)
                lhs_s[h, n, 0:c, :] = _bf(qt[rows])
                lhs_s[h, n, c:c + d, :] = _bf(-kr[:, d:])
                ou_s[h, n] = ar[rows, :d]
                psi_s[h, n] = kr[:, :d]
                dec_s[h, n] = p_["dec"][h]
        for n, s_ in zip(chunks, st):
            h_qe[n] = s_["q"] * s_["egc"]
            h_kd[n] = s_["k"] * jnp.exp(s_["g_last"] - s_["gc_col"])
            h_mp[n] = s_["mp"]
            h_r[n] = s_["r"]
            h_at[n] = _bf(s_["attn"])
            for h in range(nh):
                h_dec[n, h] = jnp.broadcast_to(jnp.exp(s_["g_last"][h * c:h * c + 1, :]), (8, d))
        return carry

    lax.fori_loop(0, nct // GDN_UNROLL, prep, 0)

    @pl.when((g + nt - 1) % nt == 0)
    def _():
        state_s[...] = jnp.zeros_like(state_s)

    def scan(n, carry):
        sl = pl.ds(pl.multiple_of(n * c, c), c)
        for h in range(nh):
            hs = slice(h * d, (h + 1) * d)
            state = state_s[h]
            res = jnp.dot(lhs_s[h, n], _bf(state), preferred_element_type=F32)
            o = res[:c] + ou_s[h, n]
            state_s[h] = dec_s[h, n][0:1, :] * state + res[c:] + psi_s[h, n]
            o = o * lax.rsqrt(jnp.mean(o * o, axis=-1, keepdims=True) + EPS) * nw
            o_ref[sl, hs] = o * _silu(z_ref[sl, hs])
        return carry

    lax.fori_loop(0, nct, scan, 0, unroll=True)
    xs_s[0:8, :] = qkv_ref[ts - 8:ts, :]


def _gdn(qkv, z, sm, smt, conv_w, alog_row, dtb_row, alog_lanes, dtb_lanes, norm_w, s, ts=512):
    m, wq = qkv.shape
    nh, c, d = GDN_HEADS, GDN_CHUNK, GDN_D
    hc = nh * c
    nct = ts // c
    nt = s // ts
    ntile = m // ts
    assert nct % GDN_UNROLL == 0 and s % ts == 0
    front = lambda wd: pl.BlockSpec((ts, wd), lambda g: (jnp.minimum(g, ntile - 1), 0))
    back = lambda wd: pl.BlockSpec((ts, wd), lambda g: (jnp.maximum(g - 1, 0), 0))
    full = lambda a: pl.BlockSpec(a.shape, lambda g, n=a.ndim: (0,) * n)
    return pl.pallas_call(
        functools.partial(_gdn_kernel, nt=nt),
        out_shape=jax.ShapeDtypeStruct((m, GDN_W), F32),
        grid=(ntile + 1,),
        in_specs=[front(wq), back(GDN_W), front(LANES),
                  pl.BlockSpec((nct, 8, hc), lambda g: (jnp.minimum(g, ntile - 1), 0, 0)),
                  full(conv_w), full(alog_row), full(dtb_row), full(alog_lanes), full(dtb_lanes), full(norm_w)],
        out_specs=back(GDN_W),
        scratch_shapes=[pltpu.VMEM((nh, d, d), F32),
                        pltpu.VMEM((ts + 8, wq), F32),
                        pltpu.VMEM((nct, hc, d), F32),
                        pltpu.VMEM((nct, hc, d), F32),
                        pltpu.VMEM((nct, hc, hc), BF16),
                        pltpu.VMEM((nct, hc, 2 * d), F32),
                        pltpu.VMEM((nct, hc, hc), BF16),
                        pltpu.VMEM((nct, nh, 8, d), F32),
                        pltpu.VMEM((nct, hc, d), F32),
                        pltpu.VMEM((nct, hc, d), F32),
                        pltpu.VMEM((nct, hc, hc), BF16),
                        pltpu.VMEM((nct, hc, 2 * d), F32),
                        pltpu.VMEM((nct, hc, hc), BF16),
                        pltpu.VMEM((nct, nh, 8, d), F32),
                        pltpu.VMEM((nh, nct, c + d, d), BF16),
                        pltpu.VMEM((nh, nct, c, d), F32),
                        pltpu.VMEM((nh, nct, d, d), F32),
                        pltpu.VMEM((nh, nct, 8, d), F32)],
        compiler_params=_params("arbitrary"),
        name="gated_delta_net",
    )(qkv, z, sm, smt, conv_w, alog_row, dtb_row, alog_lanes, dtb_lanes, norm_w)


def _cmp_kernel(yk_ref, yv_ref, pek_ref, pev_ref, w1k_ref, w1v_ref, w2k_ref, w2vt_ref, kc_ref, vct_ref):
    nrow = yk_ref.shape[0]

    def hidden(y_ref, pe_ref, w1_ref):
        y = y_ref[...]
        first = _mm(y + pe_ref[0:1, :], w1_ref[0])
        second = _mm(y + pe_ref[1:2, :], w1_ref[1])
        pre = first + pltpu.roll(second, nrow - 1, 0)
        return jax.nn.gelu(pre)

    hk = hidden(yk_ref, pek_ref, w1k_ref)
    kc_ref[...] = jnp.concatenate(
        [_mm(hk[:, g * CMP_HIDDEN:(g + 1) * CMP_HIDDEN], w2k_ref[...]) for g in range(NSA_GROUPS)], axis=1)
    hv = hidden(yv_ref, pev_ref, w1v_ref)
    vct_ref[...] = jnp.concatenate(
        [_mm_nt(w2vt_ref[...], hv[:, g * CMP_HIDDEN:(g + 1) * CMP_HIDDEN]) for g in range(NSA_GROUPS)], axis=0)


def _compress(yk, yv, pek, pev, w1k, w1v, w2k, w2vt):
    b, nrow, width = yk.shape
    hid = NSA_GROUPS * CMP_HIDDEN
    full = lambda shape: pl.BlockSpec(shape, lambda i, n=len(shape): (0,) * n)
    return pl.pallas_call(
        _cmp_kernel,
        out_shape=[jax.ShapeDtypeStruct((b, nrow, NSA_KW), F32), jax.ShapeDtypeStruct((b, NSA_KW, nrow), F32)],
        grid=(b,),
        in_specs=[pl.BlockSpec((None, nrow, width), lambda i: (i, 0, 0)),
                  pl.BlockSpec((None, nrow, width), lambda i: (i, 0, 0)),
                  full((2, width)), full((2, width)), full((2, width, hid)), full((2, width, hid)),
                  full((CMP_HIDDEN, NSA_DK)), full((NSA_DK, CMP_HIDDEN))],
        out_specs=[pl.BlockSpec((None, nrow, NSA_KW), lambda i: (i, 0, 0)),
                   pl.BlockSpec((None, NSA_KW, nrow), lambda i: (i, 0, 0))],
        compiler_params=_params("parallel"),
        name="nsa_compress",
    )(yk, yv, pek, pev, w1k, w1v, w2k, w2vt)


LOG2E = 1.4426950408889634


def _exp2_cols(s):
    return jnp.exp2(s - jnp.max(s, axis=0, keepdims=True))


def _with_ones(vt):
    return jnp.concatenate([vt, jnp.ones((16, vt.shape[1]), BF16)], axis=0)


def _nsa_kernel(xt_ref, z_ref, kc_ref, vct_ref, ks_ref, vst_ref, kw_ref, vwt_ref, ovl_ref, o_ref, sel_s,
                s_even, s_odd, p_even, p_odd, acc_s):
    tq = z_ref.shape[0]
    s_len = ks_ref.shape[0]
    ncmp = kc_ref.shape[0]
    nslc = s_len // SLC_BLOCK
    qt = pl.program_id(1)
    t0 = qt * tq
    scale = NSA_DK ** -0.5 * LOG2E
    ng, hp, nh = NSA_GROUPS, NSA_HPG, NSA_HEADS
    cols = nh * tq
    t_row = t0 + lax.broadcasted_iota(jnp.int32, (1, tq), 1)
    sub = lax.broadcasted_iota(jnp.int32, (LANES, tq), 0)
    low = sub < NSA_DK
    gates = _sigmoid(xt_ref[NSA_QW + 2 * NSA_KW:NSA_QW + 2 * NSA_KW + NSA_GATE_ROWS, :])
    tile = lambda a, n: jnp.concatenate([a] * n, axis=1)
    zeros_half = jnp.zeros((NSA_DK, tq), F32)
    dk = NSA_DK

    def pv_groups(prep, vt, p):
        return jnp.concatenate(
            [jnp.dot(prep(vt[g * dk:(g + 1) * dk]), p[:, g * hp * tq:(g + 1) * hp * tq], preferred_element_type=F32)
             for g in range(ng)], axis=1)

    qs = []
    for head in range(nh):
        qh = xt_ref[head * NSA_DK:(head + 1) * NSA_DK, :] * scale
        qs.append(jnp.concatenate([qh, zeros_half] if head // hp == 0 else [zeros_half, qh], axis=0))
    q8 = _bf(jnp.concatenate(qs, axis=1))

    nwin = WIN + tq
    w0 = pl.multiple_of(jnp.maximum(t0 - WIN, 0), LANES)
    kpos = w0 + lax.broadcasted_iota(jnp.int32, (nwin, 1), 0)
    dq = t_row - kpos
    wmask8 = tile(dq.astype(jnp.uint32) < WIN, nh)
    s_w = jnp.dot(_bf(kw_ref[pl.ds(w0, nwin), :]), q8, preferred_element_type=F32)
    e_w = _exp2_cols(jnp.where(wmask8, s_w, NEG))
    o_win = pv_groups(_with_ones, _bf(vwt_ref[:, pl.ds(w0, nwin)]), _bf(e_w))
    o_win = o_win[:dk] * (1.0 / o_win[dk:dk + 1])

    c_end =lax.broadcasted_iota(jnp.int32, (ncmp, 1), 0) * CMP_STRIDE + (CMP_LEN - 1)
    cmask8 = tile(c_end <= t_row, nh)
    s_c = jnp.dot(_bf(kc_ref[...]), q8, preferred_element_type=F32)
    e_c = _exp2_cols(jnp.where(cmask8, s_c, NEG))
    p_c = jnp.where(cmask8, e_c * (1.0 / jnp.sum(e_c, axis=0, keepdims=True)), 0.0)
    o_cmp = pv_groups(lambda v: v, _bf(vct_ref[...]), _bf(p_c))

    ovl = ovl_ref[...]
    jb = lax.broadcasted_iota(jnp.int32, (nslc, tq), 0)
    cur = t_row // SLC_BLOCK
    forced = (jb == 0) | (jb == cur) | (jb == cur - 1)
    for g in range(ng):
        p_sum = p_c[:, g * hp * tq:(g * hp + 1) * tq]
        for hh in range(1, hp):
            p_sum = p_sum + p_c[:, (g * hp + hh) * tq:(g * hp + hh + 1) * tq]
        imp = sum(jnp.dot(ovl, part, preferred_element_type=F32) for part in _split3(p_sum))
        rank = jnp.where(jb <= cur, jnp.where(forced, BIG, imp), -BIG)
        cnt = jnp.zeros((nslc, tq), F32)
        for i in range(nslc):
            ri = rank[i:i + 1, :]
            tie = jnp.where(jb > i, 1.0, 0.0)
            cnt = cnt + jnp.where(ri > rank, 1.0, jnp.where(ri == rank, tie, 0.0))
        sel = jnp.where(cnt < float(min(N_SEL, nslc)), 0.0, NEG)
        for kt in range(s_len // LANES):
            sel_s[g, kt, 0:2, :] = sel[2 * kt:2 * kt + 2, :]

    def qk(kt):
        return jnp.dot(_bf(ks_ref[pl.ds(pl.multiple_of(kt * LANES, LANES), LANES), :]), q8,
                       preferred_element_type=F32)

    def pv(kt, p):
        return pv_groups(_with_ones, _bf(vst_ref[:, pl.ds(pl.multiple_of(kt * LANES, LANES), LANES)]), p)

    nkt = s_len // LANES

    def one_tile(kt, m_run, s_rd, s_wr, p_rd, p_wr):
        s_wr[...] = qk(jnp.minimum(kt + 1, nkt - 1))
        pv_prev = pv(jnp.maximum(kt - 1, 0), p_rd[...])
        causal = (kt * LANES + sub) <= t_row
        biases = []
        for g in range(ng):
            pair = sel_s[g, kt, 0:2, :]
            biases.append(tile(jnp.where(causal, jnp.where(low, pair[0:1, :], pair[1:2, :]), NEG), hp))
        s_s = s_rd[...] + jnp.concatenate(biases, axis=1)
        m_new = jnp.maximum(m_run, jnp.max(s_s, axis=0, keepdims=True))
        alpha = jnp.exp2(m_run - m_new)
        p = jnp.exp2(s_s - m_new)
        p_wr[...] = _bf(p)
        acc_s[...] = alpha * (acc_s[...] + pv_prev)
        return m_new

    def slc_body(j, m_run):
        m_run = one_tile(2 * j, m_run, s_even, s_odd, p_odd, p_even)
        return one_tile(2 * j + 1, m_run, s_odd, s_even, p_even, p_odd)

    s_even[...] = qk(0)
    p_odd[...] = jnp.zeros((LANES, cols), BF16)
    acc_s[...] = jnp.zeros(acc_s.shape, F32)
    npair = ((qt + 1) * tq // LANES + 1) // 2
    lax.fori_loop(0, npair, slc_body, jnp.full((1, cols), NEG, F32))
    o_slc = acc_s[...] + pv(2 * npair - 1, p_odd[...])
    o_slc = o_slc[:dk] * (1.0 / o_slc[dk:dk + 1])

    def head_out(head):
        cs = slice(head * tq, (head + 1) * tq)
        row = 3 * head
        return (gates[row:row + 1, :] * o_cmp[:, cs] + gates[row + 1:row + 2, :] * o_slc[:, cs]
                + gates[row + 2:row + 3, :] * o_win[:, cs])

    for hh in range(hp):
        both = jnp.concatenate([head_out(hh), head_out(hp + hh)], axis=0)
        cs = slice(hh * LANES, (hh + 1) * LANES)
        o_ref[:, cs] = both.T * _silu(z_ref[:, cs])


def _nsa_attention(xt, z, kcb, vcbt, ks, kw, ovl, b, s):
    tq = Q_TILE
    ncmp = kcb.shape[1]
    nqt = s // tq
    nslc = s // SLC_BLOCK
    vrow = NSA_QW // NSA_KW
    return pl.pallas_call(
        _nsa_kernel,
        out_shape=jax.ShapeDtypeStruct((b, s, NSA_QW), F32),
        grid=(b, nqt),
        in_specs=[pl.BlockSpec((NSA_T_ROWS, tq), lambda i, t: (0, i * nqt + t)),
                  pl.BlockSpec((None, tq, NSA_QW), lambda i, t: (i, t, 0)),
                  pl.BlockSpec((None, ncmp, NSA_KW), lambda i, t: (i, 0, 0)),
                  pl.BlockSpec((None, NSA_KW, ncmp), lambda i, t: (i, 0, 0)),
                  pl.BlockSpec((None, s, NSA_KW), lambda i, t: (i, 0, 0)),
                  pl.BlockSpec((NSA_KW, s), lambda i, t: (vrow, i)),
                  pl.BlockSpec((None, s, NSA_KW), lambda i, t: (i, 0, 0)),
                  pl.BlockSpec((NSA_KW, s), lambda i, t: (vrow + 1, i)),
                  pl.BlockSpec(ovl.shape, lambda i, t: (0, 0))],
        out_specs=pl.BlockSpec((None, tq, NSA_QW), lambda i, t: (i, t, 0)),
        scratch_shapes=[pltpu.VMEM((NSA_GROUPS, s // LANES, 8, tq), F32),
                        pltpu.VMEM((LANES, NSA_HEADS * tq), F32),
                        pltpu.VMEM((LANES, NSA_HEADS * tq), F32),
                        pltpu.VMEM((LANES, NSA_HEADS * tq), BF16),
                        pltpu.VMEM((LANES, NSA_HEADS * tq), BF16),
                        pltpu.VMEM((NSA_DK + 16, NSA_HEADS * tq), F32)],
        compiler_params=_params("parallel", "arbitrary"),
        name="nsa_attention",
    )(xt, z, kcb, vcbt, ks, xt, kw, xt, ovl)


def _nsa_tables(s):
    ncmp_rows = s // CMP_STRIDE
    nslc = s // SLC_BLOCK
    ci = np.arange(ncmp_rows)[None, :] * CMP_STRIDE
    sj = np.arange(nslc)[:, None] * SLC_BLOCK
    ovl = ((ci < sj + SLC_BLOCK) & (ci + CMP_LEN > sj)).astype(np.float32)
    return jnp.asarray(ovl, BF16)


def _merge_kernel(x_ref, p_ref, oa_ref, ob_ref, oc_ref, wm_ref, wa_ref, wb_ref, wc_ref, wo_ref, wpg_ref, wp_ref,
                  lg_ref, lb_ref, o_ref, *, alpha):
    d = x_ref.shape[1]
    x = x_ref[...]
    xb = _bf(x)
    mixed = None
    for idx, (br_ref, w_ref) in enumerate(((oa_ref, wa_ref), (ob_ref, wb_ref), (oc_ref, wc_ref))):
        gate = _sigmoid(jnp.dot(xb, wm_ref[:, idx * d:(idx + 1) * d], preferred_element_type=F32))
        term = gate * _mm(br_ref[...], w_ref[...])
        mixed = term if mixed is None else mixed + term
    sub = _mm(mixed, wo_ref[...])
    ple = _sigmoid(jnp.dot(xb, wpg_ref[...], preferred_element_type=F32)) * _mm(p_ref[...], wp_ref[...])
    y = alpha * x + sub + ple
    mu = jnp.mean(y, axis=-1, keepdims=True)
    yc = y - mu
    var = jnp.mean(yc * yc, axis=-1, keepdims=True)
    o_ref[...] = yc * lax.rsqrt(var + 1e-5) * lg_ref[...] + lb_ref[...]


def _merge(x2d, p2d, oa, ob, oc, wm, wa, wb, wc, wo, wpg, wp, lg, lb, alpha, tm=256):
    m, d = x2d.shape
    tile = lambda wd: pl.BlockSpec((tm, wd), lambda i: (i, 0))
    full = lambda a: pl.BlockSpec(a.shape, lambda i: (0, 0))
    weights = (wm, wa, wb, wc, wo, wpg, wp, lg, lb)
    return pl.pallas_call(
        functools.partial(_merge_kernel, alpha=alpha),
        out_shape=jax.ShapeDtypeStruct((m, d), F32),
        grid=(m // tm,),
        in_specs=[tile(d), tile(p2d.shape[1]), tile(oa.shape[1]), tile(ob.shape[1]), tile(oc.shape[1])]
        + [full(a) for a in weights],
        out_specs=tile(d),
        compiler_params=_params("parallel"),
        name="merge_out_norm",
    )(x2d, p2d, oa, ob, oc, *weights)


def _nsa_out_perm():
    idx = np.arange(NSA_QW).reshape(NSA_GROUPS, NSA_HPG, NSA_DK)
    return idx.transpose(1, 0, 2).reshape(-1)


def _layer_weights(w_in, conv_w, a_log, dt_bias, norm_w, pe_k, pe_v, w1k, w2k, w1v, w2v, w_br_c, d_model):
    sizes = (RET_W, RET_W, RET_W, RET_W, 3 * GDN_W, GDN_HEADS, GDN_HEADS, GDN_W,
             NSA_QW, NSA_KW, NSA_KW, NSA_KW, NSA_KW, NSA_KW, NSA_KW, 3 * NSA_HEADS, NSA_QW, 3 * d_model)
    offs = np.concatenate([[0], np.cumsum(sizes)])
    col = lambda i: w_in[:, offs[i]:offs[i + 1]]
    perm = np.concatenate([hd * RET_D + np.concatenate([np.arange(0, RET_D, 2), np.arange(1, RET_D, 2)])
                           for hd in range(RET_HEADS)])
    w_ret = _bf(jnp.concatenate([col(0)[:, perm], col(1)[:, perm], col(2), col(3)], axis=1))
    small = jnp.concatenate([col(5), col(6)], axis=1)
    w_gdn = _bf(jnp.concatenate([col(4), col(7), jnp.pad(small, ((0, 0), (0, LANES - small.shape[1])))], axis=1))
    w_gdn_t = _bf(small.T)
    operm = _nsa_out_perm()
    w_nsa = _bf(jnp.concatenate([col(16)[:, operm], col(9), col(10), col(11), col(13)], axis=1))
    gates_t = jnp.pad(col(15).T, ((0, NSA_GATE_ROWS - 3 * NSA_HEADS), (0, 0)))
    w_nsa_t = _bf(jnp.concatenate([col(8).T, col(12).T, col(14).T, gates_t], axis=0))
    w_merge = _bf(col(17))
    lane_ids = jnp.arange(LANES)
    pick = lambda vec: jnp.where((lane_ids >= GDN_HEADS) & (lane_ids < 2 * GDN_HEADS),
                                 jnp.pad(vec, (GDN_HEADS, LANES - 2 * GDN_HEADS)), 0.0)[None, :]
    alog_row = pick(a_log.astype(F32))
    dtb_row = pick(dt_bias.astype(F32))
    alog_lanes = jnp.repeat(a_log.astype(F32), GDN_CHUNK)[None, :]
    dtb_lanes = jnp.repeat(dt_bias.astype(F32), GDN_CHUNK)[None, :]
    half = CMP_LEN // 2

    def big_w1(w1):
        w = w1.reshape(2, half, 1, NSA_DK, 1, CMP_HIDDEN)
        eye = jnp.eye(NSA_GROUPS, dtype=F32).reshape(1, 1, NSA_GROUPS, 1, NSA_GROUPS, 1)
        return _bf((w * eye).reshape(2, half * NSA_GROUPS * NSA_DK, NSA_GROUPS * CMP_HIDDEN))

    def pe_rows(pe):
        t = jnp.broadcast_to(pe.reshape(2, half, 1, NSA_DK), (2, half, NSA_GROUPS, NSA_DK))
        return t.reshape(2, half * NSA_GROUPS * NSA_DK).astype(F32)

    return dict(w_ret=w_ret, w_gdn=w_gdn, w_gdn_t=w_gdn_t, w_nsa=w_nsa, w_nsa_t=w_nsa_t, w_merge=w_merge,
                alog_row=alog_row, dtb_row=dtb_row, alog_lanes=alog_lanes, dtb_lanes=dtb_lanes,
                norm_w=norm_w.astype(F32)[None, :], conv_w=conv_w.astype(F32),
                pek=pe_rows(pe_k), pev=pe_rows(pe_v), w1k=big_w1(w1k), w1v=big_w1(w1v), w2k=_bf(w2k),
                w2vt=_bf(w2v.T), w_br_c=_bf(w_br_c[operm, :]))


def _layer(x, p_i, lw, w_br_a, w_br_b, w_out, ln_g, ln_b, w_ple, w_ple_gate, ret_tabs, nsa_ovl, alpha):
    b, s, d = x.shape
    m = b * s
    x2d = x.reshape(m, d)
    r3 = lambda a: a.reshape(b, s, a.shape[-1])

    o_a = _retention(x, lw["w_ret"], ret_tabs)

    qkv, gz, sm, smt = _proj_t(x2d, lw["w_gdn"], lw["w_gdn_t"], (3 * GDN_W, GDN_W, LANES), name="proj_gdn")
    nchunk = s // GDN_CHUNK
    smt = smt.reshape(2, GDN_HEADS, b, nchunk, GDN_CHUNK).transpose(2, 3, 0, 1, 4)
    smt = jnp.pad(smt.reshape(b * nchunk, 2, GDN_HEADS * GDN_CHUNK), ((0, 0), (0, 6), (0, 0)))
    o_b = _gdn(qkv, gz, sm, smt, lw["conv_w"], lw["alog_row"], lw["dtb_row"], lw["alog_lanes"],
               lw["dtb_lanes"], lw["norm_w"], s)

    nz, kc, vc, ks, kw, xt = _proj_t(x2d, lw["w_nsa"], lw["w_nsa_t"], (NSA_QW,) + (NSA_KW,) * 4, name="proj_nsa")
    slab = lambda a: a.reshape(b, s // CMP_STRIDE, CMP_STRIDE * NSA_KW)
    kcb, vcbt = _compress(slab(kc), slab(vc), lw["pek"], lw["pev"], lw["w1k"], lw["w1v"], lw["w2k"], lw["w2vt"])
    o_c = _nsa_attention(xt, r3(nz), kcb, vcbt, r3(ks), r3(kw), nsa_ovl, b, s)

    out = _merge(x2d, p_i.reshape(m, -1), o_a.reshape(m, -1), o_b.reshape(m, -1), o_c.reshape(m, -1),
                 lw["w_merge"], _bf(w_br_a), _bf(w_br_b), lw["w_br_c"], _bf(w_out), _bf(w_ple_gate), _bf(w_ple),
                 ln_g.astype(F32)[None, :], ln_b.astype(F32)[None, :], alpha)
    return out.reshape(b, s, d)


def kernel(x, p, w_in, gdn_conv_w, gdn_a_log, gdn_dt_bias, gdn_norm_w, nsa_pe_k, nsa_pe_v, nsa_cmp_w1k, nsa_cmp_w2k, nsa_cmp_w1v, nsa_cmp_w2v, w_branch_a, w_branch_b, w_branch_c, w_out, ln_g, ln_b, w_ple, w_ple_gate):
    depth = w_in.shape[0]
    s, d = x.shape[1], x.shape[2]
    alpha = (2 * depth) ** 0.25
    ret_tabs = _ret_tables(s)
    nsa_ovl = _nsa_tables(s)
    for i in range(depth):
        lw = _layer_weights(w_in[i], gdn_conv_w[i], gdn_a_log[i], gdn_dt_bias[i], gdn_norm_w[i], nsa_pe_k[i],
                            nsa_pe_v[i], nsa_cmp_w1k[i], nsa_cmp_w2k[i], nsa_cmp_w1v[i], nsa_cmp_w2v[i],
                            w_branch_c[i], d)
        x = _layer(x, p[i], lw, w_branch_a[i], w_branch_b[i], w_out[i], ln_g[i], ln_b[i],
                   w_ple[i], w_ple_gate[i], ret_tabs, nsa_ovl, alpha)
    return x
```

```python
import functools

import numpy as np
import jax
import jax.numpy as jnp
from jax import lax
from jax.experimental import pallas as pl
from jax.experimental.pallas import tpu as pltpu

F32 = jnp.float32
BF16 = jnp.bfloat16

RET_HEADS, RET_D, RET_CHUNK = 4, 128, 128
ROPE_BASE = 10000.0
GDN_HEADS, GDN_D, GDN_CHUNK, CONV_K = 4, 128, 64, 4
NSA_HEADS, NSA_GROUPS, NSA_DK = 8, 2, 64
NSA_HPG = NSA_HEADS // NSA_GROUPS
CMP_LEN, CMP_STRIDE, CMP_HIDDEN = 32, 16, 128
SLC_BLOCK, N_SEL, WIN = 64, 8, 256
NEG, BIG, EPS = -1e30, 1e30, 1e-6
LANES = 128
Q_TILE = 256
GDN_UNROLL = 4
VMEM_LIMIT = 56 * 1024 * 1024

RET_W = RET_HEADS * RET_D
GDN_W = GDN_HEADS * GDN_D
NSA_QW = NSA_HEADS * NSA_DK
NSA_KW = NSA_GROUPS * NSA_DK
NSA_GATE_ROWS = 32
NSA_T_ROWS = NSA_QW + 2 * NSA_KW + NSA_GATE_ROWS


def _bf(x):
    return x.astype(BF16)


def _mm(a, b):
    return jnp.dot(_bf(a), _bf(b), preferred_element_type=F32)


def _mm_nt(a, b):
    return lax.dot_general(_bf(a), _bf(b), (((1,), (1,)), ((), ())), preferred_element_type=F32)


def _split3(x):
    hi = _bf(x)
    r1 = x - hi.astype(F32)
    mid = _bf(r1)
    lo = _bf(r1 - mid.astype(F32))
    return hi, mid, lo


def _sigmoid(x):
    return 0.5 * jnp.tanh(0.5 * x) + 0.5


def _silu(x):
    return x * _sigmoid(x)


def _softplus(x):
    return jnp.maximum(x, 0.0) + jnp.log(1.0 + jnp.exp(-jnp.abs(x)))


def _params(*sem):
    return pltpu.CompilerParams(dimension_semantics=sem, vmem_limit_bytes=VMEM_LIMIT)


def _proj_kernel(x_ref, w_ref, *o_refs, widths):
    x = _bf(x_ref[...])
    off = 0
    for o_ref, wd in zip(o_refs, widths):
        o_ref[...] = jnp.dot(x, w_ref[:, off:off + wd], preferred_element_type=F32).astype(o_ref.dtype)
        off += wd


def _proj(x2d, w, widths, tm=512, name="proj"):
    m, k = x2d.shape
    n = w.shape[1]
    assert sum(widths) == n and m % tm == 0
    return pl.pallas_call(
        functools.partial(_proj_kernel, widths=tuple(widths)),
        out_shape=[jax.ShapeDtypeStruct((m, wd), F32) for wd in widths],
        grid=(m // tm,),
        in_specs=[pl.BlockSpec((tm, k), lambda i: (i, 0)), pl.BlockSpec((k, n), lambda i: (0, 0))],
        out_specs=[pl.BlockSpec((tm, wd), lambda i: (i, 0)) for wd in widths],
        compiler_params=_params("parallel"),
        name=name,
    )(x2d, w)


def _proj_t_kernel(x_ref, w_ref, wt_ref, *o_refs, widths):
    x = _bf(x_ref[...])
    off = 0
    for o_ref, wd in zip(o_refs[:-1], widths):
        o_ref[...] = jnp.dot(x, w_ref[:, off:off + wd], preferred_element_type=F32)
        off += wd
    o_refs[-1][...] = lax.dot_general(wt_ref[...], x, (((1,), (1,)), ((), ())), preferred_element_type=F32)


def _proj_t(x2d, w, wt, widths, tm=512, name="proj_t"):
    m, k = x2d.shape
    n = w.shape[1]
    nt = wt.shape[0]
    assert sum(widths) == n and m % tm == 0
    return pl.pallas_call(
        functools.partial(_proj_t_kernel, widths=tuple(widths)),
        out_shape=[jax.ShapeDtypeStruct((m, wd), F32) for wd in widths] + [jax.ShapeDtypeStruct((nt, m), F32)],
        grid=(m // tm,),
        in_specs=[pl.BlockSpec((tm, k), lambda i: (i, 0)), pl.BlockSpec((k, n), lambda i: (0, 0)),
                  pl.BlockSpec((nt, k), lambda i: (0, 0))],
        out_specs=[pl.BlockSpec((tm, wd), lambda i: (i, 0)) for wd in widths]
        + [pl.BlockSpec((nt, tm), lambda i: (0, i))],
        compiler_params=_params("parallel"),
        name=name,
    )(x2d, w, wt)


def _ret_kernel(x_ref, w_ref, cos_ref, sin_ref, dmask_ref, qdec_ref, kdec_ref, cdec_ref, o_ref, state_s):
    ts = x_ref.shape[0]
    c = RET_CHUNK
    half = RET_D // 2
    scale = RET_D ** -0.5

    @pl.when(pl.program_id(1) == 0)
    def _():
        state_s[...] = jnp.zeros_like(state_s)

    def project(n):
        return jnp.dot(_bf(x_ref[n * c:(n + 1) * c, :]), w_ref[...], preferred_element_type=F32)

    def chunk(n, hx):
        sl = slice(n * c, (n + 1) * c)
        cos = cos_ref[sl, :]
        sin = sin_ref[sl, :]
        for h in range(RET_HEADS):
            hs = slice(h * RET_D, (h + 1) * RET_D)
            q = hx[:, h * RET_D:(h + 1) * RET_D]
            k = hx[:, RET_W + h * RET_D:RET_W + (h + 1) * RET_D]
            v = hx[:, 2 * RET_W + h * RET_D:2 * RET_W + (h + 1) * RET_D]
            state = state_s[h]
            q = q * cos + pltpu.roll(q, half, 1) * sin
            k = (k * cos + pltpu.roll(k, half, 1) * sin) * scale
            scores = _mm_nt(q, k) * dmask_ref[h]
            o = _mm(scores, v) + _mm(q * qdec_ref[h], state)
            state_s[h] = cdec_ref[h] * state + _mm((k * kdec_ref[h]).T, v)
            mu = jnp.mean(o, axis=-1, keepdims=True)
            d = o - mu
            var = jnp.mean(d * d, axis=-1, keepdims=True)
            o_ref[sl, hs] = d * lax.rsqrt(var + 1e-5) * _silu(hx[:, 3 * RET_W + h * RET_D:3 * RET_W + (h + 1) * RET_D])

    hx = project(0)
    for n in range(ts // c):
        hx_next = project(n + 1) if n + 1 < ts // c else None
        chunk(n, hx)
        hx = hx_next


def _retention(x, w, tabs, ts=512):
    b, s, dm = x.shape
    cos, sin, dmask, qdec, kdec, cdec = tabs
    seq = lambda: pl.BlockSpec((None, ts, RET_W), lambda i, t: (i, t, 0))
    tab = lambda: pl.BlockSpec((RET_HEADS, RET_CHUNK, RET_D), lambda i, t: (0, 0, 0))
    rot = lambda: pl.BlockSpec((ts, RET_D), lambda i, t: (t, 0))
    return pl.pallas_call(
        _ret_kernel,
        out_shape=jax.ShapeDtypeStruct((b, s, RET_W), F32),
        grid=(b, s // ts),
        in_specs=[pl.BlockSpec((None, ts, dm), lambda i, t: (i, t, 0)), pl.BlockSpec(w.shape, lambda i, t: (0, 0)),
                  rot(), rot(), tab(), tab(), tab(), tab()],
        out_specs=seq(),
        scratch_shapes=[pltpu.VMEM((RET_HEADS, RET_D, RET_D), F32)],
        compiler_params=_params("parallel", "arbitrary"),
        name="retention",
    )(x, w, cos, sin, dmask, qdec, kdec, cdec)


def _ret_tables(s):
    inv = ROPE_BASE ** (-jnp.arange(0, RET_D, 2, dtype=F32) / RET_D)
    ang = jnp.arange(s, dtype=F32)[:, None] * inv[None, :]
    cos = jnp.concatenate([jnp.cos(ang), jnp.cos(ang)], axis=-1)
    sin = jnp.concatenate([-jnp.sin(ang), jnp.sin(ang)], axis=-1)
    c = RET_CHUNK
    log_g = jnp.log1p(-jnp.exp2(-5.0 - jnp.arange(RET_HEADS, dtype=F32)))
    pos = jnp.arange(c, dtype=F32)
    diff = pos[:, None] - pos[None, :]
    causal = diff >= 0
    dmask = jnp.where(causal[None], jnp.exp(jnp.where(causal, diff, 0.0)[None] * log_g[:, None, None]), 0.0)
    ones = jnp.ones((1, 1, RET_D), F32)
    kdec = jnp.exp((c - 1 - pos)[None, :, None] * log_g[:, None, None]) * ones
    qdec = jnp.exp((pos + 1.0)[None, :, None] * log_g[:, None, None]) * ones
    cdec = jnp.exp(c * log_g)[:, None, None] * jnp.ones((1, c, RET_D), F32)
    return cos, sin, dmask, qdec, kdec, cdec


def _gdn_kernel(qkv_ref, z_ref, sm_ref, smt_ref, cw_ref, alog_ref, dtb_ref, alog_l_ref, dtb_l_ref, nw_ref, o_ref,
                state_s, xs_s, h_qe, h_kd, h_mp, h_r, h_at, h_dec, c_qe, c_kd, c_mp, c_r, c_at, c_dec,
                lhs_s, ou_s, psi_s, dec_s, *, nt):
    ts = qkv_ref.shape[0]
    c, d, nh = GDN_CHUNK, GDN_D, GDN_HEADS
    hc = nh * c
    nct = ts // c
    g = pl.program_id(0)
    ri = lax.broadcasted_iota(jnp.int32, (hc, hc), 0)
    ci = lax.broadcasted_iota(jnp.int32, (hc, hc), 1)
    same = (ri // c) == (ci // c)
    incl = same & (ri >= ci)
    strict = same & (ri > ci)
    upper = same & (ri <= ci)
    neg_a = -jnp.exp(alog_ref[...])
    dtb = dtb_ref[...]
    neg_a_l = -jnp.exp(alog_l_ref[...])
    dtb_l = dtb_l_ref[...]
    nw = nw_ref[...]

    @pl.when(g == 0)
    def _():
        for ref in (state_s, h_qe, h_kd, h_mp, h_r, h_at, h_dec):
            ref[...] = jnp.zeros_like(ref)

    @pl.when(g % nt == 0)
    def _():
        xs_s[0:8, :] = jnp.zeros((8, xs_s.shape[1]), F32)

    xs_s[8:8 + ts, :] = qkv_ref[...]
    for dst, src in ((c_qe, h_qe), (c_kd, h_kd), (c_mp, h_mp), (c_r, h_r), (c_at, h_at), (c_dec, h_dec)):
        dst[...] = src[...]

    def conv_silu_stack(n, part):
        start = pl.multiple_of(n * c, c)
        cols = slice(part * nh * d, (part + 1) * nh * d)
        w = cw_ref[:, cols]
        ext = xs_s[pl.ds(start, c + 8), cols]
        y = ext[8:8 + c, :] * w[CONV_K - 1:CONV_K, :]
        for j in range(1, CONV_K):
            y = y + ext[8 - j:8 - j + c, :] * w[CONV_K - 1 - j:CONV_K - j, :]
        y = _silu(y)
        return [y[:, h * d:(h + 1) * d] for h in range(nh)]

    def l2n(xs, mult):
        out = [x * (lax.rsqrt(jnp.sum(x * x, axis=-1, keepdims=True) + EPS) * mult) for x in xs]
        return jnp.concatenate(out, axis=0)

    def prep(it, carry):
        chunks = [it * GDN_UNROLL + u for u in range(GDN_UNROLL)]
        prev = [dict(qe=c_qe[n], kd=c_kd[n], mp=c_mp[n], r=c_r[n], at=c_at[n], dec=c_dec[n]) for n in chunks]
        st = []
        for n in chunks:
            sl = pl.ds(pl.multiple_of(n * c, c), c)
            q = l2n(conv_silu_stack(n, 0), d ** -0.5)
            k = l2n(conv_silu_stack(n, 1), 1.0)
            v = jnp.concatenate(conv_silu_stack(n, 2), axis=0)
            sm = sm_ref[sl, :]
            beta_col = jnp.concatenate([_sigmoid(sm[:, h:h + 1]) for h in range(nh)], axis=0)
            g_col = jnp.concatenate(
                [neg_a[:, nh + h:nh + h + 1] * _softplus(sm[:, nh + h:nh + h + 1] + dtb[:, nh + h:nh + h + 1])
                 for h in range(nh)], axis=0)
            smt = smt_ref[n]
            g_row = neg_a_l * _softplus(smt[1:2, :] + dtb_l)
            g_row_b = jnp.broadcast_to(g_row, (hc, hc))
            gc_col = jnp.sum(jnp.where(incl, g_row_b, 0.0), axis=1, keepdims=True)
            gc_row = jnp.sum(jnp.where(upper, jnp.broadcast_to(g_col, (hc, hc)), 0.0), axis=0, keepdims=True)
            g_last = jnp.sum(jnp.where(same, g_row_b, 0.0), axis=1, keepdims=True)
            decay = jnp.where(incl, jnp.exp(jnp.where(incl, gc_col - gc_row, 0.0)), 0.0)
            egc = jnp.exp(gc_col)
            st.append(dict(q=q, k=k, v=v, beta=beta_col, gc_col=gc_col, g_last=g_last, decay=decay, egc=egc))
        for s_ in st:
            qk_kk = _mm_nt(jnp.concatenate([s_["q"], s_["k"]], axis=0), s_["k"])
            s_["attn"] = qk_kk[:hc] * s_["decay"]
            s_["mp"] = _bf(-jnp.where(strict, s_["beta"] * qk_kk[hc:] * s_["decay"], 0.0))
            s_["r"] = jnp.concatenate([s_["v"] * s_["beta"], s_["k"] * (s_["beta"] * s_["egc"])], axis=1)
        for p_ in prev:
            p_["r"] = p_["r"] + jnp.dot(p_["mp"], _bf(p_["r"]), preferred_element_type=F32)
        for _ in range(5):
            for p_ in prev:
                p_["mp"] = _bf(jnp.dot(p_["mp"], p_["mp"], preferred_element_type=F32))
            for p_ in prev:
                p_["r"] = p_["r"] + jnp.dot(p_["mp"], _bf(p_["r"]), preferred_element_type=F32)
        for n, p_ in zip(chunks, prev):
            rb = _bf(p_["r"])
            ar = jnp.dot(p_["at"], rb, preferred_element_type=F32)
            qt = p_["qe"] - ar[:, d:]
            for h in range(nh):
                rows = slice(h * c, (h + 1) * c)
                kr = jnp.dot(_bf(p_["kd"][rows].T), rb[rows], preferred_element_type=F32)
                lhs_s[h, n, 0:c, :] = _bf(qt[rows])
                lhs_s[h, n, c:c + d, :] = _bf(-kr[:, d:])
                ou_s[h, n] = ar[rows, :d]
                psi_s[h, n] = kr[:, :d]
                dec_s[h, n] = p_["dec"][h]
        for n, s_ in zip(chunks, st):
            h_qe[n] = s_["q"] * s_["egc"]
            h_kd[n] = s_["k"] * jnp.exp(s_["g_last"] - s_["gc_col"])
            h_mp[n] = s_["mp"]
            h_r[n] = s_["r"]
            h_at[n] = _bf(s_["attn"])
            for h in range(nh):
                h_dec[n, h] = jnp.broadcast_to(jnp.exp(s_["g_last"][h * c:h * c + 1, :]), (8, d))
        return carry

    lax.fori_loop(0, nct // GDN_UNROLL, prep, 0)

    @pl.when((g + nt - 1) % nt == 0)
    def _():
        state_s[...] = jnp.zeros_like(state_s)

    def scan(n, carry):
        sl = pl.ds(pl.multiple_of(n * c, c), c)
        for h in range(nh):
            hs = slice(h * d, (h + 1) * d)
            state = state_s[h]
            res = jnp.dot(lhs_s[h, n], _bf(state), preferred_element_type=F32)
            o = res[:c] + ou_s[h, n]
            state_s[h] = dec_s[h, n][0:1, :] * state + res[c:] + psi_s[h, n]
            o = o * lax.rsqrt(jnp.mean(o * o, axis=-1, keepdims=True) + EPS) * nw
            o_ref[sl, hs] = o * _silu(z_ref[sl, hs])
        return carry

    lax.fori_loop(0, nct, scan, 0, unroll=True)
    xs_s[0:8, :] = qkv_ref[ts - 8:ts, :]


def _gdn(qkv, z, sm, smt, conv_w, alog_row, dtb_row, alog_lanes, dtb_lanes, norm_w, s, ts=512):
    m, wq = qkv.shape
    nh, c, d = GDN_HEADS, GDN_CHUNK, GDN_D
    hc = nh * c
    nct = ts // c
    nt = s // ts
    ntile = m // ts
    assert nct % GDN_UNROLL == 0 and s % ts == 0
    front = lambda wd: pl.BlockSpec((ts, wd), lambda g: (jnp.minimum(g, ntile - 1), 0))
    back = lambda wd: pl.BlockSpec((ts, wd), lambda g: (jnp.maximum(g - 1, 0), 0))
    full = lambda a: pl.BlockSpec(a.shape, lambda g, n=a.ndim: (0,) * n)
    return pl.pallas_call(
        functools.partial(_gdn_kernel, nt=nt),
        out_shape=jax.ShapeDtypeStruct((m, GDN_W), F32),
        grid=(ntile + 1,),
        in_specs=[front(wq), back(GDN_W), front(LANES),
                  pl.BlockSpec((nct, 8, hc), lambda g: (jnp.minimum(g, ntile - 1), 0, 0)),
                  full(conv_w), full(alog_row), full(dtb_row), full(alog_lanes), full(dtb_lanes), full(norm_w)],
        out_specs=back(GDN_W),
        scratch_shapes=[pltpu.VMEM((nh, d, d), F32),
                        pltpu.VMEM((ts + 8, wq), F32),
                        pltpu.VMEM((nct, hc, d), F32),
                        pltpu.VMEM((nct, hc, d), F32),
                        pltpu.VMEM((nct, hc, hc), BF16),
                        pltpu.VMEM((nct, hc, 2 * d), F32),
                        pltpu.VMEM((nct, hc, hc), BF16),
                        pltpu.VMEM((nct, nh, 8, d), F32),
                        pltpu.VMEM((nct, hc, d), F32),
                        pltpu.VMEM((nct, hc, d), F32),
                        pltpu.VMEM((nct, hc, hc), BF16),
                        pltpu.VMEM((nct, hc, 2 * d), F32),
                        pltpu.VMEM((nct, hc, hc), BF16),
                        pltpu.VMEM((nct, nh, 8, d), F32),
                        pltpu.VMEM((nh, nct, c + d, d), BF16),
                        pltpu.VMEM((nh, nct, c, d), F32),
                        pltpu.VMEM((nh, nct, d, d), F32),
                        pltpu.VMEM((nh, nct, 8, d), F32)],
        compiler_params=_params("arbitrary"),
        name="gated_delta_net",
    )(qkv, z, sm, smt, conv_w, alog_row, dtb_row, alog_lanes, dtb_lanes, norm_w)


def _cmp_kernel(yk_ref, yv_ref, pek_ref, pev_ref, w1k_ref, w1v_ref, w2k_ref, w2vt_ref, kc_ref, vct_ref):
    nrow = yk_ref.shape[0]

    def hidden(y_ref, pe_ref, w1_ref):
        y = y_ref[...]
        first = _mm(y + pe_ref[0:1, :], w1_ref[0])
        second = _mm(y + pe_ref[1:2, :], w1_ref[1])
        pre = first + pltpu.roll(second, nrow - 1, 0)
        return jax.nn.gelu(pre)

    hk = hidden(yk_ref, pek_ref, w1k_ref)
    kc_ref[...] = jnp.concatenate(
        [_mm(hk[:, g * CMP_HIDDEN:(g + 1) * CMP_HIDDEN], w2k_ref[...]) for g in range(NSA_GROUPS)], axis=1)
    hv = hidden(yv_ref, pev_ref, w1v_ref)
    vct_ref[...] = jnp.concatenate(
        [_mm_nt(w2vt_ref[...], hv[:, g * CMP_HIDDEN:(g + 1) * CMP_HIDDEN]) for g in range(NSA_GROUPS)], axis=0)


def _compress(yk, yv, pek, pev, w1k, w1v, w2k, w2vt):
    b, nrow, width = yk.shape
    hid = NSA_GROUPS * CMP_HIDDEN
    full = lambda shape: pl.BlockSpec(shape, lambda i, n=len(shape): (0,) * n)
    return pl.pallas_call(
        _cmp_kernel,
        out_shape=[jax.ShapeDtypeStruct((b, nrow, NSA_KW), F32), jax.ShapeDtypeStruct((b, NSA_KW, nrow), F32)],
        grid=(b,),
        in_specs=[pl.BlockSpec((None, nrow, width), lambda i: (i, 0, 0)),
                  pl.BlockSpec((None, nrow, width), lambda i: (i, 0, 0)),
                  full((2, width)), full((2, width)), full((2, width, hid)), full((2, width, hid)),
                  full((CMP_HIDDEN, NSA_DK)), full((NSA_DK, CMP_HIDDEN))],
        out_specs=[pl.BlockSpec((None, nrow, NSA_KW), lambda i: (i, 0, 0)),
                   pl.BlockSpec((None, NSA_KW, nrow), lambda i: (i, 0, 0))],
        compiler_params=_params("parallel"),
        name="nsa_compress",
    )(yk, yv, pek, pev, w1k, w1v, w2k, w2vt)


LOG2E = 1.4426950408889634


def _exp2_cols(s):
    return jnp.exp2(s - jnp.max(s, axis=0, keepdims=True))


def _with_ones(vt):
    return jnp.concatenate([vt, jnp.ones((16, vt.shape[1]), BF16)], axis=0)


def _nsa_kernel(xt_ref, z_ref, kc_ref, vct_ref, ks_ref, vst_ref, kw_ref, vwt_ref, ovl_ref, o_ref, sel_s,
                s_even, s_odd, p_even, p_odd, acc_s):
    tq = z_ref.shape[0]
    s_len = ks_ref.shape[0]
    ncmp = kc_ref.shape[0]
    nslc = s_len // SLC_BLOCK
    qt = pl.program_id(1)
    t0 = qt * tq
    scale = NSA_DK ** -0.5 * LOG2E
    ng, hp, nh = NSA_GROUPS, NSA_HPG, NSA_HEADS
    cols = nh * tq
    t_row = t0 + lax.broadcasted_iota(jnp.int32, (1, tq), 1)
    sub = lax.broadcasted_iota(jnp.int32, (LANES, tq), 0)
    low = sub < NSA_DK
    gates = _sigmoid(xt_ref[NSA_QW + 2 * NSA_KW:NSA_QW + 2 * NSA_KW + NSA_GATE_ROWS, :])
    tile = lambda a, n: jnp.concatenate([a] * n, axis=1)
    zeros_half = jnp.zeros((NSA_DK, tq), F32)
    dk = NSA_DK

    def pv_groups(prep, vt, p):
        return jnp.concatenate(
            [jnp.dot(prep(vt[g * dk:(g + 1) * dk]), p[:, g * hp * tq:(g + 1) * hp * tq], preferred_element_type=F32)
             for g in range(ng)], axis=1)

    qs = []
    for head in range(nh):
        qh = xt_ref[head * NSA_DK:(head + 1) * NSA_DK, :] * scale
        qs.append(jnp.concatenate([qh, zeros_half] if head // hp == 0 else [zeros_half, qh], axis=0))
    q8 = _bf(jnp.concatenate(qs, axis=1))

    nwin = WIN + tq
    w0 = pl.multiple_of(jnp.maximum(t0 - WIN, 0), LANES)
    kpos = w0 + lax.broadcasted_iota(jnp.int32, (nwin, 1), 0)
    dq = t_row - kpos
    wmask8 = tile(dq.astype(jnp.uint32) < WIN, nh)
    s_w = jnp.dot(_bf(kw_ref[pl.ds(w0, nwin), :]), q8, preferred_element_type=F32)
    e_w = _exp2_cols(jnp.where(wmask8, s_w, NEG))
    o_win = pv_groups(_with_ones, _bf(vwt_ref[:, pl.ds(w0, nwin)]), _bf(e_w))
    o_win = o_win[:dk] * (1.0 / o_win[dk:dk + 1])

    c_end =lax.broadcasted_iota(jnp.int32, (ncmp, 1), 0) * CMP_STRIDE + (CMP_LEN - 1)
    cmask8 = tile(c_end <= t_row, nh)
    s_c = jnp.dot(_bf(kc_ref[...]), q8, preferred_element_type=F32)
    e_c = _exp2_cols(jnp.where(cmask8, s_c, NEG))
    p_c = jnp.where(cmask8, e_c * (1.0 / jnp.sum(e_c, axis=0, keepdims=True)), 0.0)
    o_cmp = pv_groups(lambda v: v, _bf(vct_ref[...]), _bf(p_c))

    ovl = ovl_ref[...]
    jb = lax.broadcasted_iota(jnp.int32, (nslc, tq), 0)
    cur = t_row // SLC_BLOCK
    forced = (jb == 0) | (jb == cur) | (jb == cur - 1)
    for g in range(ng):
        p_sum = p_c[:, g * hp * tq:(g * hp + 1) * tq]
        for hh in range(1, hp):
            p_sum = p_sum + p_c[:, (g * hp + hh) * tq:(g * hp + hh + 1) * tq]
        imp = sum(jnp.dot(ovl, part, preferred_element_type=F32) for part in _split3(p_sum))
        rank = jnp.where(jb <= cur, jnp.where(forced, BIG, imp), -BIG)
        cnt = jnp.zeros((nslc, tq), F32)
        for i in range(nslc):
            ri = rank[i:i + 1, :]
            tie = jnp.where(jb > i, 1.0, 0.0)
            cnt = cnt + jnp.where(ri > rank, 1.0, jnp.where(ri == rank, tie, 0.0))
        sel = jnp.where(cnt < float(min(N_SEL, nslc)), 0.0, NEG)
        for kt in range(s_len // LANES):
            sel_s[g, kt, 0:2, :] = sel[2 * kt:2 * kt + 2, :]

    def qk(kt):
        return jnp.dot(_bf(ks_ref[pl.ds(pl.multiple_of(kt * LANES, LANES), LANES), :]), q8,
                       preferred_element_type=F32)

    def pv(kt, p):
        return pv_groups(_with_ones, _bf(vst_ref[:, pl.ds(pl.multiple_of(kt * LANES, LANES), LANES)]), p)

    nkt = s_len // LANES

    def one_tile(kt, carry, s_rd, s_wr, p_buf):
        m_run, a_prev = carry
        s_wr[...] = qk(jnp.minimum(kt + 1, nkt - 1))
        pv_old = pv(jnp.maximum(kt - 2, 0), p_buf[...])
        causal = (kt * LANES + sub) <= t_row
        biases = []
        for g in range(ng):
            pair = sel_s[g, kt, 0:2, :]
            biases.append(tile(jnp.where(causal, jnp.where(low, pair[0:1, :], pair[1:2, :]), NEG), hp))
        s_s = s_rd[...] + jnp.concatenate(biases, axis=1)
        m_new = jnp.maximum(m_run, jnp.max(s_s, axis=0, keepdims=True))
        alpha = jnp.exp2(m_run - m_new)
        p_buf[...] = _bf(jnp.exp2(s_s - m_new))
        acc_s[...] = alpha * (acc_s[...] + a_prev * pv_old)
        return m_new, alpha

    def slc_body(j, carry):
        carry = one_tile(2 * j, carry, s_even, s_odd, p_even)
        return one_tile(2 * j + 1, carry, s_odd, s_even, p_odd)

    s_even[...] = qk(0)
    p_even[...] = jnp.zeros((LANES, cols), BF16)
    p_odd[...] = jnp.zeros((LANES, cols), BF16)
    acc_s[...] = jnp.zeros(acc_s.shape, F32)
    npair = ((qt + 1) * tq // LANES + 1) // 2
    _, a_last = lax.fori_loop(0, npair, slc_body, (jnp.full((1, cols), NEG, F32), jnp.ones((1, cols), F32)))
    o_slc = acc_s[...] + a_last * pv(2 * npair - 2, p_even[...]) + pv(2 * npair - 1, p_odd[...])
    o_slc = o_slc[:dk] * (1.0 / o_slc[dk:dk + 1])

    def head_out(head):
        cs = slice(head * tq, (head + 1) * tq)
        row = 3 * head
        return (gates[row:row + 1, :] * o_cmp[:, cs] + gates[row + 1:row + 2, :] * o_slc[:, cs]
                + gates[row + 2:row + 3, :] * o_win[:, cs])

    for hh in range(hp):
        both = jnp.concatenate([head_out(hh), head_out(hp + hh)], axis=0)
        cs = slice(hh * LANES, (hh + 1) * LANES)
        o_ref[:, cs] = both.T * _silu(z_ref[:, cs])


def _nsa_attention(xt, z, kcb, vcbt, ks, kw, ovl, b, s):
    tq = Q_TILE
    ncmp = kcb.shape[1]
    nqt = s // tq
    nslc = s // SLC_BLOCK
    vrow = NSA_QW // NSA_KW
    return pl.pallas_call(
        _nsa_kernel,
        out_shape=jax.ShapeDtypeStruct((b, s, NSA_QW), F32),
        grid=(b, nqt),
        in_specs=[pl.BlockSpec((NSA_T_ROWS, tq), lambda i, t: (0, i * nqt + t)),
                  pl.BlockSpec((None, tq, NSA_QW), lambda i, t: (i, t, 0)),
                  pl.BlockSpec((None, ncmp, NSA_KW), lambda i, t: (i, 0, 0)),
                  pl.BlockSpec((None, NSA_KW, ncmp), lambda i, t: (i, 0, 0)),
                  pl.BlockSpec((None, s, NSA_KW), lambda i, t: (i, 0, 0)),
                  pl.BlockSpec((NSA_KW, s), lambda i, t: (vrow, i)),
                  pl.BlockSpec((None, s, NSA_KW), lambda i, t: (i, 0, 0)),
                  pl.BlockSpec((NSA_KW, s), lambda i, t: (vrow + 1, i)),
                  pl.BlockSpec(ovl.shape, lambda i, t: (0, 0))],
        out_specs=pl.BlockSpec((None, tq, NSA_QW), lambda i, t: (i, t, 0)),
        scratch_shapes=[pltpu.VMEM((NSA_GROUPS, s // LANES, 8, tq), F32),
                        pltpu.VMEM((LANES, NSA_HEADS * tq), F32),
                        pltpu.VMEM((LANES, NSA_HEADS * tq), F32),
                        pltpu.VMEM((LANES, NSA_HEADS * tq), BF16),
                        pltpu.VMEM((LANES, NSA_HEADS * tq), BF16),
                        pltpu.VMEM((NSA_DK + 16, NSA_HEADS * tq), F32)],
        compiler_params=_params("parallel", "arbitrary"),
        name="nsa_attention",
    )(xt, z, kcb, vcbt, ks, xt, kw, xt, ovl)


def _nsa_tables(s):
    ncmp_rows = s // CMP_STRIDE
    nslc = s // SLC_BLOCK
    ci = np.arange(ncmp_rows)[None, :] * CMP_STRIDE
    sj = np.arange(nslc)[:, None] * SLC_BLOCK
    ovl = ((ci < sj + SLC_BLOCK) & (ci + CMP_LEN > sj)).astype(np.float32)
    return jnp.asarray(ovl, BF16)


def _merge_kernel(x_ref, p_ref, oa_ref, ob_ref, oc_ref, wm_ref, wa_ref, wb_ref, wc_ref, wo_ref, wpg_ref, wp_ref,
                  lg_ref, lb_ref, o_ref, *, alpha):
    d = x_ref.shape[1]
    x = x_ref[...]
    xb = _bf(x)
    mixed = None
    for idx, (br_ref, w_ref) in enumerate(((oa_ref, wa_ref), (ob_ref, wb_ref), (oc_ref, wc_ref))):
        gate = _sigmoid(jnp.dot(xb, wm_ref[:, idx * d:(idx + 1) * d], preferred_element_type=F32))
        term = gate * _mm(br_ref[...], w_ref[...])
        mixed = term if mixed is None else mixed + term
    sub = _mm(mixed, wo_ref[...])
    ple = _sigmoid(jnp.dot(xb, wpg_ref[...], preferred_element_type=F32)) * _mm(p_ref[...], wp_ref[...])
    y = alpha * x + sub + ple
    mu = jnp.mean(y, axis=-1, keepdims=True)
    yc = y - mu
    var = jnp.mean(yc * yc, axis=-1, keepdims=True)
    o_ref[...] = yc * lax.rsqrt(var + 1e-5) * lg_ref[...] + lb_ref[...]


def _merge(x2d, p2d, oa, ob, oc, wm, wa, wb, wc, wo, wpg, wp, lg, lb, alpha, tm=256):
    m, d = x2d.shape
    tile = lambda wd: pl.BlockSpec((tm, wd), lambda i: (i, 0))
    full = lambda a: pl.BlockSpec(a.shape, lambda i: (0, 0))
    weights = (wm, wa, wb, wc, wo, wpg, wp, lg, lb)
    return pl.pallas_call(
        functools.partial(_merge_kernel, alpha=alpha),
        out_shape=jax.ShapeDtypeStruct((m, d), F32),
        grid=(m // tm,),
        in_specs=[tile(d), tile(p2d.shape[1]), tile(oa.shape[1]), tile(ob.shape[1]), tile(oc.shape[1])]
        + [full(a) for a in weights],
        out_specs=tile(d),
        compiler_params=_params("parallel"),
        name="merge_out_norm",
    )(x2d, p2d, oa, ob, oc, *weights)


def _nsa_out_perm():
    idx = np.arange(NSA_QW).reshape(NSA_GROUPS, NSA_HPG, NSA_DK)
    return idx.transpose(1, 0, 2).reshape(-1)


def _layer_weights(w_in, conv_w, a_log, dt_bias, norm_w, pe_k, pe_v, w1k, w2k, w1v, w2v, w_br_c, d_model):
    sizes = (RET_W, RET_W, RET_W, RET_W, 3 * GDN_W, GDN_HEADS, GDN_HEADS, GDN_W,
             NSA_QW, NSA_KW, NSA_KW, NSA_KW, NSA_KW, NSA_KW, NSA_KW, 3 * NSA_HEADS, NSA_QW, 3 * d_model)
    offs = np.concatenate([[0], np.cumsum(sizes)])
    col = lambda i: w_in[:, offs[i]:offs[i + 1]]
    perm = np.concatenate([hd * RET_D + np.concatenate([np.arange(0, RET_D, 2), np.arange(1, RET_D, 2)])
                           for hd in range(RET_HEADS)])
    w_ret = _bf(jnp.concatenate([col(0)[:, perm], col(1)[:, perm], col(2), col(3)], axis=1))
    small = jnp.concatenate([col(5), col(6)], axis=1)
    w_gdn = _bf(jnp.concatenate([col(4), col(7), jnp.pad(small, ((0, 0), (0, LANES - small.shape[1])))], axis=1))
    w_gdn_t = _bf(small.T)
    operm = _nsa_out_perm()
    w_nsa = _bf(jnp.concatenate([col(16)[:, operm], col(9), col(10), col(11), col(13)], axis=1))
    gates_t = jnp.pad(col(15).T, ((0, NSA_GATE_ROWS - 3 * NSA_HEADS), (0, 0)))
    w_nsa_t = _bf(jnp.concatenate([col(8).T, col(12).T, col(14).T, gates_t], axis=0))
    w_merge = _bf(col(17))
    lane_ids = jnp.arange(LANES)
    pick = lambda vec: jnp.where((lane_ids >= GDN_HEADS) & (lane_ids < 2 * GDN_HEADS),
                                 jnp.pad(vec, (GDN_HEADS, LANES - 2 * GDN_HEADS)), 0.0)[None, :]
    alog_row = pick(a_log.astype(F32))
    dtb_row = pick(dt_bias.astype(F32))
    alog_lanes = jnp.repeat(a_log.astype(F32), GDN_CHUNK)[None, :]
    dtb_lanes = jnp.repeat(dt_bias.astype(F32), GDN_CHUNK)[None, :]
    half = CMP_LEN // 2

    def big_w1(w1):
        w = w1.reshape(2, half, 1, NSA_DK, 1, CMP_HIDDEN)
        eye = jnp.eye(NSA_GROUPS, dtype=F32).reshape(1, 1, NSA_GROUPS, 1, NSA_GROUPS, 1)
        return _bf((w * eye).reshape(2, half * NSA_GROUPS * NSA_DK, NSA_GROUPS * CMP_HIDDEN))

    def pe_rows(pe):
        t = jnp.broadcast_to(pe.reshape(2, half, 1, NSA_DK), (2, half, NSA_GROUPS, NSA_DK))
        return t.reshape(2, half * NSA_GROUPS * NSA_DK).astype(F32)

    return dict(w_ret=w_ret, w_gdn=w_gdn, w_gdn_t=w_gdn_t, w_nsa=w_nsa, w_nsa_t=w_nsa_t, w_merge=w_merge,
                alog_row=alog_row, dtb_row=dtb_row, alog_lanes=alog_lanes, dtb_lanes=dtb_lanes,
                norm_w=norm_w.astype(F32)[None, :], conv_w=conv_w.astype(F32),
                pek=pe_rows(pe_k), pev=pe_rows(pe_v), w1k=big_w1(w1k), w1v=big_w1(w1v), w2k=_bf(w2k),
                w2vt=_bf(w2v.T), w_br_c=_bf(w_br_c[operm, :]))


def _layer(x, p_i, lw, w_br_a, w_br_b, w_out, ln_g, ln_b, w_ple, w_ple_gate, ret_tabs, nsa_ovl, alpha):
    b, s, d = x.shape
    m = b * s
    x2d = x.reshape(m, d)
    r3 = lambda a: a.reshape(b, s, a.shape[-1])

    o_a = _retention(x, lw["w_ret"], ret_tabs)

    qkv, gz, sm, smt = _proj_t(x2d, lw["w_gdn"], lw["w_gdn_t"], (3 * GDN_W, GDN_W, LANES), name="proj_gdn")
    nchunk = s // GDN_CHUNK
    smt = smt.reshape(2, GDN_HEADS, b, nchunk, GDN_CHUNK).transpose(2, 3, 0, 1, 4)
    smt = jnp.pad(smt.reshape(b * nchunk, 2, GDN_HEADS * GDN_CHUNK), ((0, 0), (0, 6), (0, 0)))
    o_b = _gdn(qkv, gz, sm, smt, lw["conv_w"], lw["alog_row"], lw["dtb_row"], lw["alog_lanes"],
               lw["dtb_lanes"], lw["norm_w"], s)

    nz, kc, vc, ks, kw, xt = _proj_t(x2d, lw["w_nsa"], lw["w_nsa_t"], (NSA_QW,) + (NSA_KW,) * 4, name="proj_nsa")
    slab = lambda a: a.reshape(b, s // CMP_STRIDE, CMP_STRIDE * NSA_KW)
    kcb, vcbt = _compress(slab(kc), slab(vc), lw["pek"], lw["pev"], lw["w1k"], lw["w1v"], lw["w2k"], lw["w2vt"])
    o_c = _nsa_attention(xt, r3(nz), kcb, vcbt, r3(ks), r3(kw), nsa_ovl, b, s)

    out = _merge(x2d, p_i.reshape(m, -1), o_a.reshape(m, -1), o_b.reshape(m, -1), o_c.reshape(m, -1),
                 lw["w_merge"], _bf(w_br_a), _bf(w_br_b), lw["w_br_c"], _bf(w_out), _bf(w_ple_gate), _bf(w_ple),
                 ln_g.astype(F32)[None, :], ln_b.astype(F32)[None, :], alpha)
    return out.reshape(b, s, d)


def kernel(x, p, w_in, gdn_conv_w, gdn_a_log, gdn_dt_bias, gdn_norm_w, nsa_pe_k, nsa_pe_v, nsa_cmp_w1k, nsa_cmp_w2k, nsa_cmp_w1v, nsa_cmp_w2v, w_branch_a, w_branch_b, w_branch_c, w_out, ln_g, ln_b, w_ple, w_ple_gate):
    depth = w_in.shape[0]
    s, d = x.shape[1], x.shape[2]
    alpha = (2 * depth) ** 0.25
    ret_tabs = _ret_tables(s)
    nsa_ovl = _nsa_tables(s)
    for i in range(depth):
        lw = _layer_weights(w_in[i], gdn_conv_w[i], gdn_a_log[i], gdn_dt_bias[i], gdn_norm_w[i], nsa_pe_k[i],
                            nsa_pe_v[i], nsa_cmp_w1k[i], nsa_cmp_w2k[i], nsa_cmp_w1v[i], nsa_cmp_w2v[i],
                            w_branch_c[i], d)
        x = _layer(x, p[i], lw, w_branch_a[i], w_branch_b[i], w_out[i], ln_g[i], ln_b[i],
                   w_ple[i], w_ple_gate[i], ret_tabs, nsa_ovl, alpha)
    return x
```

```python
import functools

import numpy as np
import jax
import jax.numpy as jnp
from jax import lax
from jax.experimental import pallas as pl
from jax.experimental.pallas import tpu as pltpu

F32 = jnp.float32
BF16 = jnp.bfloat16

RET_HEADS, RET_D, RET_CHUNK = 4, 128, 128
ROPE_BASE = 10000.0
GDN_HEADS, GDN_D, GDN_CHUNK, CONV_K = 4, 128, 64, 4
NSA_HEADS, NSA_GROUPS, NSA_DK = 8, 2, 64
NSA_HPG = NSA_HEADS // NSA_GROUPS
CMP_LEN, CMP_STRIDE, CMP_HIDDEN = 32, 16, 128
SLC_BLOCK, N_SEL, WIN = 64, 8, 256
NEG, BIG, EPS = -1e30, 1e30, 1e-6
LANES = 128
Q_TILE = 256
GDN_UNROLL = 4
VMEM_LIMIT = 56 * 1024 * 1024

RET_W = RET_HEADS * RET_D
GDN_W = GDN_HEADS * GDN_D
NSA_QW = NSA_HEADS * NSA_DK
NSA_KW = NSA_GROUPS * NSA_DK
NSA_GATE_ROWS = 32
NSA_T_ROWS = NSA_QW + 2 * NSA_KW + NSA_GATE_ROWS


def _bf(x):
    return x.astype(BF16)


def _mm(a, b):
    return jnp.dot(_bf(a), _bf(b), preferred_element_type=F32)


def _mm_nt(a, b):
    return lax.dot_general(_bf(a), _bf(b), (((1,), (1,)), ((), ())), preferred_element_type=F32)


def _split3(x):
    hi = _bf(x)
    r1 = x - hi.astype(F32)
    mid = _bf(r1)
    lo = _bf(r1 - mid.astype(F32))
    return hi, mid, lo


def _sigmoid(x):
    return 0.5 * jnp.tanh(0.5 * x) + 0.5


def _silu(x):
    return x * _sigmoid(x)


def _softplus(x):
    return jnp.maximum(x, 0.0) + jnp.log(1.0 + jnp.exp(-jnp.abs(x)))


def _params(*sem):
    return pltpu.CompilerParams(dimension_semantics=sem, vmem_limit_bytes=VMEM_LIMIT)


def _proj_t_kernel(x_ref, w_ref, wt_ref, *o_refs, widths):
    x = _bf(x_ref[...])
    off = 0
    for o_ref, wd in zip(o_refs[:-1], widths):
        o_ref[...] = jnp.dot(x, w_ref[:, off:off + wd], preferred_element_type=F32)
        off += wd
    o_refs[-1][...] = lax.dot_general(wt_ref[...], x, (((1,), (1,)), ((), ())), preferred_element_type=F32)


def _proj_t(x2d, w, wt, widths, tm=512, name="proj_t"):
    m, k = x2d.shape
    n = w.shape[1]
    nt = wt.shape[0]
    assert sum(widths) == n and m % tm == 0
    return pl.pallas_call(
        functools.partial(_proj_t_kernel, widths=tuple(widths)),
        out_shape=[jax.ShapeDtypeStruct((m, wd), F32) for wd in widths] + [jax.ShapeDtypeStruct((nt, m), F32)],
        grid=(m // tm,),
        in_specs=[pl.BlockSpec((tm, k), lambda i: (i, 0)), pl.BlockSpec((k, n), lambda i: (0, 0)),
                  pl.BlockSpec((nt, k), lambda i: (0, 0))],
        out_specs=[pl.BlockSpec((tm, wd), lambda i: (i, 0)) for wd in widths]
        + [pl.BlockSpec((nt, tm), lambda i: (0, i))],
        compiler_params=_params("parallel"),
        name=name,
    )(x2d, w, wt)


def _proj_gdn_kernel(x_ref, w_ref, qkv_ref, z_ref, sm_ref, smt_ref):
    x = _bf(x_ref[...])
    nq, nz = qkv_ref.shape[1], z_ref.shape[1]
    qkv_ref[...] = jnp.dot(x, w_ref[:, :nq], preferred_element_type=F32)
    z_ref[...] = jnp.dot(x, w_ref[:, nq:nq + nz], preferred_element_type=F32)
    sm = jnp.dot(x, w_ref[:, nq + nz:], preferred_element_type=F32)
    sm_ref[...] = sm
    smt_ref[...] = sm.T[0:8, :]


def _proj_gdn(x2d, w, tm=512):
    m, k = x2d.shape
    n = w.shape[1]
    widths = (3 * GDN_W, GDN_W, LANES)
    assert sum(widths) == n and m % tm == 0
    return pl.pallas_call(
        _proj_gdn_kernel,
        out_shape=[jax.ShapeDtypeStruct((m, wd), F32) for wd in widths] + [jax.ShapeDtypeStruct((8, m), F32)],
        grid=(m // tm,),
        in_specs=[pl.BlockSpec((tm, k), lambda i: (i, 0)), pl.BlockSpec((k, n), lambda i: (0, 0))],
        out_specs=[pl.BlockSpec((tm, wd), lambda i: (i, 0)) for wd in widths] + [pl.BlockSpec((8, tm), lambda i: (0, i))],
        compiler_params=_params("parallel"),
        name="proj_gdn",
    )(x2d, w)


def _ret_kernel(x_ref, w_ref, cos_ref, sin_ref, dmask_ref, qdec_ref, kdec_ref, cdec_ref, o_ref, state_s):
    ts = x_ref.shape[0]
    c = RET_CHUNK
    half = RET_D // 2
    scale = RET_D ** -0.5

    @pl.when(pl.program_id(1) == 0)
    def _():
        state_s[...] = jnp.zeros_like(state_s)

    def project(n):
        return jnp.dot(_bf(x_ref[n * c:(n + 1) * c, :]), w_ref[...], preferred_element_type=F32)

    def chunk(n, hx):
        sl = slice(n * c, (n + 1) * c)
        cos = cos_ref[sl, :]
        sin = sin_ref[sl, :]
        for h in range(RET_HEADS):
            hs = slice(h * RET_D, (h + 1) * RET_D)
            q = hx[:, h * RET_D:(h + 1) * RET_D]
            k = hx[:, RET_W + h * RET_D:RET_W + (h + 1) * RET_D]
            v = hx[:, 2 * RET_W + h * RET_D:2 * RET_W + (h + 1) * RET_D]
            state = state_s[h]
            q = q * cos + pltpu.roll(q, half, 1) * sin
            k = (k * cos + pltpu.roll(k, half, 1) * sin) * scale
            scores = _mm_nt(q, k) * dmask_ref[h]
            o = _mm(scores, v) + _mm(q * qdec_ref[h], state)
            state_s[h] = cdec_ref[h] * state + _mm((k * kdec_ref[h]).T, v)
            mu = jnp.mean(o, axis=-1, keepdims=True)
            d = o - mu
            var = jnp.mean(d * d, axis=-1, keepdims=True)
            o_ref[sl, hs] = d * lax.rsqrt(var + 1e-5) * _silu(hx[:, 3 * RET_W + h * RET_D:3 * RET_W + (h + 1) * RET_D])

    hx = project(0)
    for n in range(ts // c):
        hx_next = project(n + 1) if n + 1 < ts // c else None
        chunk(n, hx)
        hx = hx_next


def _retention(x, w, tabs, ts=512):
    b, s, dm = x.shape
    cos, sin, dmask, qdec, kdec, cdec = tabs
    seq = lambda: pl.BlockSpec((None, ts, RET_W), lambda i, t: (i, t, 0))
    tab = lambda: pl.BlockSpec((RET_HEADS, RET_CHUNK, RET_D), lambda i, t: (0, 0, 0))
    rot = lambda: pl.BlockSpec((ts, RET_D), lambda i, t: (t, 0))
    return pl.pallas_call(
        _ret_kernel,
        out_shape=jax.ShapeDtypeStruct((b, s, RET_W), F32),
        grid=(b, s // ts),
        in_specs=[pl.BlockSpec((None, ts, dm), lambda i, t: (i, t, 0)), pl.BlockSpec(w.shape, lambda i, t: (0, 0)),
                  rot(), rot(), tab(), tab(), tab(), tab()],
        out_specs=seq(),
        scratch_shapes=[pltpu.VMEM((RET_HEADS, RET_D, RET_D), F32)],
        compiler_params=_params("parallel", "arbitrary"),
        name="retention",
    )(x, w, cos, sin, dmask, qdec, kdec, cdec)


def _ret_tables(s):
    inv = ROPE_BASE ** (-jnp.arange(0, RET_D, 2, dtype=F32) / RET_D)
    ang = jnp.arange(s, dtype=F32)[:, None] * inv[None, :]
    cos = jnp.concatenate([jnp.cos(ang), jnp.cos(ang)], axis=-1)
    sin = jnp.concatenate([-jnp.sin(ang), jnp.sin(ang)], axis=-1)
    c = RET_CHUNK
    log_g = jnp.log1p(-jnp.exp2(-5.0 - jnp.arange(RET_HEADS, dtype=F32)))
    pos = jnp.arange(c, dtype=F32)
    diff = pos[:, None] - pos[None, :]
    causal = diff >= 0
    dmask = jnp.where(causal[None], jnp.exp(jnp.where(causal, diff, 0.0)[None] * log_g[:, None, None]), 0.0)
    ones = jnp.ones((1, 1, RET_D), F32)
    kdec = jnp.exp((c - 1 - pos)[None, :, None] * log_g[:, None, None]) * ones
    qdec = jnp.exp((pos + 1.0)[None, :, None] * log_g[:, None, None]) * ones
    cdec = jnp.exp(c * log_g)[:, None, None] * jnp.ones((1, c, RET_D), F32)
    return cos, sin, dmask, qdec, kdec, cdec


def _gdn_kernel(qkv_ref, z_ref, sm_ref, smt_ref, cw_ref, alog_ref, dtb_ref, alog_l_ref, dtb_l_ref, nw_ref, o_ref,
                state_s, xs_s, h_qe, h_kd, h_mp, h_r, h_at, h_dec,
                lhs_s, ou_s, psi_s, dec_s, *, nt):
    ts = qkv_ref.shape[0]
    c, d, nh = GDN_CHUNK, GDN_D, GDN_HEADS
    hc = nh * c
    nct = ts // c
    g = pl.program_id(0)
    ri = lax.broadcasted_iota(jnp.int32, (hc, hc), 0)
    ci = lax.broadcasted_iota(jnp.int32, (hc, hc), 1)
    same = (ri // c) == (ci // c)
    incl = same & (ri >= ci)
    strict = same & (ri > ci)
    upper = same & (ri <= ci)
    neg_a = -jnp.exp(alog_ref[...])
    dtb = dtb_ref[...]
    neg_a_l = -jnp.exp(alog_l_ref[...])
    dtb_l = dtb_l_ref[...]
    nw = nw_ref[...]

    @pl.when(g == 0)
    def _():
        for ref in (state_s, h_qe, h_kd, h_mp, h_r, h_at, h_dec):
            ref[...] = jnp.zeros_like(ref)

    @pl.when(g % nt == 0)
    def _():
        xs_s[0:8, :] = jnp.zeros((8, xs_s.shape[1]), F32)

    xs_s[8:8 + ts, :] = qkv_ref[...]
    wr = g % 2
    rd = 1 - wr

    def conv_silu_stack(n, part):
        start = pl.multiple_of(n * c, c)
        cols = slice(part * nh * d, (part + 1) * nh * d)
        w = cw_ref[:, cols]
        ext = xs_s[pl.ds(start, c + 8), cols]
        y = ext[8:8 + c, :] * w[CONV_K - 1:CONV_K, :]
        for j in range(1, CONV_K):
            y = y + ext[8 - j:8 - j + c, :] * w[CONV_K - 1 - j:CONV_K - j, :]
        y = _silu(y)
        return [y[:, h * d:(h + 1) * d] for h in range(nh)]

    def l2n(xs, mult):
        out = [x * (lax.rsqrt(jnp.sum(x * x, axis=-1, keepdims=True) + EPS) * mult) for x in xs]
        return jnp.concatenate(out, axis=0)

    def prep(it, carry):
        chunks = [it * GDN_UNROLL + u for u in range(GDN_UNROLL)]
        prev = [dict(qe=h_qe[rd, n], kd=h_kd[rd, n], mp=h_mp[rd, n], r=h_r[rd, n], at=h_at[rd, n], dec=h_dec[rd, n])
                for n in chunks]
        st = []
        for n in chunks:
            sl = pl.ds(pl.multiple_of(n * c, c), c)
            q = l2n(conv_silu_stack(n, 0), d ** -0.5)
            k = l2n(conv_silu_stack(n, 1), 1.0)
            v = jnp.concatenate(conv_silu_stack(n, 2), axis=0)
            sm = sm_ref[sl, :]
            beta_col = jnp.concatenate([_sigmoid(sm[:, h:h + 1]) for h in range(nh)], axis=0)
            g_col = jnp.concatenate(
                [neg_a[:, nh + h:nh + h + 1] * _softplus(sm[:, nh + h:nh + h + 1] + dtb[:, nh + h:nh + h + 1])
                 for h in range(nh)], axis=0)
            smt = smt_ref[n]
            g_row = neg_a_l * _softplus(smt[1:2, :] + dtb_l)
            g_row_b = jnp.broadcast_to(g_row, (hc, hc))
            gc_col = jnp.sum(jnp.where(incl, g_row_b, 0.0), axis=1, keepdims=True)
            gc_row = jnp.sum(jnp.where(upper, jnp.broadcast_to(g_col, (hc, hc)), 0.0), axis=0, keepdims=True)
            g_last = jnp.sum(jnp.where(same, g_row_b, 0.0), axis=1, keepdims=True)
            decay = jnp.where(incl, jnp.exp(jnp.where(incl, gc_col - gc_row, 0.0)), 0.0)
            egc = jnp.exp(gc_col)
            st.append(dict(q=q, k=k, v=v, beta=beta_col, gc_col=gc_col, g_last=g_last, decay=decay, egc=egc))
        for s_ in st:
            qk_kk = _mm_nt(jnp.concatenate([s_["q"], s_["k"]], axis=0), s_["k"])
            s_["attn"] = qk_kk[:hc] * s_["decay"]
            s_["mp"] = _bf(-jnp.where(strict, s_["beta"] * qk_kk[hc:] * s_["decay"], 0.0))
            s_["r"] = jnp.concatenate([s_["v"] * s_["beta"], s_["k"] * (s_["beta"] * s_["egc"])], axis=1)
        for p_ in prev:
            p_["r"] = p_["r"] + jnp.dot(p_["mp"], _bf(p_["r"]), preferred_element_type=F32)
        for _ in range(5):
            for p_ in prev:
                p_["mp"] = _bf(jnp.dot(p_["mp"], p_["mp"], preferred_element_type=F32))
            for p_ in prev:
                p_["r"] = p_["r"] + jnp.dot(p_["mp"], _bf(p_["r"]), preferred_element_type=F32)
        for n, p_ in zip(chunks, prev):
            rb = _bf(p_["r"])
            ar = jnp.dot(p_["at"], rb, preferred_element_type=F32)
            qt = p_["qe"] - ar[:, d:]
            for h in range(nh):
                rows = slice(h * c, (h + 1) * c)
                kr = jnp.dot(_bf(p_["kd"][rows].T), rb[rows], preferred_element_type=F32)
                lhs_s[h, n, 0:c, :] = _bf(qt[rows])
                lhs_s[h, n, c:c + d, :] = _bf(-kr[:, d:])
                ou_s[h, n] = ar[rows, :d]
                psi_s[h, n] = kr[:, :d]
                dec_s[h, n] = p_["dec"][h]
        for n, s_ in zip(chunks, st):
            h_qe[wr, n] = s_["q"] * s_["egc"]
            h_kd[wr, n] = s_["k"] * jnp.exp(s_["g_last"] - s_["gc_col"])
            h_mp[wr, n] = s_["mp"]
            h_r[wr, n] = s_["r"]
            h_at[wr, n] = _bf(s_["attn"])
            for h in range(nh):
                h_dec[wr, n, h] = jnp.broadcast_to(jnp.exp(s_["g_last"][h * c:h * c + 1, :]), (8, d))
        return carry

    lax.fori_loop(0, nct // GDN_UNROLL, prep, 0)

    @pl.when((g + nt - 1) % nt == 0)
    def _():
        state_s[...] = jnp.zeros_like(state_s)

    def scan(n, carry):
        sl = pl.ds(pl.multiple_of(n * c, c), c)
        for h in range(nh):
            hs = slice(h * d, (h + 1) * d)
            state = state_s[h]
            res = jnp.dot(lhs_s[h, n], _bf(state), preferred_element_type=F32)
            o = res[:c] + ou_s[h, n]
            state_s[h] = dec_s[h, n][0:1, :] * state + res[c:] + psi_s[h, n]
            o = o * lax.rsqrt(jnp.mean(o * o, axis=-1, keepdims=True) + EPS) * nw
            o_ref[sl, hs] = o * _silu(z_ref[sl, hs])
        return carry

    lax.fori_loop(0, nct, scan, 0, unroll=True)
    xs_s[0:8, :] = qkv_ref[ts - 8:ts, :]


def _gdn(qkv, z, sm, smt, conv_w, alog_row, dtb_row, alog_lanes, dtb_lanes, norm_w, s, ts=512):
    m, wq = qkv.shape
    nh, c, d = GDN_HEADS, GDN_CHUNK, GDN_D
    hc = nh * c
    nct = ts // c
    nt = s // ts
    ntile = m // ts
    assert nct % GDN_UNROLL == 0 and s % ts == 0
    front = lambda wd: pl.BlockSpec((ts, wd), lambda g: (jnp.minimum(g, ntile - 1), 0))
    back = lambda wd: pl.BlockSpec((ts, wd), lambda g: (jnp.maximum(g - 1, 0), 0))
    full = lambda a: pl.BlockSpec(a.shape, lambda g, n=a.ndim: (0,) * n)
    return pl.pallas_call(
        functools.partial(_gdn_kernel, nt=nt),
        out_shape=jax.ShapeDtypeStruct((m, GDN_W), F32),
        grid=(ntile + 1,),
        in_specs=[front(wq), back(GDN_W), front(LANES),
                  pl.BlockSpec((nct, 8, hc), lambda g: (jnp.minimum(g, ntile - 1), 0, 0)),
                  full(conv_w), full(alog_row), full(dtb_row), full(alog_lanes), full(dtb_lanes), full(norm_w)],
        out_specs=back(GDN_W),
        scratch_shapes=[pltpu.VMEM((nh, d, d), F32),
                        pltpu.VMEM((ts + 8, wq), F32),
                        pltpu.VMEM((2, nct, hc, d), F32),
                        pltpu.VMEM((2, nct, hc, d), F32),
                        pltpu.VMEM((2, nct, hc, hc), BF16),
                        pltpu.VMEM((2, nct, hc, 2 * d), F32),
                        pltpu.VMEM((2, nct, hc, hc), BF16),
                        pltpu.VMEM((2, nct, nh, 8, d), F32),
                        pltpu.VMEM((nh, nct, c + d, d), BF16),
                        pltpu.VMEM((nh, nct, c, d), F32),
                        pltpu.VMEM((nh, nct, d, d), F32),
                        pltpu.VMEM((nh, nct, 8, d), F32)],
        compiler_params=_params("arbitrary"),
        name="gated_delta_net",
    )(qkv, z, sm, smt, conv_w, alog_row, dtb_row, alog_lanes, dtb_lanes, norm_w)


def _cmp_kernel(yk_ref, yv_ref, pek_ref, pev_ref, w1k_ref, w1v_ref, w2k_ref, w2vt_ref, kc_ref, vct_ref):
    nrow = yk_ref.shape[0]

    def hidden(y_ref, pe_ref, w1_ref):
        y = y_ref[...]
        first = _mm(y + pe_ref[0:1, :], w1_ref[0])
        second = _mm(y + pe_ref[1:2, :], w1_ref[1])
        pre = first + pltpu.roll(second, nrow - 1, 0)
        return jax.nn.gelu(pre)

    hk = hidden(yk_ref, pek_ref, w1k_ref)
    kc_ref[...] = jnp.concatenate(
        [_mm(hk[:, g * CMP_HIDDEN:(g + 1) * CMP_HIDDEN], w2k_ref[...]) for g in range(NSA_GROUPS)], axis=1)
    hv = hidden(yv_ref, pev_ref, w1v_ref)
    vct_ref[...] = jnp.concatenate(
        [_mm_nt(w2vt_ref[...], hv[:, g * CMP_HIDDEN:(g + 1) * CMP_HIDDEN]) for g in range(NSA_GROUPS)], axis=0)


def _compress(yk, yv, pek, pev, w1k, w1v, w2k, w2vt):
    b, nrow, width = yk.shape
    hid = NSA_GROUPS * CMP_HIDDEN
    full = lambda shape: pl.BlockSpec(shape, lambda i, n=len(shape): (0,) * n)
    return pl.pallas_call(
        _cmp_kernel,
        out_shape=[jax.ShapeDtypeStruct((b, nrow, NSA_KW), F32), jax.ShapeDtypeStruct((b, NSA_KW, nrow), F32)],
        grid=(b,),
        in_specs=[pl.BlockSpec((None, nrow, width), lambda i: (i, 0, 0)),
                  pl.BlockSpec((None, nrow, width), lambda i: (i, 0, 0)),
                  full((2, width)), full((2, width)), full((2, width, hid)), full((2, width, hid)),
                  full((CMP_HIDDEN, NSA_DK)), full((NSA_DK, CMP_HIDDEN))],
        out_specs=[pl.BlockSpec((None, nrow, NSA_KW), lambda i: (i, 0, 0)),
                   pl.BlockSpec((None, NSA_KW, nrow), lambda i: (i, 0, 0))],
        compiler_params=_params("parallel"),
        name="nsa_compress",
    )(yk, yv, pek, pev, w1k, w1v, w2k, w2vt)


LOG2E = 1.4426950408889634


def _exp2_cols(s):
    return jnp.exp2(s - jnp.max(s, axis=0, keepdims=True))


def _with_ones(vt):
    return jnp.concatenate([vt, jnp.ones((16, vt.shape[1]), BF16)], axis=0)


def _nsa_kernel(xt_ref, z_ref, kc_ref, vct_ref, ks_ref, vst_ref, kw_ref, vwt_ref, ovl_ref, o_ref, sel_s,
                s_even, s_odd, p_even, p_odd, acc_s):
    tq = z_ref.shape[0]
    s_len = ks_ref.shape[0]
    ncmp = kc_ref.shape[0]
    nslc = s_len // SLC_BLOCK
    qt = pl.program_id(1)
    t0 = qt * tq
    scale = NSA_DK ** -0.5 * LOG2E
    ng, hp, nh = NSA_GROUPS, NSA_HPG, NSA_HEADS
    cols = nh * tq
    t_row = t0 + lax.broadcasted_iota(jnp.int32, (1, tq), 1)
    sub = lax.broadcasted_iota(jnp.int32, (LANES, tq), 0)
    low = sub < NSA_DK
    gates = _sigmoid(xt_ref[NSA_QW + 2 * NSA_KW:NSA_QW + 2 * NSA_KW + NSA_GATE_ROWS, :])
    tile = lambda a, n: jnp.concatenate([a] * n, axis=1)
    zeros_half = jnp.zeros((NSA_DK, tq), F32)
    dk = NSA_DK

    def pv_groups(prep, vt, p):
        return jnp.concatenate(
            [jnp.dot(prep(vt[g * dk:(g + 1) * dk]), p[:, g * hp * tq:(g + 1) * hp * tq], preferred_element_type=F32)
             for g in range(ng)], axis=1)

    qs = []
    for head in range(nh):
        qh = xt_ref[head * NSA_DK:(head + 1) * NSA_DK, :] * scale
        qs.append(jnp.concatenate([qh, zeros_half] if head // hp == 0 else [zeros_half, qh], axis=0))
    q8 = _bf(jnp.concatenate(qs, axis=1))

    nwin = WIN + tq
    w0 = pl.multiple_of(jnp.maximum(t0 - WIN, 0), LANES)
    kpos = w0 + lax.broadcasted_iota(jnp.int32, (nwin, 1), 0)
    dq = t_row - kpos
    wmask8 = tile(dq.astype(jnp.uint32) < WIN, nh)
    s_w = jnp.dot(_bf(kw_ref[pl.ds(w0, nwin), :]), q8, preferred_element_type=F32)
    e_w = _exp2_cols(jnp.where(wmask8, s_w, NEG))
    o_win = pv_groups(_with_ones, _bf(vwt_ref[:, pl.ds(w0, nwin)]), _bf(e_w))
    o_win = o_win[:dk] * (1.0 / o_win[dk:dk + 1])

    c_end =lax.broadcasted_iota(jnp.int32, (ncmp, 1), 0) * CMP_STRIDE + (CMP_LEN - 1)
    cmask8 = tile(c_end <= t_row, nh)
    s_c = jnp.dot(_bf(kc_ref[...]), q8, preferred_element_type=F32)
    e_c = _exp2_cols(jnp.where(cmask8, s_c, NEG))
    p_c = jnp.where(cmask8, e_c * (1.0 / jnp.sum(e_c, axis=0, keepdims=True)), 0.0)
    o_cmp = pv_groups(lambda v: v, _bf(vct_ref[...]), _bf(p_c))

    ovl = ovl_ref[...]
    jb = lax.broadcasted_iota(jnp.int32, (nslc, tq), 0)
    cur = t_row // SLC_BLOCK
    forced = (jb == 0) | (jb == cur) | (jb == cur - 1)
    for g in range(ng):
        p_sum = p_c[:, g * hp * tq:(g * hp + 1) * tq]
        for hh in range(1, hp):
            p_sum = p_sum + p_c[:, (g * hp + hh) * tq:(g * hp + hh + 1) * tq]
        imp = sum(jnp.dot(ovl, part, preferred_element_type=F32) for part in _split3(p_sum))
        rank = jnp.where(jb <= cur, jnp.where(forced, BIG, imp), -BIG)
        cnt = jnp.zeros((nslc, tq), F32)
        for i in range(nslc):
            ri = rank[i:i + 1, :]
            tie = jnp.where(jb > i, 1.0, 0.0)
            cnt = cnt + jnp.where(ri > rank, 1.0, jnp.where(ri == rank, tie, 0.0))
        sel = jnp.where(cnt < float(min(N_SEL, nslc)), 0.0, NEG)
        for kt in range(s_len // LANES):
            sel_s[g, kt, 0:2, :] = sel[2 * kt:2 * kt + 2, :]

    def qk(kt):
        return jnp.dot(_bf(ks_ref[pl.ds(pl.multiple_of(kt * LANES, LANES), LANES), :]), q8,
                       preferred_element_type=F32)

    def pv(kt, p):
        return pv_groups(_with_ones, _bf(vst_ref[:, pl.ds(pl.multiple_of(kt * LANES, LANES), LANES)]), p)

    nkt = s_len // LANES

    def one_tile(kt, carry, s_rd, s_wr, p_buf):
        m_run, a_prev = carry
        s_wr[...] = qk(jnp.minimum(kt + 1, nkt - 1))
        pv_old = pv(jnp.maximum(kt - 2, 0), p_buf[...])
        causal = (kt * LANES + sub) <= t_row
        biases = []
        for g in range(ng):
            pair = sel_s[g, kt, 0:2, :]
            biases.append(tile(jnp.where(causal, jnp.where(low, pair[0:1, :], pair[1:2, :]), NEG), hp))
        s_s = s_rd[...] + jnp.concatenate(biases, axis=1)
        m_new = jnp.maximum(m_run, jnp.max(s_s, axis=0, keepdims=True))
        alpha = jnp.exp2(m_run - m_new)
        p_buf[...] = _bf(jnp.exp2(s_s - m_new))
        acc_s[...] = alpha * (acc_s[...] + a_prev * pv_old)
        return m_new, alpha

    def slc_body(j, carry):
        carry = one_tile(2 * j, carry, s_even, s_odd, p_even)
        return one_tile(2 * j + 1, carry, s_odd, s_even, p_odd)

    s_even[...] = qk(0)
    p_even[...] = jnp.zeros((LANES, cols), BF16)
    p_odd[...] = jnp.zeros((LANES, cols), BF16)
    acc_s[...] = jnp.zeros(acc_s.shape, F32)
    npair = ((qt + 1) * tq // LANES + 1) // 2
    _, a_last = lax.fori_loop(0, npair, slc_body, (jnp.full((1, cols), NEG, F32), jnp.ones((1, cols), F32)))
    o_slc = acc_s[...] + a_last * pv(2 * npair - 2, p_even[...]) + pv(2 * npair - 1, p_odd[...])
    o_slc = o_slc[:dk] * (1.0 / o_slc[dk:dk + 1])

    def head_out(head):
        cs = slice(head * tq, (head + 1) * tq)
        row = 3 * head
        return (gates[row:row + 1, :] * o_cmp[:, cs] + gates[row + 1:row + 2, :] * o_slc[:, cs]
                + gates[row + 2:row + 3, :] * o_win[:, cs])

    for hh in range(hp):
        both = jnp.concatenate([head_out(hh), head_out(hp + hh)], axis=0)
        cs = slice(hh * LANES, (hh + 1) * LANES)
        o_ref[:, cs] = both.T * _silu(z_ref[:, cs])


def _nsa_attention(xt, z, kcb, vcbt, ks, kw, ovl, b, s):
    tq = Q_TILE
    ncmp = kcb.shape[1]
    nqt = s // tq
    nslc = s // SLC_BLOCK
    vrow = NSA_QW // NSA_KW
    return pl.pallas_call(
        _nsa_kernel,
        out_shape=jax.ShapeDtypeStruct((b, s, NSA_QW), F32),
        grid=(b, nqt),
        in_specs=[pl.BlockSpec((NSA_T_ROWS, tq), lambda i, t: (0, i * nqt + t)),
                  pl.BlockSpec((None, tq, NSA_QW), lambda i, t: (i, t, 0)),
                  pl.BlockSpec((None, ncmp, NSA_KW), lambda i, t: (i, 0, 0)),
                  pl.BlockSpec((None, NSA_KW, ncmp), lambda i, t: (i, 0, 0)),
                  pl.BlockSpec((None, s, NSA_KW), lambda i, t: (i, 0, 0)),
                  pl.BlockSpec((NSA_KW, s), lambda i, t: (vrow, i)),
                  pl.BlockSpec((None, s, NSA_KW), lambda i, t: (i, 0, 0)),
                  pl.BlockSpec((NSA_KW, s), lambda i, t: (vrow + 1, i)),
                  pl.BlockSpec(ovl.shape, lambda i, t: (0, 0))],
        out_specs=pl.BlockSpec((None, tq, NSA_QW), lambda i, t: (i, t, 0)),
        scratch_shapes=[pltpu.VMEM((NSA_GROUPS, s // LANES, 8, tq), F32),
                        pltpu.VMEM((LANES, NSA_HEADS * tq), F32),
                        pltpu.VMEM((LANES, NSA_HEADS * tq), F32),
                        pltpu.VMEM((LANES, NSA_HEADS * tq), BF16),
                        pltpu.VMEM((LANES, NSA_HEADS * tq), BF16),
                        pltpu.VMEM((NSA_DK + 16, NSA_HEADS * tq), F32)],
        compiler_params=_params("parallel", "arbitrary"),
        name="nsa_attention",
    )(xt, z, kcb, vcbt, ks, xt, kw, xt, ovl)


def _nsa_tables(s):
    ncmp_rows = s // CMP_STRIDE
    nslc = s // SLC_BLOCK
    ci = np.arange(ncmp_rows)[None, :] * CMP_STRIDE
    sj = np.arange(nslc)[:, None] * SLC_BLOCK
    ovl = ((ci < sj + SLC_BLOCK) & (ci + CMP_LEN > sj)).astype(np.float32)
    return jnp.asarray(ovl, BF16)


def _merge_kernel(x_ref, p_ref, oa_ref, ob_ref, oc_ref, wm_ref, wa_ref, wb_ref, wc_ref, wo_ref, wpg_ref, wp_ref,
                  lg_ref, lb_ref, o_ref, *, alpha):
    d = x_ref.shape[1]
    x = x_ref[...]
    xb = _bf(x)
    mixed = None
    for idx, (br_ref, w_ref) in enumerate(((oa_ref, wa_ref), (ob_ref, wb_ref), (oc_ref, wc_ref))):
        gate = _sigmoid(jnp.dot(xb, wm_ref[:, idx * d:(idx + 1) * d], preferred_element_type=F32))
        term = gate * _mm(br_ref[...], w_ref[...])
        mixed = term if mixed is None else mixed + term
    sub = _mm(mixed, wo_ref[...])
    ple = _sigmoid(jnp.dot(xb, wpg_ref[...], preferred_element_type=F32)) * _mm(p_ref[...], wp_ref[...])
    y = alpha * x + sub + ple
    mu = jnp.mean(y, axis=-1, keepdims=True)
    yc = y - mu
    var = jnp.mean(yc * yc, axis=-1, keepdims=True)
    o_ref[...] = yc * lax.rsqrt(var + 1e-5) * lg_ref[...] + lb_ref[...]


def _merge(x2d, p2d, oa, ob, oc, wm, wa, wb, wc, wo, wpg, wp, lg, lb, alpha, tm=512):
    m, d = x2d.shape
    tile = lambda wd: pl.BlockSpec((tm, wd), lambda i: (i, 0))
    full = lambda a: pl.BlockSpec(a.shape, lambda i: (0, 0), pipeline_mode=pl.Buffered(1))
    weights = (wm, wa, wb, wc, wo, wpg, wp, lg, lb)
    return pl.pallas_call(
        functools.partial(_merge_kernel, alpha=alpha),
        out_shape=jax.ShapeDtypeStruct((m, d), F32),
        grid=(m // tm,),
        in_specs=[tile(d), tile(p2d.shape[1]), tile(oa.shape[1]), tile(ob.shape[1]), tile(oc.shape[1])]
        + [full(a) for a in weights],
        out_specs=tile(d),
        compiler_params=_params("parallel"),
        name="merge_out_norm",
    )(x2d, p2d, oa, ob, oc, *weights)


def _nsa_out_perm():
    idx = np.arange(NSA_QW).reshape(NSA_GROUPS, NSA_HPG, NSA_DK)
    return idx.transpose(1, 0, 2).reshape(-1)


def _layer_weights(w_in, conv_w, a_log, dt_bias, norm_w, pe_k, pe_v, w1k, w2k, w1v, w2v, w_br_c, d_model):
    sizes = (RET_W, RET_W, RET_W, RET_W, 3 * GDN_W, GDN_HEADS, GDN_HEADS, GDN_W,
             NSA_QW, NSA_KW, NSA_KW, NSA_KW, NSA_KW, NSA_KW, NSA_KW, 3 * NSA_HEADS, NSA_QW, 3 * d_model)
    offs = np.concatenate([[0], np.cumsum(sizes)])
    col = lambda i: w_in[:, offs[i]:offs[i + 1]]
    perm = np.concatenate([hd * RET_D + np.concatenate([np.arange(0, RET_D, 2), np.arange(1, RET_D, 2)])
                           for hd in range(RET_HEADS)])
    w_ret = _bf(jnp.concatenate([col(0)[:, perm], col(1)[:, perm], col(2), col(3)], axis=1))
    small = jnp.concatenate([col(5), col(6)], axis=1)
    w_gdn = _bf(jnp.concatenate([col(4), col(7), jnp.pad(small, ((0, 0), (0, LANES - small.shape[1])))], axis=1))
    operm = _nsa_out_perm()
    w_nsa = _bf(jnp.concatenate([col(16)[:, operm], col(9), col(10), col(11), col(13)], axis=1))
    gates_t = jnp.pad(col(15).T, ((0, NSA_GATE_ROWS - 3 * NSA_HEADS), (0, 0)))
    w_nsa_t = _bf(jnp.concatenate([col(8).T, col(12).T, col(14).T, gates_t], axis=0))
    w_merge = _bf(col(17))
    lane_ids = jnp.arange(LANES)
    pick = lambda vec: jnp.where((lane_ids >= GDN_HEADS) & (lane_ids < 2 * GDN_HEADS),
                                 jnp.pad(vec, (GDN_HEADS, LANES - 2 * GDN_HEADS)), 0.0)[None, :]
    alog_row = pick(a_log.astype(F32))
    dtb_row = pick(dt_bias.astype(F32))
    alog_lanes = jnp.repeat(a_log.astype(F32), GDN_CHUNK)[None, :]
    dtb_lanes = jnp.repeat(dt_bias.astype(F32), GDN_CHUNK)[None, :]
    half = CMP_LEN // 2

    def big_w1(w1):
        w = w1.reshape(2, half, 1, NSA_DK, 1, CMP_HIDDEN)
        eye = jnp.eye(NSA_GROUPS, dtype=F32).reshape(1, 1, NSA_GROUPS, 1, NSA_GROUPS, 1)
        return _bf((w * eye).reshape(2, half * NSA_GROUPS * NSA_DK, NSA_GROUPS * CMP_HIDDEN))

    def pe_rows(pe):
        t = jnp.broadcast_to(pe.reshape(2, half, 1, NSA_DK), (2, half, NSA_GROUPS, NSA_DK))
        return t.reshape(2, half * NSA_GROUPS * NSA_DK).astype(F32)

    return dict(w_ret=w_ret, w_gdn=w_gdn, w_nsa=w_nsa, w_nsa_t=w_nsa_t, w_merge=w_merge,
                alog_row=alog_row, dtb_row=dtb_row, alog_lanes=alog_lanes, dtb_lanes=dtb_lanes,
                norm_w=norm_w.astype(F32)[None, :], conv_w=conv_w.astype(F32),
                pek=pe_rows(pe_k), pev=pe_rows(pe_v), w1k=big_w1(w1k), w1v=big_w1(w1v), w2k=_bf(w2k),
                w2vt=_bf(w2v.T), w_br_c=_bf(w_br_c[operm, :]))


def _layer(x, p_i, lw, w_br_a, w_br_b, w_out, ln_g, ln_b, w_ple, w_ple_gate, ret_tabs, nsa_ovl, alpha):
    b, s, d = x.shape
    m = b * s
    x2d = x.reshape(m, d)
    r3 = lambda a: a.reshape(b, s, a.shape[-1])

    o_a = _retention(x, lw["w_ret"], ret_tabs)

    qkv, gz, sm, smt = _proj_gdn(x2d, lw["w_gdn"])
    nchunk = s // GDN_CHUNK
    smt = smt.reshape(2, GDN_HEADS, b, nchunk, GDN_CHUNK).transpose(2, 3, 0, 1, 4)
    smt = jnp.pad(smt.reshape(b * nchunk, 2, GDN_HEADS * GDN_CHUNK), ((0, 0), (0, 6), (0, 0)))
    o_b = _gdn(qkv, gz, sm, smt, lw["conv_w"], lw["alog_row"], lw["dtb_row"], lw["alog_lanes"],
               lw["dtb_lanes"], lw["norm_w"], s)

    nz, kc, vc, ks, kw, xt = _proj_t(x2d, lw["w_nsa"], lw["w_nsa_t"], (NSA_QW,) + (NSA_KW,) * 4, name="proj_nsa")
    slab = lambda a: a.reshape(b, s // CMP_STRIDE, CMP_STRIDE * NSA_KW)
    kcb, vcbt = _compress(slab(kc), slab(vc), lw["pek"], lw["pev"], lw["w1k"], lw["w1v"], lw["w2k"], lw["w2vt"])
    o_c = _nsa_attention(xt, r3(nz), kcb, vcbt, r3(ks), r3(kw), nsa_ovl, b, s)

    out = _merge(x2d, p_i.reshape(m, -1), o_a.reshape(m, -1), o_b.reshape(m, -1), o_c.reshape(m, -1),
                 lw["w_merge"], _bf(w_br_a), _bf(w_br_b), lw["w_br_c"], _bf(w_out), _bf(w_ple_gate), _bf(w_ple),
                 ln_g.astype(F32)[None, :], ln_b.astype(F32)[None, :], alpha)
    return out.reshape(b, s, d)


def kernel(x, p, w_in, gdn_conv_w, gdn_a_log, gdn_dt_bias, gdn_norm_w, nsa_pe_k, nsa_pe_v, nsa_cmp_w1k, nsa_cmp_w2k, nsa_cmp_w1v, nsa_cmp_w2v, w_branch_a, w_branch_b, w_branch_c, w_out, ln_g, ln_b, w_ple, w_ple_gate):
    depth = w_in.shape[0]
    s, d = x.shape[1], x.shape[2]
    alpha = (2 * depth) ** 0.25
    ret_tabs = _ret_tables(s)
    nsa_ovl = _nsa_tables(s)
    for i in range(depth):
        lw = _layer_weights(w_in[i], gdn_conv_w[i], gdn_a_log[i], gdn_dt_bias[i], gdn_norm_w[i], nsa_pe_k[i],
                            nsa_pe_v[i], nsa_cmp_w1k[i], nsa_cmp_w2k[i], nsa_cmp_w1v[i], nsa_cmp_w2v[i],
                            w_branch_c[i], d)
        x = _layer(x, p[i], lw, w_branch_a[i], w_branch_b[i], w_out[i], ln_g[i], ln_b[i],
                   w_ple[i], w_ple_gate[i], ret_tabs, nsa_ovl, alpha)
    return x
```

```python
import functools

import numpy as np
import jax
import jax.numpy as jnp
from jax import lax
from jax.experimental import pallas as pl
from jax.experimental.pallas import tpu as pltpu

F32 = jnp.float32
BF16 = jnp.bfloat16

RET_HEADS, RET_D, RET_CHUNK = 4, 128, 128
ROPE_BASE = 10000.0
GDN_HEADS, GDN_D, GDN_CHUNK, CONV_K = 4, 128, 64, 4
NSA_HEADS, NSA_GROUPS, NSA_DK = 8, 2, 64
NSA_HPG = NSA_HEADS // NSA_GROUPS
CMP_LEN, CMP_STRIDE, CMP_HIDDEN = 32, 16, 128
SLC_BLOCK, N_SEL, WIN = 64, 8, 256
NEG, BIG, EPS = -1e30, 1e30, 1e-6
LANES = 128
Q_TILE = 256
GDN_UNROLL = 4
VMEM_LIMIT = 56 * 1024 * 1024

RET_W = RET_HEADS * RET_D
GDN_W = GDN_HEADS * GDN_D
NSA_QW = NSA_HEADS * NSA_DK
NSA_KW = NSA_GROUPS * NSA_DK
NSA_GATE_ROWS = 32
NSA_T_ROWS = NSA_QW + 2 * NSA_KW + NSA_GATE_ROWS


def _bf(x):
    return x.astype(BF16)


def _mm(a, b):
    return jnp.dot(_bf(a), _bf(b), preferred_element_type=F32)


def _mm_nt(a, b):
    return lax.dot_general(_bf(a), _bf(b), (((1,), (1,)), ((), ())), preferred_element_type=F32)


def _split3(x):
    hi = _bf(x)
    r1 = x - hi.astype(F32)
    mid = _bf(r1)
    lo = _bf(r1 - mid.astype(F32))
    return hi, mid, lo


def _sigmoid(x):
    return 0.5 * jnp.tanh(0.5 * x) + 0.5


def _silu(x):
    return x * _sigmoid(x)


def _softplus(x):
    return jnp.maximum(x, 0.0) + jnp.log(1.0 + jnp.exp(-jnp.abs(x)))


def _params(*sem):
    return pltpu.CompilerParams(dimension_semantics=sem, vmem_limit_bytes=VMEM_LIMIT)


def _proj_nsa_kernel(x_ref, w_ref, wt_ref, z_ref, kcs_ref, vcs_ref, ks_ref, kw_ref, xt_ref, kc_s, vc_s):
    x = _bf(x_ref[...])
    tm = x_ref.shape[0]
    nz, nk = z_ref.shape[1], ks_ref.shape[1]
    z_ref[...] = jnp.dot(x, w_ref[:, :nz], preferred_element_type=F32)
    for idx, (dst, tmp) in enumerate(((kcs_ref, kc_s), (vcs_ref, vc_s))):
        tmp[...] = jnp.dot(x, w_ref[:, nz + idx * nk:nz + (idx + 1) * nk], preferred_element_type=F32)
        dst[...] = jnp.concatenate([tmp[pl.ds(tok, tm // CMP_STRIDE, stride=CMP_STRIDE), :]
                                    for tok in range(CMP_STRIDE)], axis=1)
    ks_ref[...] = jnp.dot(x, w_ref[:, nz + 2 * nk:nz + 3 * nk], preferred_element_type=F32)
    kw_ref[...] = jnp.dot(x, w_ref[:, nz + 3 * nk:nz + 4 * nk], preferred_element_type=F32)
    xt_ref[...] = lax.dot_general(wt_ref[...], x, (((1,), (1,)), ((), ())), preferred_element_type=F32)


def _proj_nsa(x2d, w, wt, tm=512):
    m, k = x2d.shape
    n = w.shape[1]
    nt = wt.shape[0]
    slab_w = CMP_STRIDE * NSA_KW
    assert n == NSA_QW + 4 * NSA_KW and m % tm == 0 and tm % (8 * CMP_STRIDE) == 0
    row = lambda wd: pl.BlockSpec((tm, wd), lambda i: (i, 0))
    slab = lambda: pl.BlockSpec((tm // CMP_STRIDE, slab_w), lambda i: (i, 0))
    return pl.pallas_call(
        _proj_nsa_kernel,
        out_shape=[jax.ShapeDtypeStruct((m, NSA_QW), F32), jax.ShapeDtypeStruct((m // CMP_STRIDE, slab_w), F32),
                   jax.ShapeDtypeStruct((m // CMP_STRIDE, slab_w), F32), jax.ShapeDtypeStruct((m, NSA_KW), F32),
                   jax.ShapeDtypeStruct((m, NSA_KW), F32), jax.ShapeDtypeStruct((nt, m), F32)],
        grid=(m // tm,),
        in_specs=[pl.BlockSpec((tm, k), lambda i: (i, 0)), pl.BlockSpec((k, n), lambda i: (0, 0)),
                  pl.BlockSpec((nt, k), lambda i: (0, 0))],
        out_specs=[row(NSA_QW), slab(), slab(), row(NSA_KW), row(NSA_KW), pl.BlockSpec((nt, tm), lambda i: (0, i))],
        scratch_shapes=[pltpu.VMEM((tm, NSA_KW), F32), pltpu.VMEM((tm, NSA_KW), F32)],
        compiler_params=_params("parallel"),
        name="proj_nsa",
    )(x2d, w, wt)


def _proj_gdn_kernel(x_ref, w_ref, qkv_ref, z_ref, sm_ref, smt_ref):
    x = _bf(x_ref[...])
    nq, nz = qkv_ref.shape[1], z_ref.shape[1]
    qkv_ref[...] = jnp.dot(x, w_ref[:, :nq], preferred_element_type=F32)
    z_ref[...] = jnp.dot(x, w_ref[:, nq:nq + nz], preferred_element_type=F32)
    sm = jnp.dot(x, w_ref[:, nq + nz:], preferred_element_type=F32)
    sm_ref[...] = sm
    smt_ref[...] = sm.T[0:8, :]


def _proj_gdn(x2d, w, tm=512):
    m, k = x2d.shape
    n = w.shape[1]
    widths = (3 * GDN_W, GDN_W, LANES)
    assert sum(widths) == n and m % tm == 0
    return pl.pallas_call(
        _proj_gdn_kernel,
        out_shape=[jax.ShapeDtypeStruct((m, wd), F32) for wd in widths] + [jax.ShapeDtypeStruct((8, m), F32)],
        grid=(m // tm,),
        in_specs=[pl.BlockSpec((tm, k), lambda i: (i, 0)), pl.BlockSpec((k, n), lambda i: (0, 0))],
        out_specs=[pl.BlockSpec((tm, wd), lambda i: (i, 0)) for wd in widths] + [pl.BlockSpec((8, tm), lambda i: (0, i))],
        compiler_params=_params("parallel"),
        name="proj_gdn",
    )(x2d, w)


def _ret_kernel(x_ref, w_ref, cos_ref, sin_ref, dmask_ref, qdec_ref, kdec_ref, cdec_ref, o_ref, state_s):
    ts = x_ref.shape[0]
    c = RET_CHUNK
    half = RET_D // 2
    scale = RET_D ** -0.5

    @pl.when(pl.program_id(1) == 0)
    def _():
        state_s[...] = jnp.zeros_like(state_s)

    def project(n):
        return jnp.dot(_bf(x_ref[n * c:(n + 1) * c, :]), w_ref[...], preferred_element_type=F32)

    def chunk(n, hx):
        sl = slice(n * c, (n + 1) * c)
        cos = cos_ref[sl, :]
        sin = sin_ref[sl, :]
        for h in range(RET_HEADS):
            hs = slice(h * RET_D, (h + 1) * RET_D)
            q = hx[:, h * RET_D:(h + 1) * RET_D]
            k = hx[:, RET_W + h * RET_D:RET_W + (h + 1) * RET_D]
            v = hx[:, 2 * RET_W + h * RET_D:2 * RET_W + (h + 1) * RET_D]
            state = state_s[h]
            q = q * cos + pltpu.roll(q, half, 1) * sin
            k = (k * cos + pltpu.roll(k, half, 1) * sin) * scale
            scores = _mm_nt(q, k) * dmask_ref[h]
            o = _mm(scores, v) + _mm(q * qdec_ref[h], state)
            state_s[h] = cdec_ref[h] * state + _mm((k * kdec_ref[h]).T, v)
            mu = jnp.mean(o, axis=-1, keepdims=True)
            d = o - mu
            var = jnp.mean(d * d, axis=-1, keepdims=True)
            o_ref[sl, hs] = d * lax.rsqrt(var + 1e-5) * _silu(hx[:, 3 * RET_W + h * RET_D:3 * RET_W + (h + 1) * RET_D])

    hx = project(0)
    for n in range(ts // c):
        hx_next = project(n + 1) if n + 1 < ts // c else None
        chunk(n, hx)
        hx = hx_next


def _retention(x, w, tabs, ts=512):
    b, s, dm = x.shape
    cos, sin, dmask, qdec, kdec, cdec = tabs
    seq = lambda: pl.BlockSpec((None, ts, RET_W), lambda i, t: (i, t, 0))
    tab = lambda: pl.BlockSpec((RET_HEADS, RET_CHUNK, RET_D), lambda i, t: (0, 0, 0))
    rot = lambda: pl.BlockSpec((ts, RET_D), lambda i, t: (t, 0))
    return pl.pallas_call(
        _ret_kernel,
        out_shape=jax.ShapeDtypeStruct((b, s, RET_W), F32),
        grid=(b, s // ts),
        in_specs=[pl.BlockSpec((None, ts, dm), lambda i, t: (i, t, 0)), pl.BlockSpec(w.shape, lambda i, t: (0, 0)),
                  rot(), rot(), tab(), tab(), tab(), tab()],
        out_specs=seq(),
        scratch_shapes=[pltpu.VMEM((RET_HEADS, RET_D, RET_D), F32)],
        compiler_params=_params("parallel", "arbitrary"),
        name="retention",
    )(x, w, cos, sin, dmask, qdec, kdec, cdec)


def _ret_tables(s):
    inv = ROPE_BASE ** (-jnp.arange(0, RET_D, 2, dtype=F32) / RET_D)
    ang = jnp.arange(s, dtype=F32)[:, None] * inv[None, :]
    cos = jnp.concatenate([jnp.cos(ang), jnp.cos(ang)], axis=-1)
    sin = jnp.concatenate([-jnp.sin(ang), jnp.sin(ang)], axis=-1)
    c = RET_CHUNK
    log_g = jnp.log1p(-jnp.exp2(-5.0 - jnp.arange(RET_HEADS, dtype=F32)))
    pos = jnp.arange(c, dtype=F32)
    diff = pos[:, None] - pos[None, :]
    causal = diff >= 0
    dmask = jnp.where(causal[None], jnp.exp(jnp.where(causal, diff, 0.0)[None] * log_g[:, None, None]), 0.0)
    ones = jnp.ones((1, 1, RET_D), F32)
    kdec = jnp.exp((c - 1 - pos)[None, :, None] * log_g[:, None, None]) * ones
    qdec = jnp.exp((pos + 1.0)[None, :, None] * log_g[:, None, None]) * ones
    cdec = jnp.exp(c * log_g)[:, None, None] * jnp.ones((1, c, RET_D), F32)
    return cos, sin, dmask, qdec, kdec, cdec


def _gdn_kernel(qkv_ref, z_ref, sm_ref, smt_ref, cw_ref, alog_ref, dtb_ref, alog_l_ref, dtb_l_ref, nw_ref, o_ref,
                state_s, xs_s, h_qe, h_kd, h_mp, h_r, h_at, h_dec,
                lhs_s, ou_s, psi_s, dec_s, *, nt):
    ts = qkv_ref.shape[0]
    c, d, nh = GDN_CHUNK, GDN_D, GDN_HEADS
    hc = nh * c
    nct = ts // c
    g = pl.program_id(0)
    ri = lax.broadcasted_iota(jnp.int32, (hc, hc), 0)
    ci = lax.broadcasted_iota(jnp.int32, (hc, hc), 1)
    same = (ri // c) == (ci // c)
    incl = same & (ri >= ci)
    strict = same & (ri > ci)
    upper = same & (ri <= ci)
    neg_a = -jnp.exp(alog_ref[...])
    dtb = dtb_ref[...]
    neg_a_l = -jnp.exp(alog_l_ref[...])
    dtb_l = dtb_l_ref[...]
    nw = nw_ref[...]

    @pl.when(g == 0)
    def _():
        for ref in (state_s, h_qe, h_kd, h_mp, h_r, h_at, h_dec):
            ref[...] = jnp.zeros_like(ref)

    @pl.when(g % nt == 0)
    def _():
        xs_s[0:8, :] = jnp.zeros((8, xs_s.shape[1]), F32)

    xs_s[8:8 + ts, :] = qkv_ref[...]
    wr = g % 2
    rd = 1 - wr

    def conv_silu_stack(n, part):
        start = pl.multiple_of(n * c, c)
        cols = slice(part * nh * d, (part + 1) * nh * d)
        w = cw_ref[:, cols]
        ext = xs_s[pl.ds(start, c + 8), cols]
        y = ext[8:8 + c, :] * w[CONV_K - 1:CONV_K, :]
        for j in range(1, CONV_K):
            y = y + ext[8 - j:8 - j + c, :] * w[CONV_K - 1 - j:CONV_K - j, :]
        y = _silu(y)
        return [y[:, h * d:(h + 1) * d] for h in range(nh)]

    def l2n(xs, mult):
        out = [x * (lax.rsqrt(jnp.sum(x * x, axis=-1, keepdims=True) + EPS) * mult) for x in xs]
        return jnp.concatenate(out, axis=0)

    def prep(it, carry):
        chunks = [it * GDN_UNROLL + u for u in range(GDN_UNROLL)]
        prev = [dict(qe=h_qe[rd, n], kd=h_kd[rd, n], mp=h_mp[rd, n], r=h_r[rd, n], at=h_at[rd, n], dec=h_dec[rd, n])
                for n in chunks]
        st = []
        for n in chunks:
            sl = pl.ds(pl.multiple_of(n * c, c), c)
            q = l2n(conv_silu_stack(n, 0), d ** -0.5)
            k = l2n(conv_silu_stack(n, 1), 1.0)
            v = jnp.concatenate(conv_silu_stack(n, 2), axis=0)
            sm = sm_ref[sl, :]
            beta_col = jnp.concatenate([_sigmoid(sm[:, h:h + 1]) for h in range(nh)], axis=0)
            g_col = jnp.concatenate(
                [neg_a[:, nh + h:nh + h + 1] * _softplus(sm[:, nh + h:nh + h + 1] + dtb[:, nh + h:nh + h + 1])
                 for h in range(nh)], axis=0)
            smt = smt_ref[n]
            g_row = neg_a_l * _softplus(smt[1:2, :] + dtb_l)
            g_row_b = jnp.broadcast_to(g_row, (hc, hc))
            gc_col = jnp.sum(jnp.where(incl, g_row_b, 0.0), axis=1, keepdims=True)
            gc_row = jnp.sum(jnp.where(upper, jnp.broadcast_to(g_col, (hc, hc)), 0.0), axis=0, keepdims=True)
            g_last = jnp.sum(jnp.where(same, g_row_b, 0.0), axis=1, keepdims=True)
            decay = jnp.where(incl, jnp.exp(jnp.where(incl, gc_col - gc_row, 0.0)), 0.0)
            egc = jnp.exp(gc_col)
            st.append(dict(q=q, k=k, v=v, beta=beta_col, gc_col=gc_col, g_last=g_last, decay=decay, egc=egc))
        for s_ in st:
            qk_kk = _mm_nt(jnp.concatenate([s_["q"], s_["k"]], axis=0), s_["k"])
            s_["attn"] = qk_kk[:hc] * s_["decay"]
            s_["mp"] = _bf(-jnp.where(strict, s_["beta"] * qk_kk[hc:] * s_["decay"], 0.0))
            s_["r"] = jnp.concatenate([s_["v"] * s_["beta"], s_["k"] * (s_["beta"] * s_["egc"])], axis=1)
        for p_ in prev:
            p_["r"] = p_["r"] + jnp.dot(p_["mp"], _bf(p_["r"]), preferred_element_type=F32)
        for _ in range(5):
            for p_ in prev:
                p_["mp"] = _bf(jnp.dot(p_["mp"], p_["mp"], preferred_element_type=F32))
            for p_ in prev:
                p_["r"] = p_["r"] + jnp.dot(p_["mp"], _bf(p_["r"]), preferred_element_type=F32)
        for n, p_ in zip(chunks, prev):
            rb = _bf(p_["r"])
            ar = jnp.dot(p_["at"], rb, preferred_element_type=F32)
            qt = p_["qe"] - ar[:, d:]
            for h in range(nh):
                rows = slice(h * c, (h + 1) * c)
                kr = jnp.dot(_bf(p_["kd"][rows].T), rb[rows], preferred_element_type=F32)
                lhs_s[h, n, 0:c, :] = _bf(qt[rows])
                lhs_s[h, n, c:c + d, :] = _bf(-kr[:, d:])
                ou_s[h, n] = ar[rows, :d]
                psi_s[h, n] = kr[:, :d]
                dec_s[h, n] = p_["dec"][h]
        for n, s_ in zip(chunks, st):
            h_qe[wr, n] = s_["q"] * s_["egc"]
            h_kd[wr, n] = s_["k"] * jnp.exp(s_["g_last"] - s_["gc_col"])
            h_mp[wr, n] = s_["mp"]
            h_r[wr, n] = s_["r"]
            h_at[wr, n] = _bf(s_["attn"])
            for h in range(nh):
                h_dec[wr, n, h] = jnp.broadcast_to(jnp.exp(s_["g_last"][h * c:h * c + 1, :]), (8, d))
        return carry

    lax.fori_loop(0, nct // GDN_UNROLL, prep, 0)

    @pl.when((g + nt - 1) % nt == 0)
    def _():
        state_s[...] = jnp.zeros_like(state_s)

    def scan(n, carry):
        sl = pl.ds(pl.multiple_of(n * c, c), c)
        for h in range(nh):
            hs = slice(h * d, (h + 1) * d)
            state = state_s[h]
            res = jnp.dot(lhs_s[h, n], _bf(state), preferred_element_type=F32)
            o = res[:c] + ou_s[h, n]
            state_s[h] = dec_s[h, n][0:1, :] * state + res[c:] + psi_s[h, n]
            o = o * lax.rsqrt(jnp.mean(o * o, axis=-1, keepdims=True) + EPS) * nw
            o_ref[sl, hs] = o * _silu(z_ref[sl, hs])
        return carry

    lax.fori_loop(0, nct, scan, 0, unroll=True)
    xs_s[0:8, :] = qkv_ref[ts - 8:ts, :]


def _gdn(qkv, z, sm, smt, conv_w, alog_row, dtb_row, alog_lanes, dtb_lanes, norm_w, s, ts=512):
    m, wq = qkv.shape
    nh, c, d = GDN_HEADS, GDN_CHUNK, GDN_D
    hc = nh * c
    nct = ts // c
    nt = s // ts
    ntile = m // ts
    assert nct % GDN_UNROLL == 0 and s % ts == 0
    front = lambda wd: pl.BlockSpec((ts, wd), lambda g: (jnp.minimum(g, ntile - 1), 0))
    back = lambda wd: pl.BlockSpec((ts, wd), lambda g: (jnp.maximum(g - 1, 0), 0))
    full = lambda a: pl.BlockSpec(a.shape, lambda g, n=a.ndim: (0,) * n)
    return pl.pallas_call(
        functools.partial(_gdn_kernel, nt=nt),
        out_shape=jax.ShapeDtypeStruct((m, GDN_W), F32),
        grid=(ntile + 1,),
        in_specs=[front(wq), back(GDN_W), front(LANES),
                  pl.BlockSpec((nct, 8, hc), lambda g: (jnp.minimum(g, ntile - 1), 0, 0)),
                  full(conv_w), full(alog_row), full(dtb_row), full(alog_lanes), full(dtb_lanes), full(norm_w)],
        out_specs=back(GDN_W),
        scratch_shapes=[pltpu.VMEM((nh, d, d), F32),
                        pltpu.VMEM((ts + 8, wq), F32),
                        pltpu.VMEM((2, nct, hc, d), F32),
                        pltpu.VMEM((2, nct, hc, d), F32),
                        pltpu.VMEM((2, nct, hc, hc), BF16),
                        pltpu.VMEM((2, nct, hc, 2 * d), F32),
                        pltpu.VMEM((2, nct, hc, hc), BF16),
                        pltpu.VMEM((2, nct, nh, 8, d), F32),
                        pltpu.VMEM((nh, nct, c + d, d), BF16),
                        pltpu.VMEM((nh, nct, c, d), F32),
                        pltpu.VMEM((nh, nct, d, d), F32),
                        pltpu.VMEM((nh, nct, 8, d), F32)],
        compiler_params=_params("arbitrary"),
        name="gated_delta_net",
    )(qkv, z, sm, smt, conv_w, alog_row, dtb_row, alog_lanes, dtb_lanes, norm_w)


def _cmp_kernel(yk_ref, yv_ref, pek_ref, pev_ref, w1k_ref, w1v_ref, w2k_ref, w2vt_ref, kc_ref, vct_ref):
    nrow = yk_ref.shape[0]

    def hidden(y_ref, pe_ref, w1_ref):
        y = y_ref[...]
        first = _mm(y + pe_ref[0:1, :], w1_ref[0])
        second = _mm(y + pe_ref[1:2, :], w1_ref[1])
        pre = first + pltpu.roll(second, nrow - 1, 0)
        return jax.nn.gelu(pre)

    hk = hidden(yk_ref, pek_ref, w1k_ref)
    kc_ref[...] = jnp.concatenate(
        [_mm(hk[:, g * CMP_HIDDEN:(g + 1) * CMP_HIDDEN], w2k_ref[...]) for g in range(NSA_GROUPS)], axis=1)
    hv = hidden(yv_ref, pev_ref, w1v_ref)
    vct_ref[...] = jnp.concatenate(
        [_mm_nt(w2vt_ref[...], hv[:, g * CMP_HIDDEN:(g + 1) * CMP_HIDDEN]) for g in range(NSA_GROUPS)], axis=0)


def _compress(yk, yv, pek, pev, w1k, w1v, w2k, w2vt):
    b, nrow, width = yk.shape
    hid = NSA_GROUPS * CMP_HIDDEN
    full = lambda shape: pl.BlockSpec(shape, lambda i, n=len(shape): (0,) * n)
    return pl.pallas_call(
        _cmp_kernel,
        out_shape=[jax.ShapeDtypeStruct((b, nrow, NSA_KW), F32), jax.ShapeDtypeStruct((b, NSA_KW, nrow), F32)],
        grid=(b,),
        in_specs=[pl.BlockSpec((None, nrow, width), lambda i: (i, 0, 0)),
                  pl.BlockSpec((None, nrow, width), lambda i: (i, 0, 0)),
                  full((2, width)), full((2, width)), full((2, width, hid)), full((2, width, hid)),
                  full((CMP_HIDDEN, NSA_DK)), full((NSA_DK, CMP_HIDDEN))],
        out_specs=[pl.BlockSpec((None, nrow, NSA_KW), lambda i: (i, 0, 0)),
                   pl.BlockSpec((None, NSA_KW, nrow), lambda i: (i, 0, 0))],
        compiler_params=_params("parallel"),
        name="nsa_compress",
    )(yk, yv, pek, pev, w1k, w1v, w2k, w2vt)


LOG2E = 1.4426950408889634


def _exp2_cols(s):
    return jnp.exp2(s - jnp.max(s, axis=0, keepdims=True))


def _with_ones(vt):
    return jnp.concatenate([vt, jnp.ones((16, vt.shape[1]), BF16)], axis=0)


def _nsa_kernel(xt_ref, z_ref, kc_ref, vct_ref, ks_ref, vst_ref, kw_ref, vwt_ref, ovl_ref, o_ref, sel_s,
                s_even, s_odd, p_even, p_odd, acc_s):
    tq = z_ref.shape[0]
    s_len = ks_ref.shape[0]
    ncmp = kc_ref.shape[0]
    nslc = s_len // SLC_BLOCK
    qt = pl.program_id(1)
    t0 = qt * tq
    scale = NSA_DK ** -0.5 * LOG2E
    ng, hp, nh = NSA_GROUPS, NSA_HPG, NSA_HEADS
    cols = nh * tq
    t_row = t0 + lax.broadcasted_iota(jnp.int32, (1, tq), 1)
    sub = lax.broadcasted_iota(jnp.int32, (LANES, tq), 0)
    low = sub < NSA_DK
    gates = _sigmoid(xt_ref[NSA_QW + 2 * NSA_KW:NSA_QW + 2 * NSA_KW + NSA_GATE_ROWS, :])
    tile = lambda a, n: jnp.concatenate([a] * n, axis=1)
    zeros_half = jnp.zeros((NSA_DK, tq), F32)
    dk = NSA_DK

    def pv_groups(prep, vt, p):
        return jnp.concatenate(
            [jnp.dot(prep(vt[g * dk:(g + 1) * dk]), p[:, g * hp * tq:(g + 1) * hp * tq], preferred_element_type=F32)
             for g in range(ng)], axis=1)

    qs = []
    for head in range(nh):
        qh = xt_ref[head * NSA_DK:(head + 1) * NSA_DK, :] * scale
        qs.append(jnp.concatenate([qh, zeros_half] if head // hp == 0 else [zeros_half, qh], axis=0))
    q8 = _bf(jnp.concatenate(qs, axis=1))

    nwin = WIN + tq
    w0 = pl.multiple_of(jnp.maximum(t0 - WIN, 0), LANES)
    kpos = w0 + lax.broadcasted_iota(jnp.int32, (nwin, 1), 0)
    dq = t_row - kpos
    wmask8 = tile(dq.astype(jnp.uint32) < WIN, nh)
    s_w = jnp.dot(_bf(kw_ref[pl.ds(w0, nwin), :]), q8, preferred_element_type=F32)
    e_w = _exp2_cols(jnp.where(wmask8, s_w, NEG))
    o_win = pv_groups(_with_ones, _bf(vwt_ref[:, pl.ds(w0, nwin)]), _bf(e_w))
    o_win = o_win[:dk] * (1.0 / o_win[dk:dk + 1])

    c_end =lax.broadcasted_iota(jnp.int32, (ncmp, 1), 0) * CMP_STRIDE + (CMP_LEN - 1)
    cmask8 = tile(c_end <= t_row, nh)
    s_c = jnp.dot(_bf(kc_ref[...]), q8, preferred_element_type=F32)
    e_c = _exp2_cols(jnp.where(cmask8, s_c, NEG))
    p_c = jnp.where(cmask8, e_c * (1.0 / jnp.sum(e_c, axis=0, keepdims=True)), 0.0)
    o_cmp = pv_groups(lambda v: v, _bf(vct_ref[...]), _bf(p_c))

    ovl = ovl_ref[...]
    jb = lax.broadcasted_iota(jnp.int32, (nslc, tq), 0)
    cur = t_row // SLC_BLOCK
    forced = (jb == 0) | (jb == cur) | (jb == cur - 1)
    for g in range(ng):
        p_sum = p_c[:, g * hp * tq:(g * hp + 1) * tq]
        for hh in range(1, hp):
            p_sum = p_sum + p_c[:, (g * hp + hh) * tq:(g * hp + hh + 1) * tq]
        imp = sum(jnp.dot(ovl, part, preferred_element_type=F32) for part in _split3(p_sum))
        rank = jnp.where(jb <= cur, jnp.where(forced, BIG, imp), -BIG)
        cnt = jnp.zeros((nslc, tq), F32)
        for i in range(nslc):
            ri = rank[i:i + 1, :]
            tie = jnp.where(jb > i, 1.0, 0.0)
            cnt = cnt + jnp.where(ri > rank, 1.0, jnp.where(ri == rank, tie, 0.0))
        sel = jnp.where(cnt < float(min(N_SEL, nslc)), 0.0, NEG)
        for kt in range(s_len // LANES):
            sel_s[g, kt, 0:2, :] = sel[2 * kt:2 * kt + 2, :]

    def qk(kt):
        return jnp.dot(_bf(ks_ref[pl.ds(pl.multiple_of(kt * LANES, LANES), LANES), :]), q8,
                       preferred_element_type=F32)

    def pv(kt, p):
        return pv_groups(_with_ones, _bf(vst_ref[:, pl.ds(pl.multiple_of(kt * LANES, LANES), LANES)]), p)

    nkt = s_len // LANES

    def one_tile(kt, carry, s_rd, s_wr, p_buf):
        m_run, a_prev = carry
        s_wr[...] = qk(jnp.minimum(kt + 1, nkt - 1))
        pv_old = pv(jnp.maximum(kt - 2, 0), p_buf[...])
        causal = (kt * LANES + sub) <= t_row
        biases = []
        for g in range(ng):
            pair = sel_s[g, kt, 0:2, :]
            biases.append(tile(jnp.where(causal, jnp.where(low, pair[0:1, :], pair[1:2, :]), NEG), hp))
        s_s = s_rd[...] + jnp.concatenate(biases, axis=1)
        m_new = jnp.maximum(m_run, jnp.max(s_s, axis=0, keepdims=True))
        alpha = jnp.exp2(m_run - m_new)
        p_buf[...] = _bf(jnp.exp2(s_s - m_new))
        acc_s[...] = alpha * (acc_s[...] + a_prev * pv_old)
        return m_new, alpha

    def slc_body(j, carry):
        carry = one_tile(2 * j, carry, s_even, s_odd, p_even)
        return one_tile(2 * j + 1, carry, s_odd, s_even, p_odd)

    s_even[...] = qk(0)
    p_even[...] = jnp.zeros((LANES, cols), BF16)
    p_odd[...] = jnp.zeros((LANES, cols), BF16)
    acc_s[...] = jnp.zeros(acc_s.shape, F32)
    npair = ((qt + 1) * tq // LANES + 1) // 2
    _, a_last = lax.fori_loop(0, npair, slc_body, (jnp.full((1, cols), NEG, F32), jnp.ones((1, cols), F32)))
    o_slc = acc_s[...] + a_last * pv(2 * npair - 2, p_even[...]) + pv(2 * npair - 1, p_odd[...])
    o_slc = o_slc[:dk] * (1.0 / o_slc[dk:dk + 1])

    def head_out(head):
        cs = slice(head * tq, (head + 1) * tq)
        row = 3 * head
        return (gates[row:row + 1, :] * o_cmp[:, cs] + gates[row + 1:row + 2, :] * o_slc[:, cs]
                + gates[row + 2:row + 3, :] * o_win[:, cs])

    for hh in range(hp):
        both = jnp.concatenate([head_out(hh), head_out(hp + hh)], axis=0)
        cs = slice(hh * LANES, (hh + 1) * LANES)
        o_ref[:, cs] = both.T * _silu(z_ref[:, cs])


def _nsa_attention(xt, z, kcb, vcbt, ks, kw, ovl, b, s):
    tq = Q_TILE
    ncmp = kcb.shape[1]
    nqt = s // tq
    nslc = s // SLC_BLOCK
    vrow = NSA_QW // NSA_KW
    return pl.pallas_call(
        _nsa_kernel,
        out_shape=jax.ShapeDtypeStruct((b, s, NSA_QW), F32),
        grid=(b, nqt),
        in_specs=[pl.BlockSpec((NSA_T_ROWS, tq), lambda i, t: (0, i * nqt + t)),
                  pl.BlockSpec((None, tq, NSA_QW), lambda i, t: (i, t, 0)),
                  pl.BlockSpec((None, ncmp, NSA_KW), lambda i, t: (i, 0, 0)),
                  pl.BlockSpec((None, NSA_KW, ncmp), lambda i, t: (i, 0, 0)),
                  pl.BlockSpec((None, s, NSA_KW), lambda i, t: (i, 0, 0)),
                  pl.BlockSpec((NSA_KW, s), lambda i, t: (vrow, i)),
                  pl.BlockSpec((None, s, NSA_KW), lambda i, t: (i, 0, 0)),
                  pl.BlockSpec((NSA_KW, s), lambda i, t: (vrow + 1, i)),
                  pl.BlockSpec(ovl.shape, lambda i, t: (0, 0))],
        out_specs=pl.BlockSpec((None, tq, NSA_QW), lambda i, t: (i, t, 0)),
        scratch_shapes=[pltpu.VMEM((NSA_GROUPS, s // LANES, 8, tq), F32),
                        pltpu.VMEM((LANES, NSA_HEADS * tq), F32),
                        pltpu.VMEM((LANES, NSA_HEADS * tq), F32),
                        pltpu.VMEM((LANES, NSA_HEADS * tq), BF16),
                        pltpu.VMEM((LANES, NSA_HEADS * tq), BF16),
                        pltpu.VMEM((NSA_DK + 16, NSA_HEADS * tq), F32)],
        compiler_params=_params("parallel", "arbitrary"),
        name="nsa_attention",
    )(xt, z, kcb, vcbt, ks, xt, kw, xt, ovl)


def _nsa_tables(s):
    ncmp_rows = s // CMP_STRIDE
    nslc = s // SLC_BLOCK
    ci = np.arange(ncmp_rows)[None, :] * CMP_STRIDE
    sj = np.arange(nslc)[:, None] * SLC_BLOCK
    ovl = ((ci < sj + SLC_BLOCK) & (ci + CMP_LEN > sj)).astype(np.float32)
    return jnp.asarray(ovl, BF16)


def _merge_kernel(x_ref, p_ref, oa_ref, ob_ref, oc_ref, wm_ref, wa_ref, wb_ref, wc_ref, wo_ref, wpg_ref, wp_ref,
                  lg_ref, lb_ref, o_ref, *, alpha):
    d = x_ref.shape[1]
    x = x_ref[...]
    xb = _bf(x)
    mixed = None
    for idx, (br_ref, w_ref) in enumerate(((oa_ref, wa_ref), (ob_ref, wb_ref), (oc_ref, wc_ref))):
        gate = _sigmoid(jnp.dot(xb, wm_ref[:, idx * d:(idx + 1) * d], preferred_element_type=F32))
        term = gate * _mm(br_ref[...], w_ref[...])
        mixed = term if mixed is None else mixed + term
    sub = _mm(mixed, wo_ref[...])
    ple = _sigmoid(jnp.dot(xb, wpg_ref[...], preferred_element_type=F32)) * _mm(p_ref[...], wp_ref[...])
    y = alpha * x + sub + ple
    mu = jnp.mean(y, axis=-1, keepdims=True)
    yc = y - mu
    var = jnp.mean(yc * yc, axis=-1, keepdims=True)
    o_ref[...] = yc * lax.rsqrt(var + 1e-5) * lg_ref[...] + lb_ref[...]


def _merge(x2d, p_all, layer, oa, ob, oc, wm, wa, wb, wc, wo, wpg, wp, lg, lb, alpha, tm=512):
    m, d = x2d.shape
    tile = lambda wd: pl.BlockSpec((tm, wd), lambda i: (i, 0))
    full = lambda a: pl.BlockSpec(a.shape, lambda i: (0, 0), pipeline_mode=pl.Buffered(1))
    weights = (wm, wa, wb, wc, wo, wpg, wp, lg, lb)
    return pl.pallas_call(
        functools.partial(_merge_kernel, alpha=alpha),
        out_shape=jax.ShapeDtypeStruct((m, d), F32),
        grid=(m // tm,),
        in_specs=[tile(d), pl.BlockSpec((None, tm, p_all.shape[2]), lambda i: (layer, i, 0)),
                  tile(oa.shape[1]), tile(ob.shape[1]), tile(oc.shape[1])]
        + [full(a) for a in weights],
        out_specs=tile(d),
        compiler_params=_params("parallel"),
        name="merge_out_norm",
    )(x2d, p_all, oa, ob, oc, *weights)


def _nsa_out_perm():
    idx = np.arange(NSA_QW).reshape(NSA_GROUPS, NSA_HPG, NSA_DK)
    return idx.transpose(1, 0, 2).reshape(-1)


def _layer_weights(w_in, conv_w, a_log, dt_bias, norm_w, pe_k, pe_v, w1k, w2k, w1v, w2v, w_br_c, d_model):
    sizes = (RET_W, RET_W, RET_W, RET_W, 3 * GDN_W, GDN_HEADS, GDN_HEADS, GDN_W,
             NSA_QW, NSA_KW, NSA_KW, NSA_KW, NSA_KW, NSA_KW, NSA_KW, 3 * NSA_HEADS, NSA_QW, 3 * d_model)
    offs = np.concatenate([[0], np.cumsum(sizes)])
    col = lambda i: w_in[:, offs[i]:offs[i + 1]]
    perm = np.concatenate([hd * RET_D + np.concatenate([np.arange(0, RET_D, 2), np.arange(1, RET_D, 2)])
                           for hd in range(RET_HEADS)])
    w_ret = _bf(jnp.concatenate([col(0)[:, perm], col(1)[:, perm], col(2), col(3)], axis=1))
    small = jnp.concatenate([col(5), col(6)], axis=1)
    w_gdn = _bf(jnp.concatenate([col(4), col(7), jnp.pad(small, ((0, 0), (0, LANES - small.shape[1])))], axis=1))
    operm = _nsa_out_perm()
    w_nsa = _bf(jnp.concatenate([col(16)[:, operm], col(9), col(10), col(11), col(13)], axis=1))
    gates_t = jnp.pad(col(15).T, ((0, NSA_GATE_ROWS - 3 * NSA_HEADS), (0, 0)))
    w_nsa_t = _bf(jnp.concatenate([col(8).T, col(12).T, col(14).T, gates_t], axis=0))
    w_merge = _bf(col(17))
    lane_ids = jnp.arange(LANES)
    pick = lambda vec: jnp.where((lane_ids >= GDN_HEADS) & (lane_ids < 2 * GDN_HEADS),
                                 jnp.pad(vec, (GDN_HEADS, LANES - 2 * GDN_HEADS)), 0.0)[None, :]
    alog_row = pick(a_log.astype(F32))
    dtb_row = pick(dt_bias.astype(F32))
    alog_lanes = jnp.repeat(a_log.astype(F32), GDN_CHUNK)[None, :]
    dtb_lanes = jnp.repeat(dt_bias.astype(F32), GDN_CHUNK)[None, :]
    half = CMP_LEN // 2

    def big_w1(w1):
        w = w1.reshape(2, half, 1, NSA_DK, 1, CMP_HIDDEN)
        eye = jnp.eye(NSA_GROUPS, dtype=F32).reshape(1, 1, NSA_GROUPS, 1, NSA_GROUPS, 1)
        return _bf((w * eye).reshape(2, half * NSA_GROUPS * NSA_DK, NSA_GROUPS * CMP_HIDDEN))

    def pe_rows(pe):
        t = jnp.broadcast_to(pe.reshape(2, half, 1, NSA_DK), (2, half, NSA_GROUPS, NSA_DK))
        return t.reshape(2, half * NSA_GROUPS * NSA_DK).astype(F32)

    return dict(w_ret=w_ret, w_gdn=w_gdn, w_nsa=w_nsa, w_nsa_t=w_nsa_t, w_merge=w_merge,
                alog_row=alog_row, dtb_row=dtb_row, alog_lanes=alog_lanes, dtb_lanes=dtb_lanes,
                norm_w=norm_w.astype(F32)[None, :], conv_w=conv_w.astype(F32),
                pek=pe_rows(pe_k), pev=pe_rows(pe_v), w1k=big_w1(w1k), w1v=big_w1(w1v), w2k=_bf(w2k),
                w2vt=_bf(w2v.T), w_br_c=_bf(w_br_c[operm, :]))


def _layer(x, p_all, layer, lw, w_br_a, w_br_b, w_out, ln_g, ln_b, w_ple, w_ple_gate, ret_tabs, nsa_ovl, alpha):
    b, s, d = x.shape
    m = b * s
    x2d = x.reshape(m, d)
    r3 = lambda a: a.reshape(b, s, a.shape[-1])

    o_a = _retention(x, lw["w_ret"], ret_tabs)

    qkv, gz, sm, smt = _proj_gdn(x2d, lw["w_gdn"])
    nchunk = s // GDN_CHUNK
    smt = smt.reshape(2, GDN_HEADS, b, nchunk, GDN_CHUNK).transpose(2, 3, 0, 1, 4)
    smt = jnp.pad(smt.reshape(b * nchunk, 2, GDN_HEADS * GDN_CHUNK), ((0, 0), (0, 6), (0, 0)))
    o_b = _gdn(qkv, gz, sm, smt, lw["conv_w"], lw["alog_row"], lw["dtb_row"], lw["alog_lanes"],
               lw["dtb_lanes"], lw["norm_w"], s)

    nz, kc, vc, ks, kw, xt = _proj_nsa(x2d, lw["w_nsa"], lw["w_nsa_t"])
    slab = lambda a: a.reshape(b, s // CMP_STRIDE, CMP_STRIDE * NSA_KW)
    kcb, vcbt = _compress(slab(kc), slab(vc), lw["pek"], lw["pev"], lw["w1k"], lw["w1v"], lw["w2k"], lw["w2vt"])
    o_c = _nsa_attention(xt, r3(nz), kcb, vcbt, r3(ks), r3(kw), nsa_ovl, b, s)

    out = _merge(x2d, p_all, layer, o_a.reshape(m, -1), o_b.reshape(m, -1), o_c.reshape(m, -1),
                 lw["w_merge"], _bf(w_br_a), _bf(w_br_b), lw["w_br_c"], _bf(w_out), _bf(w_ple_gate), _bf(w_ple),
                 ln_g.astype(F32)[None, :], ln_b.astype(F32)[None, :], alpha)
    return out.reshape(b, s, d)


def kernel(x, p, w_in, gdn_conv_w, gdn_a_log, gdn_dt_bias, gdn_norm_w, nsa_pe_k, nsa_pe_v, nsa_cmp_w1k, nsa_cmp_w2k, nsa_cmp_w1v, nsa_cmp_w2v, w_branch_a, w_branch_b, w_branch_c, w_out, ln_g, ln_b, w_ple, w_ple_gate):
    depth = w_in.shape[0]
    s, d = x.shape[1], x.shape[2]
    alpha = (2 * depth) ** 0.25
    ret_tabs = _ret_tables(s)
    nsa_ovl = _nsa_tables(s)
    p_all = p.reshape(depth, -1, p.shape[-1])
    for i in range(depth):
        lw = _layer_weights(w_in[i], gdn_conv_w[i], gdn_a_log[i], gdn_dt_bias[i], gdn_norm_w[i], nsa_pe_k[i],
                            nsa_pe_v[i], nsa_cmp_w1k[i], nsa_cmp_w2k[i], nsa_cmp_w1v[i], nsa_cmp_w2v[i],
                            w_branch_c[i], d)
        x = _layer(x, p_all, i, lw, w_branch_a[i], w_branch_b[i], w_out[i], ln_g[i], ln_b[i],
                   w_ple[i], w_ple_gate[i], ret_tabs, nsa_ovl, alpha)
    return x
```

```python
import functools

import numpy as np
import jax
import jax.numpy as jnp
from jax import lax
from jax.experimental import pallas as pl
from jax.experimental.pallas import tpu as pltpu

F32 = jnp.float32
BF16 = jnp.bfloat16

RET_HEADS, RET_D, RET_CHUNK = 4, 128, 128
ROPE_BASE = 10000.0
GDN_HEADS, GDN_D, GDN_CHUNK, CONV_K = 4, 128, 64, 4
NSA_HEADS, NSA_GROUPS, NSA_DK = 8, 2, 64
NSA_HPG = NSA_HEADS // NSA_GROUPS
CMP_LEN, CMP_STRIDE, CMP_HIDDEN = 32, 16, 128
SLC_BLOCK, N_SEL, WIN = 64, 8, 256
NEG, BIG, EPS = -1e30, 1e30, 1e-6
LANES = 128
BF16_ROWS = 16
ROW_TILE = 512
SEQ_TILE = 512
Q_TILE = 256
GDN_UNROLL = 4
VMEM_LIMIT = 56 * 1024 * 1024

RET_W = RET_HEADS * RET_D
GDN_W = GDN_HEADS * GDN_D
NSA_QW = NSA_HEADS * NSA_DK
NSA_KW = NSA_GROUPS * NSA_DK
NSA_GATE_ROWS = 32
NSA_T_ROWS = NSA_QW + 2 * NSA_KW + NSA_GATE_ROWS


def _bf(x):
    return x.astype(BF16)


def _mm(a, b):
    return jnp.dot(_bf(a), _bf(b), preferred_element_type=F32)


def _mm_nt(a, b):
    return lax.dot_general(_bf(a), _bf(b), (((1,), (1,)), ((), ())), preferred_element_type=F32)


def _split3(x):
    hi = _bf(x)
    r1 = x - hi.astype(F32)
    mid = _bf(r1)
    lo = _bf(r1 - mid.astype(F32))
    return hi, mid, lo


def _sigmoid(x):
    return 0.5 * jnp.tanh(0.5 * x) + 0.5


def _silu(x):
    return x * _sigmoid(x)


def _softplus(x):
    return jnp.maximum(x, 0.0) + jnp.log(1.0 + jnp.exp(-jnp.abs(x)))


def _params(*sem):
    return pltpu.CompilerParams(dimension_semantics=sem, vmem_limit_bytes=VMEM_LIMIT)


def _proj_nsa_kernel(x_ref, w_ref, wt_ref, z_ref, kcs_ref, vcs_ref, ks_ref, kw_ref, xt_ref, kc_s, vc_s):
    x = _bf(x_ref[...])
    tm = x_ref.shape[0]
    nz, nk = z_ref.shape[1], ks_ref.shape[1]
    z_ref[...] = jnp.dot(x, w_ref[:, :nz], preferred_element_type=F32)
    for idx, (dst, tmp) in enumerate(((kcs_ref, kc_s), (vcs_ref, vc_s))):
        tmp[...] = jnp.dot(x, w_ref[:, nz + idx * nk:nz + (idx + 1) * nk], preferred_element_type=F32)
        dst[...] = jnp.concatenate([tmp[pl.ds(tok, tm // CMP_STRIDE, stride=CMP_STRIDE), :]
                                    for tok in range(CMP_STRIDE)], axis=1)
    ks_ref[...] = jnp.dot(x, w_ref[:, nz + 2 * nk:nz + 3 * nk], preferred_element_type=F32)
    kw_ref[...] = jnp.dot(x, w_ref[:, nz + 3 * nk:nz + 4 * nk], preferred_element_type=F32)
    xt_ref[...] = lax.dot_general(wt_ref[...], x, (((1,), (1,)), ((), ())), preferred_element_type=F32)


def _proj_nsa(x2d, w, wt, tm=ROW_TILE):
    m, k = x2d.shape
    n = w.shape[1]
    nt = wt.shape[0]
    slab_w = CMP_STRIDE * NSA_KW
    assert n == NSA_QW + 4 * NSA_KW and m % tm == 0 and tm % (8 * CMP_STRIDE) == 0
    row = lambda wd: pl.BlockSpec((tm, wd), lambda i: (i, 0))
    slab = lambda: pl.BlockSpec((tm // CMP_STRIDE, slab_w), lambda i: (i, 0))
    return pl.pallas_call(
        _proj_nsa_kernel,
        out_shape=[jax.ShapeDtypeStruct((m, NSA_QW), F32), jax.ShapeDtypeStruct((m // CMP_STRIDE, slab_w), F32),
                   jax.ShapeDtypeStruct((m // CMP_STRIDE, slab_w), F32), jax.ShapeDtypeStruct((m, NSA_KW), F32),
                   jax.ShapeDtypeStruct((m, NSA_KW), F32), jax.ShapeDtypeStruct((nt, m), F32)],
        grid=(m // tm,),
        in_specs=[pl.BlockSpec((tm, k), lambda i: (i, 0)), pl.BlockSpec((k, n), lambda i: (0, 0)),
                  pl.BlockSpec((nt, k), lambda i: (0, 0))],
        out_specs=[row(NSA_QW), slab(), slab(), row(NSA_KW), row(NSA_KW), pl.BlockSpec((nt, tm), lambda i: (0, i))],
        scratch_shapes=[pltpu.VMEM((tm, NSA_KW), F32), pltpu.VMEM((tm, NSA_KW), F32)],
        compiler_params=_params("parallel"),
        name="proj_nsa",
    )(x2d, w, wt)


def _proj_gdn_kernel(x_ref, w_ref, qkv_ref, z_ref, sm_ref, smt_ref):
    x = _bf(x_ref[...])
    nq, nz = qkv_ref.shape[1], z_ref.shape[1]
    qkv_ref[...] = jnp.dot(x, w_ref[:, :nq], preferred_element_type=F32)
    z_ref[...] = jnp.dot(x, w_ref[:, nq:nq + nz], preferred_element_type=F32)
    sm = jnp.dot(x, w_ref[:, nq + nz:], preferred_element_type=F32)
    sm_ref[...] = sm
    smt_ref[...] = sm.T[0:8, :]


def _proj_gdn(x2d, w, tm=ROW_TILE):
    m, k = x2d.shape
    n = w.shape[1]
    widths = (3 * GDN_W, GDN_W, LANES)
    assert sum(widths) == n and m % tm == 0
    return pl.pallas_call(
        _proj_gdn_kernel,
        out_shape=[jax.ShapeDtypeStruct((m, wd), F32) for wd in widths] + [jax.ShapeDtypeStruct((8, m), F32)],
        grid=(m // tm,),
        in_specs=[pl.BlockSpec((tm, k), lambda i: (i, 0)), pl.BlockSpec((k, n), lambda i: (0, 0))],
        out_specs=[pl.BlockSpec((tm, wd), lambda i: (i, 0)) for wd in widths] + [pl.BlockSpec((8, tm), lambda i: (0, i))],
        compiler_params=_params("parallel"),
        name="proj_gdn",
    )(x2d, w)


def _ret_kernel(x_ref, w_ref, cos_ref, sin_ref, dmask_ref, qdec_ref, kdec_ref, cdec_ref, o_ref, state_s):
    ts = x_ref.shape[0]
    c = RET_CHUNK
    half = RET_D // 2
    scale = RET_D ** -0.5

    @pl.when(pl.program_id(1) == 0)
    def _():
        state_s[...] = jnp.zeros_like(state_s)

    def project(n):
        return jnp.dot(_bf(x_ref[n * c:(n + 1) * c, :]), w_ref[...], preferred_element_type=F32)

    def chunk(n, hx):
        sl = slice(n * c, (n + 1) * c)
        cos = cos_ref[sl, :]
        sin = sin_ref[sl, :]
        for h in range(RET_HEADS):
            hs = slice(h * RET_D, (h + 1) * RET_D)
            q = hx[:, h * RET_D:(h + 1) * RET_D]
            k = hx[:, RET_W + h * RET_D:RET_W + (h + 1) * RET_D]
            v = hx[:, 2 * RET_W + h * RET_D:2 * RET_W + (h + 1) * RET_D]
            state = state_s[h]
            q = q * cos + pltpu.roll(q, half, 1) * sin
            k = (k * cos + pltpu.roll(k, half, 1) * sin) * scale
            scores = _mm_nt(q, k) * dmask_ref[h]
            o = _mm(scores, v) + _mm(q * qdec_ref[h], state)
            state_s[h] = cdec_ref[h] * state + _mm((k * kdec_ref[h]).T, v)
            mu = jnp.mean(o, axis=-1, keepdims=True)
            d = o - mu
            var = jnp.mean(d * d, axis=-1, keepdims=True)
            o_ref[sl, hs] = d * lax.rsqrt(var + 1e-5) * _silu(hx[:, 3 * RET_W + h * RET_D:3 * RET_W + (h + 1) * RET_D])

    hx = project(0)
    for n in range(ts // c):
        hx_next = project(n + 1) if n + 1 < ts // c else None
        chunk(n, hx)
        hx = hx_next


def _retention(x, w, tabs, ts=SEQ_TILE):
    b, s, dm = x.shape
    cos, sin, dmask, qdec, kdec, cdec = tabs
    seq = lambda: pl.BlockSpec((None, ts, RET_W), lambda i, t: (i, t, 0))
    tab = lambda: pl.BlockSpec((RET_HEADS, RET_CHUNK, RET_D), lambda i, t: (0, 0, 0))
    rot = lambda: pl.BlockSpec((ts, RET_D), lambda i, t: (t, 0))
    return pl.pallas_call(
        _ret_kernel,
        out_shape=jax.ShapeDtypeStruct((b, s, RET_W), F32),
        grid=(b, s // ts),
        in_specs=[pl.BlockSpec((None, ts, dm), lambda i, t: (i, t, 0)), pl.BlockSpec(w.shape, lambda i, t: (0, 0)),
                  rot(), rot(), tab(), tab(), tab(), tab()],
        out_specs=seq(),
        scratch_shapes=[pltpu.VMEM((RET_HEADS, RET_D, RET_D), F32)],
        compiler_params=_params("parallel", "arbitrary"),
        name="retention",
    )(x, w, cos, sin, dmask, qdec, kdec, cdec)


def _ret_tables(s):
    inv = ROPE_BASE ** (-jnp.arange(0, RET_D, 2, dtype=F32) / RET_D)
    ang = jnp.arange(s, dtype=F32)[:, None] * inv[None, :]
    cos = jnp.concatenate([jnp.cos(ang), jnp.cos(ang)], axis=-1)
    sin = jnp.concatenate([-jnp.sin(ang), jnp.sin(ang)], axis=-1)
    c = RET_CHUNK
    log_g = jnp.log1p(-jnp.exp2(-5.0 - jnp.arange(RET_HEADS, dtype=F32)))
    pos = jnp.arange(c, dtype=F32)
    diff = pos[:, None] - pos[None, :]
    causal = diff >= 0
    dmask = jnp.where(causal[None], jnp.exp(jnp.where(causal, diff, 0.0)[None] * log_g[:, None, None]), 0.0)
    ones = jnp.ones((1, 1, RET_D), F32)
    kdec = jnp.exp((c - 1 - pos)[None, :, None] * log_g[:, None, None]) * ones
    qdec = jnp.exp((pos + 1.0)[None, :, None] * log_g[:, None, None]) * ones
    cdec = jnp.exp(c * log_g)[:, None, None] * jnp.ones((1, c, RET_D), F32)
    return cos, sin, dmask, qdec, kdec, cdec


def _gdn_kernel(qkv_ref, z_ref, sm_ref, smt_ref, cw_ref, alog_ref, dtb_ref, alog_l_ref, dtb_l_ref, nw_ref, o_ref,
                state_s, xs_s, h_qe, h_kd, h_mp, h_r, h_at, h_dec,
                lhs_s, ou_s, psi_s, dec_s, *, nt):
    ts = qkv_ref.shape[0]
    c, d, nh = GDN_CHUNK, GDN_D, GDN_HEADS
    hc = nh * c
    nct = ts // c
    g = pl.program_id(0)
    ri = lax.broadcasted_iota(jnp.int32, (hc, hc), 0)
    ci = lax.broadcasted_iota(jnp.int32, (hc, hc), 1)
    same = (ri // c) == (ci // c)
    incl = same & (ri >= ci)
    strict = same & (ri > ci)
    upper = same & (ri <= ci)
    neg_a = -jnp.exp(alog_ref[...])
    dtb = dtb_ref[...]
    neg_a_l = -jnp.exp(alog_l_ref[...])
    dtb_l = dtb_l_ref[...]
    nw = nw_ref[...]

    @pl.when(g == 0)
    def _():
        for ref in (state_s, h_qe, h_kd, h_mp, h_r, h_at, h_dec):
            ref[...] = jnp.zeros_like(ref)

    @pl.when(g % nt == 0)
    def _():
        xs_s[0:8, :] = jnp.zeros((8, xs_s.shape[1]), F32)

    xs_s[8:8 + ts, :] = qkv_ref[...]
    wr = g % 2
    rd = 1 - wr

    def conv_silu_stack(n, part):
        start = pl.multiple_of(n * c, c)
        cols = slice(part * nh * d, (part + 1) * nh * d)
        w = cw_ref[:, cols]
        ext = xs_s[pl.ds(start, c + 8), cols]
        y = ext[8:8 + c, :] * w[CONV_K - 1:CONV_K, :]
        for j in range(1, CONV_K):
            y = y + ext[8 - j:8 - j + c, :] * w[CONV_K - 1 - j:CONV_K - j, :]
        y = _silu(y)
        return [y[:, h * d:(h + 1) * d] for h in range(nh)]

    def l2n(xs, mult):
        out = [x * (lax.rsqrt(jnp.sum(x * x, axis=-1, keepdims=True) + EPS) * mult) for x in xs]
        return jnp.concatenate(out, axis=0)

    def prep(it, carry):
        chunks = [it * GDN_UNROLL + u for u in range(GDN_UNROLL)]
        prev = [dict(qe=h_qe[rd, n], kd=h_kd[rd, n], mp=h_mp[rd, n], r=h_r[rd, n], at=h_at[rd, n], dec=h_dec[rd, n])
                for n in chunks]
        st = []
        for n in chunks:
            sl = pl.ds(pl.multiple_of(n * c, c), c)
            q = l2n(conv_silu_stack(n, 0), d ** -0.5)
            k = l2n(conv_silu_stack(n, 1), 1.0)
            v = jnp.concatenate(conv_silu_stack(n, 2), axis=0)
            sm = sm_ref[sl, :]
            beta_col = jnp.concatenate([_sigmoid(sm[:, h:h + 1]) for h in range(nh)], axis=0)
            g_col = jnp.concatenate(
                [neg_a[:, nh + h:nh + h + 1] * _softplus(sm[:, nh + h:nh + h + 1] + dtb[:, nh + h:nh + h + 1])
                 for h in range(nh)], axis=0)
            smt = smt_ref[n]
            g_row = neg_a_l * _softplus(smt[1:2, :] + dtb_l)
            g_row_b = jnp.broadcast_to(g_row, (hc, hc))
            gc_col = jnp.sum(jnp.where(incl, g_row_b, 0.0), axis=1, keepdims=True)
            gc_row = jnp.sum(jnp.where(upper, jnp.broadcast_to(g_col, (hc, hc)), 0.0), axis=0, keepdims=True)
            g_last = jnp.sum(jnp.where(same, g_row_b, 0.0), axis=1, keepdims=True)
            decay = jnp.where(incl, jnp.exp(jnp.where(incl, gc_col - gc_row, 0.0)), 0.0)
            egc = jnp.exp(gc_col)
            st.append(dict(q=q, k=k, v=v, beta=beta_col, gc_col=gc_col, g_last=g_last, decay=decay, egc=egc))
        for s_ in st:
            qk_kk = _mm_nt(jnp.concatenate([s_["q"], s_["k"]], axis=0), s_["k"])
            s_["attn"] = qk_kk[:hc] * s_["decay"]
            s_["mp"] = _bf(-jnp.where(strict, s_["beta"] * qk_kk[hc:] * s_["decay"], 0.0))
            s_["r"] = jnp.concatenate([s_["v"] * s_["beta"], s_["k"] * (s_["beta"] * s_["egc"])], axis=1)
        for p_ in prev:
            p_["r"] = p_["r"] + jnp.dot(p_["mp"], _bf(p_["r"]), preferred_element_type=F32)
        for _ in range(5):
            for p_ in prev:
                p_["mp"] = _bf(jnp.dot(p_["mp"], p_["mp"], preferred_element_type=F32))
            for p_ in prev:
                p_["r"] = p_["r"] + jnp.dot(p_["mp"], _bf(p_["r"]), preferred_element_type=F32)
        for n, p_ in zip(chunks, prev):
            rb = _bf(p_["r"])
            ar = jnp.dot(p_["at"], rb, preferred_element_type=F32)
            qt = p_["qe"] - ar[:, d:]
            for h in range(nh):
                rows = slice(h * c, (h + 1) * c)
                kr = jnp.dot(_bf(p_["kd"][rows].T), rb[rows], preferred_element_type=F32)
                lhs_s[h, n, 0:c, :] = _bf(qt[rows])
                lhs_s[h, n, c:c + d, :] = _bf(-kr[:, d:])
                ou_s[h, n] = ar[rows, :d]
                psi_s[h, n] = kr[:, :d]
                dec_s[h, n] = p_["dec"][h]
        for n, s_ in zip(chunks, st):
            h_qe[wr, n] = s_["q"] * s_["egc"]
            h_kd[wr, n] = s_["k"] * jnp.exp(s_["g_last"] - s_["gc_col"])
            h_mp[wr, n] = s_["mp"]
            h_r[wr, n] = s_["r"]
            h_at[wr, n] = _bf(s_["attn"])
            for h in range(nh):
                h_dec[wr, n, h] = jnp.broadcast_to(jnp.exp(s_["g_last"][h * c:h * c + 1, :]), (8, d))
        return carry

    lax.fori_loop(0, nct // GDN_UNROLL, prep, 0)

    @pl.when((g + nt - 1) % nt == 0)
    def _():
        state_s[...] = jnp.zeros_like(state_s)

    def scan(n, carry):
        sl = pl.ds(pl.multiple_of(n * c, c), c)
        for h in range(nh):
            hs = slice(h * d, (h + 1) * d)
            state = state_s[h]
            res = jnp.dot(lhs_s[h, n], _bf(state), preferred_element_type=F32)
            o = res[:c] + ou_s[h, n]
            state_s[h] = dec_s[h, n][0:1, :] * state + res[c:] + psi_s[h, n]
            o = o * lax.rsqrt(jnp.mean(o * o, axis=-1, keepdims=True) + EPS) * nw
            o_ref[sl, hs] = o * _silu(z_ref[sl, hs])
        return carry

    lax.fori_loop(0, nct, scan, 0, unroll=True)
    xs_s[0:8, :] = qkv_ref[ts - 8:ts, :]


def _gdn(qkv, z, sm, smt, conv_w, alog_row, dtb_row, alog_lanes, dtb_lanes, norm_w, s, ts=SEQ_TILE):
    m, wq = qkv.shape
    nh, c, d = GDN_HEADS, GDN_CHUNK, GDN_D
    hc = nh * c
    nct = ts // c
    nt = s // ts
    ntile = m // ts
    assert nct % GDN_UNROLL == 0 and s % ts == 0
    front = lambda wd: pl.BlockSpec((ts, wd), lambda g: (jnp.minimum(g, ntile - 1), 0))
    back = lambda wd: pl.BlockSpec((ts, wd), lambda g: (jnp.maximum(g - 1, 0), 0))
    full = lambda a: pl.BlockSpec(a.shape, lambda g, n=a.ndim: (0,) * n)
    return pl.pallas_call(
        functools.partial(_gdn_kernel, nt=nt),
        out_shape=jax.ShapeDtypeStruct((m, GDN_W), F32),
        grid=(ntile + 1,),
        in_specs=[front(wq), back(GDN_W), front(LANES),
                  pl.BlockSpec((nct, 8, hc), lambda g: (jnp.minimum(g, ntile - 1), 0, 0)),
                  full(conv_w), full(alog_row), full(dtb_row), full(alog_lanes), full(dtb_lanes), full(norm_w)],
        out_specs=back(GDN_W),
        scratch_shapes=[pltpu.VMEM((nh, d, d), F32),
                        pltpu.VMEM((ts + 8, wq), F32),
                        pltpu.VMEM((2, nct, hc, d), F32),
                        pltpu.VMEM((2, nct, hc, d), F32),
                        pltpu.VMEM((2, nct, hc, hc), BF16),
                        pltpu.VMEM((2, nct, hc, 2 * d), F32),
                        pltpu.VMEM((2, nct, hc, hc), BF16),
                        pltpu.VMEM((2, nct, nh, 8, d), F32),
                        pltpu.VMEM((nh, nct, c + d, d), BF16),
                        pltpu.VMEM((nh, nct, c, d), F32),
                        pltpu.VMEM((nh, nct, d, d), F32),
                        pltpu.VMEM((nh, nct, 8, d), F32)],
        compiler_params=_params("arbitrary"),
        name="gated_delta_net",
    )(qkv, z, sm, smt, conv_w, alog_row, dtb_row, alog_lanes, dtb_lanes, norm_w)


def _cmp_kernel(yk_ref, yv_ref, pek_ref, pev_ref, w1k_ref, w1v_ref, w2k_ref, w2vt_ref, kc_ref, vct_ref):
    nrow = yk_ref.shape[0]

    def hidden(y_ref, pe_ref, w1_ref):
        y = y_ref[...]
        first = _mm(y + pe_ref[0:1, :], w1_ref[0])
        second = _mm(y + pe_ref[1:2, :], w1_ref[1])
        pre = first + pltpu.roll(second, nrow - 1, 0)
        return jax.nn.gelu(pre)

    hk = hidden(yk_ref, pek_ref, w1k_ref)
    kc_ref[...] = jnp.concatenate(
        [_mm(hk[:, g * CMP_HIDDEN:(g + 1) * CMP_HIDDEN], w2k_ref[...]) for g in range(NSA_GROUPS)], axis=1)
    hv = hidden(yv_ref, pev_ref, w1v_ref)
    vct_ref[...] = jnp.concatenate(
        [_mm_nt(w2vt_ref[...], hv[:, g * CMP_HIDDEN:(g + 1) * CMP_HIDDEN]) for g in range(NSA_GROUPS)], axis=0)


def _compress(yk, yv, pek, pev, w1k, w1v, w2k, w2vt):
    b, nrow, width = yk.shape
    hid = NSA_GROUPS * CMP_HIDDEN
    full = lambda shape: pl.BlockSpec(shape, lambda i, n=len(shape): (0,) * n)
    return pl.pallas_call(
        _cmp_kernel,
        out_shape=[jax.ShapeDtypeStruct((b, nrow, NSA_KW), F32), jax.ShapeDtypeStruct((b, NSA_KW, nrow), F32)],
        grid=(b,),
        in_specs=[pl.BlockSpec((None, nrow, width), lambda i: (i, 0, 0)),
                  pl.BlockSpec((None, nrow, width), lambda i: (i, 0, 0)),
                  full((2, width)), full((2, width)), full((2, width, hid)), full((2, width, hid)),
                  full((CMP_HIDDEN, NSA_DK)), full((NSA_DK, CMP_HIDDEN))],
        out_specs=[pl.BlockSpec((None, nrow, NSA_KW), lambda i: (i, 0, 0)),
                   pl.BlockSpec((None, NSA_KW, nrow), lambda i: (i, 0, 0))],
        compiler_params=_params("parallel"),
        name="nsa_compress",
    )(yk, yv, pek, pev, w1k, w1v, w2k, w2vt)


LOG2E = 1.4426950408889634


def _exp2_cols(s):
    return jnp.exp2(s - jnp.max(s, axis=0, keepdims=True))


def _with_ones(vt):
    return jnp.concatenate([vt, jnp.ones((BF16_ROWS, vt.shape[1]), BF16)], axis=0)


def _nsa_kernel(xt_ref, z_ref, kc_ref, vct_ref, ks_ref, vst_ref, kw_ref, vwt_ref, ovl_ref, o_ref, sel_s,
                s_even, s_odd, p_even, p_odd, acc_s):
    tq = z_ref.shape[0]
    s_len = ks_ref.shape[0]
    ncmp = kc_ref.shape[0]
    nslc = s_len // SLC_BLOCK
    qt = pl.program_id(1)
    t0 = qt * tq
    scale = NSA_DK ** -0.5 * LOG2E
    ng, hp, nh = NSA_GROUPS, NSA_HPG, NSA_HEADS
    cols = nh * tq
    t_row = t0 + lax.broadcasted_iota(jnp.int32, (1, tq), 1)
    sub = lax.broadcasted_iota(jnp.int32, (LANES, tq), 0)
    low = sub < NSA_DK
    gates = _sigmoid(xt_ref[NSA_QW + 2 * NSA_KW:NSA_QW + 2 * NSA_KW + NSA_GATE_ROWS, :])
    tile = lambda a, n: jnp.concatenate([a] * n, axis=1)
    zeros_half = jnp.zeros((NSA_DK, tq), F32)
    dk = NSA_DK

    def pv_groups(prep, vt, p):
        return jnp.concatenate(
            [jnp.dot(prep(vt[g * dk:(g + 1) * dk]), p[:, g * hp * tq:(g + 1) * hp * tq], preferred_element_type=F32)
             for g in range(ng)], axis=1)

    qs = []
    for head in range(nh):
        qh = xt_ref[head * NSA_DK:(head + 1) * NSA_DK, :] * scale
        qs.append(jnp.concatenate([qh, zeros_half] if head // hp == 0 else [zeros_half, qh], axis=0))
    q8 = _bf(jnp.concatenate(qs, axis=1))

    nwin = WIN + tq
    w0 = pl.multiple_of(jnp.maximum(t0 - WIN, 0), LANES)
    kpos = w0 + lax.broadcasted_iota(jnp.int32, (nwin, 1), 0)
    dq = t_row - kpos
    wmask8 = tile(dq.astype(jnp.uint32) < WIN, nh)
    s_w = jnp.dot(_bf(kw_ref[pl.ds(w0, nwin), :]), q8, preferred_element_type=F32)
    e_w = _exp2_cols(jnp.where(wmask8, s_w, NEG))
    o_win = pv_groups(_with_ones, _bf(vwt_ref[:, pl.ds(w0, nwin)]), _bf(e_w))
    o_win = o_win[:dk] * (1.0 / o_win[dk:dk + 1])

    c_end =lax.broadcasted_iota(jnp.int32, (ncmp, 1), 0) * CMP_STRIDE + (CMP_LEN - 1)
    cmask8 = tile(c_end <= t_row, nh)
    s_c = jnp.dot(_bf(kc_ref[...]), q8, preferred_element_type=F32)
    e_c = _exp2_cols(jnp.where(cmask8, s_c, NEG))
    p_c = jnp.where(cmask8, e_c * (1.0 / jnp.sum(e_c, axis=0, keepdims=True)), 0.0)
    o_cmp = pv_groups(lambda v: v, _bf(vct_ref[...]), _bf(p_c))

    ovl = ovl_ref[...]
    jb = lax.broadcasted_iota(jnp.int32, (nslc, tq), 0)
    cur = t_row // SLC_BLOCK
    forced = (jb == 0) | (jb == cur) | (jb == cur - 1)
    for g in range(ng):
        p_sum = p_c[:, g * hp * tq:(g * hp + 1) * tq]
        for hh in range(1, hp):
            p_sum = p_sum + p_c[:, (g * hp + hh) * tq:(g * hp + hh + 1) * tq]
        imp = sum(jnp.dot(ovl, part, preferred_element_type=F32) for part in _split3(p_sum))
        rank = jnp.where(jb <= cur, jnp.where(forced, BIG, imp), -BIG)
        cnt = jnp.zeros((nslc, tq), F32)
        for i in range(nslc):
            ri = rank[i:i + 1, :]
            tie = jnp.where(jb > i, 1.0, 0.0)
            cnt = cnt + jnp.where(ri > rank, 1.0, jnp.where(ri == rank, tie, 0.0))
        sel = jnp.where(cnt < float(min(N_SEL, nslc)), 0.0, NEG)
        for kt in range(s_len // LANES):
            sel_s[g, kt, 0:2, :] = sel[2 * kt:2 * kt + 2, :]

    def qk(kt):
        return jnp.dot(_bf(ks_ref[pl.ds(pl.multiple_of(kt * LANES, LANES), LANES), :]), q8,
                       preferred_element_type=F32)

    def pv(kt, p):
        return pv_groups(_with_ones, _bf(vst_ref[:, pl.ds(pl.multiple_of(kt * LANES, LANES), LANES)]), p)

    nkt = s_len // LANES

    def one_tile(kt, carry, s_rd, s_wr, p_buf):
        m_run, a_prev = carry
        s_wr[...] = qk(jnp.minimum(kt + 1, nkt - 1))
        pv_old = pv(jnp.maximum(kt - 2, 0), p_buf[...])
        causal = (kt * LANES + sub) <= t_row
        biases = []
        for g in range(ng):
            pair = sel_s[g, kt, 0:2, :]
            biases.append(tile(jnp.where(causal, jnp.where(low, pair[0:1, :], pair[1:2, :]), NEG), hp))
        s_s = s_rd[...] + jnp.concatenate(biases, axis=1)
        m_new = jnp.maximum(m_run, jnp.max(s_s, axis=0, keepdims=True))
        alpha = jnp.exp2(m_run - m_new)
        p_buf[...] = _bf(jnp.exp2(s_s - m_new))
        acc_s[...] = alpha * (acc_s[...] + a_prev * pv_old)
        return m_new, alpha

    def slc_body(j, carry):
        carry = one_tile(2 * j, carry, s_even, s_odd, p_even)
        return one_tile(2 * j + 1, carry, s_odd, s_even, p_odd)

    s_even[...] = qk(0)
    p_even[...] = jnp.zeros((LANES, cols), BF16)
    p_odd[...] = jnp.zeros((LANES, cols), BF16)
    acc_s[...] = jnp.zeros(acc_s.shape, F32)
    npair = ((qt + 1) * tq // LANES + 1) // 2
    _, a_last = lax.fori_loop(0, npair, slc_body, (jnp.full((1, cols), NEG, F32), jnp.ones((1, cols), F32)))
    o_slc = acc_s[...] + a_last * pv(2 * npair - 2, p_even[...]) + pv(2 * npair - 1, p_odd[...])
    o_slc = o_slc[:dk] * (1.0 / o_slc[dk:dk + 1])

    def head_out(head):
        cs = slice(head * tq, (head + 1) * tq)
        row = 3 * head
        return (gates[row:row + 1, :] * o_cmp[:, cs] + gates[row + 1:row + 2, :] * o_slc[:, cs]
                + gates[row + 2:row + 3, :] * o_win[:, cs])

    for hh in range(hp):
        both = jnp.concatenate([head_out(hh), head_out(hp + hh)], axis=0)
        cs = slice(hh * LANES, (hh + 1) * LANES)
        o_ref[:, cs] = both.T * _silu(z_ref[:, cs])


def _nsa_attention(xt, z, kcb, vcbt, ks, kw, ovl, b, s):
    tq = Q_TILE
    ncmp = kcb.shape[1]
    nqt = s // tq
    nslc = s // SLC_BLOCK
    vrow = NSA_QW // NSA_KW
    return pl.pallas_call(
        _nsa_kernel,
        out_shape=jax.ShapeDtypeStruct((b, s, NSA_QW), F32),
        grid=(b, nqt),
        in_specs=[pl.BlockSpec((NSA_T_ROWS, tq), lambda i, t: (0, i * nqt + t)),
                  pl.BlockSpec((None, tq, NSA_QW), lambda i, t: (i, t, 0)),
                  pl.BlockSpec((None, ncmp, NSA_KW), lambda i, t: (i, 0, 0)),
                  pl.BlockSpec((None, NSA_KW, ncmp), lambda i, t: (i, 0, 0)),
                  pl.BlockSpec((None, s, NSA_KW), lambda i, t: (i, 0, 0)),
                  pl.BlockSpec((NSA_KW, s), lambda i, t: (vrow, i)),
                  pl.BlockSpec((None, s, NSA_KW), lambda i, t: (i, 0, 0)),
                  pl.BlockSpec((NSA_KW, s), lambda i, t: (vrow + 1, i)),
                  pl.BlockSpec(ovl.shape, lambda i, t: (0, 0))],
        out_specs=pl.BlockSpec((None, tq, NSA_QW), lambda i, t: (i, t, 0)),
        scratch_shapes=[pltpu.VMEM((NSA_GROUPS, s // LANES, 8, tq), F32),
                        pltpu.VMEM((LANES, NSA_HEADS * tq), F32),
                        pltpu.VMEM((LANES, NSA_HEADS * tq), F32),
                        pltpu.VMEM((LANES, NSA_HEADS * tq), BF16),
                        pltpu.VMEM((LANES, NSA_HEADS * tq), BF16),
                        pltpu.VMEM((NSA_DK + BF16_ROWS, NSA_HEADS * tq), F32)],
        compiler_params=_params("parallel", "arbitrary"),
        name="nsa_attention",
    )(xt, z, kcb, vcbt, ks, xt, kw, xt, ovl)


def _nsa_tables(s):
    ncmp_rows = s // CMP_STRIDE
    nslc = s // SLC_BLOCK
    ci = np.arange(ncmp_rows)[None, :] * CMP_STRIDE
    sj = np.arange(nslc)[:, None] * SLC_BLOCK
    ovl = ((ci < sj + SLC_BLOCK) & (ci + CMP_LEN > sj)).astype(np.float32)
    return jnp.asarray(ovl, BF16)


def _merge_kernel(x_ref, p_ref, oa_ref, ob_ref, oc_ref, wm_ref, wa_ref, wb_ref, wc_ref, wo_ref, wpg_ref, wp_ref,
                  lg_ref, lb_ref, o_ref, *, alpha):
    d = x_ref.shape[1]
    x = x_ref[...]
    xb = _bf(x)
    mixed = None
    for idx, (br_ref, w_ref) in enumerate(((oa_ref, wa_ref), (ob_ref, wb_ref), (oc_ref, wc_ref))):
        gate = _sigmoid(jnp.dot(xb, wm_ref[:, idx * d:(idx + 1) * d], preferred_element_type=F32))
        term = gate * _mm(br_ref[...], w_ref[...])
        mixed = term if mixed is None else mixed + term
    sub = _mm(mixed, wo_ref[...])
    ple = _sigmoid(jnp.dot(xb, wpg_ref[...], preferred_element_type=F32)) * _mm(p_ref[...], wp_ref[...])
    y = alpha * x + sub + ple
    mu = jnp.mean(y, axis=-1, keepdims=True)
    yc = y - mu
    var = jnp.mean(yc * yc, axis=-1, keepdims=True)
    o_ref[...] = yc * lax.rsqrt(var + 1e-5) * lg_ref[...] + lb_ref[...]


def _merge(x2d, p_all, layer, oa, ob, oc, wm, wa, wb, wc, wo, wpg, wp, lg, lb, alpha, tm=ROW_TILE):
    m, d = x2d.shape
    tile = lambda wd: pl.BlockSpec((tm, wd), lambda i: (i, 0))
    full = lambda a: pl.BlockSpec(a.shape, lambda i: (0, 0), pipeline_mode=pl.Buffered(1))
    weights = (wm, wa, wb, wc, wo, wpg, wp, lg, lb)
    return pl.pallas_call(
        functools.partial(_merge_kernel, alpha=alpha),
        out_shape=jax.ShapeDtypeStruct((m, d), F32),
        grid=(m // tm,),
        in_specs=[tile(d), pl.BlockSpec((None, tm, p_all.shape[2]), lambda i: (layer, i, 0)),
                  tile(oa.shape[1]), tile(ob.shape[1]), tile(oc.shape[1])]
        + [full(a) for a in weights],
        out_specs=tile(d),
        compiler_params=_params("parallel"),
        name="merge_out_norm",
    )(x2d, p_all, oa, ob, oc, *weights)


def _nsa_out_perm():
    idx = np.arange(NSA_QW).reshape(NSA_GROUPS, NSA_HPG, NSA_DK)
    return idx.transpose(1, 0, 2).reshape(-1)


def _layer_weights(w_in, conv_w, a_log, dt_bias, norm_w, pe_k, pe_v, w1k, w2k, w1v, w2v, w_br_c, d_model):
    sizes = (RET_W, RET_W, RET_W, RET_W, 3 * GDN_W, GDN_HEADS, GDN_HEADS, GDN_W,
             NSA_QW, NSA_KW, NSA_KW, NSA_KW, NSA_KW, NSA_KW, NSA_KW, 3 * NSA_HEADS, NSA_QW, 3 * d_model)
    offs = np.concatenate([[0], np.cumsum(sizes)])
    col = lambda i: w_in[:, offs[i]:offs[i + 1]]
    perm = np.concatenate([hd * RET_D + np.concatenate([np.arange(0, RET_D, 2), np.arange(1, RET_D, 2)])
                           for hd in range(RET_HEADS)])
    w_ret = _bf(jnp.concatenate([col(0)[:, perm], col(1)[:, perm], col(2), col(3)], axis=1))
    small = jnp.concatenate([col(5), col(6)], axis=1)
    w_gdn = _bf(jnp.concatenate([col(4), col(7), jnp.pad(small, ((0, 0), (0, LANES - small.shape[1])))], axis=1))
    operm = _nsa_out_perm()
    w_nsa = _bf(jnp.concatenate([col(16)[:, operm], col(9), col(10), col(11), col(13)], axis=1))
    gates_t = jnp.pad(col(15).T, ((0, NSA_GATE_ROWS - 3 * NSA_HEADS), (0, 0)))
    w_nsa_t = _bf(jnp.concatenate([col(8).T, col(12).T, col(14).T, gates_t], axis=0))
    w_merge = _bf(col(17))
    lane_ids = jnp.arange(LANES)
    pick = lambda vec: jnp.where((lane_ids >= GDN_HEADS) & (lane_ids < 2 * GDN_HEADS),
                                 jnp.pad(vec, (GDN_HEADS, LANES - 2 * GDN_HEADS)), 0.0)[None, :]
    alog_row = pick(a_log.astype(F32))
    dtb_row = pick(dt_bias.astype(F32))
    alog_lanes = jnp.repeat(a_log.astype(F32), GDN_CHUNK)[None, :]
    dtb_lanes = jnp.repeat(dt_bias.astype(F32), GDN_CHUNK)[None, :]
    half = CMP_LEN // 2

    def big_w1(w1):
        w = w1.reshape(2, half, 1, NSA_DK, 1, CMP_HIDDEN)
        eye = jnp.eye(NSA_GROUPS, dtype=F32).reshape(1, 1, NSA_GROUPS, 1, NSA_GROUPS, 1)
        return _bf((w * eye).reshape(2, half * NSA_GROUPS * NSA_DK, NSA_GROUPS * CMP_HIDDEN))

    def pe_rows(pe):
        t = jnp.broadcast_to(pe.reshape(2, half, 1, NSA_DK), (2, half, NSA_GROUPS, NSA_DK))
        return t.reshape(2, half * NSA_GROUPS * NSA_DK).astype(F32)

    return dict(w_ret=w_ret, w_gdn=w_gdn, w_nsa=w_nsa, w_nsa_t=w_nsa_t, w_merge=w_merge,
                alog_row=alog_row, dtb_row=dtb_row, alog_lanes=alog_lanes, dtb_lanes=dtb_lanes,
                norm_w=norm_w.astype(F32)[None, :], conv_w=conv_w.astype(F32),
                pek=pe_rows(pe_k), pev=pe_rows(pe_v), w1k=big_w1(w1k), w1v=big_w1(w1v), w2k=_bf(w2k),
                w2vt=_bf(w2v.T), w_br_c=_bf(w_br_c[operm, :]))


def _layer(x, p_all, layer, lw, w_br_a, w_br_b, w_out, ln_g, ln_b, w_ple, w_ple_gate, ret_tabs, nsa_ovl, alpha):
    b, s, d = x.shape
    m = b * s
    x2d = x.reshape(m, d)
    r3 = lambda a: a.reshape(b, s, a.shape[-1])

    o_a = _retention(x, lw["w_ret"], ret_tabs)

    qkv, gz, sm, smt = _proj_gdn(x2d, lw["w_gdn"])
    nchunk = s // GDN_CHUNK
    smt = smt.reshape(2, GDN_HEADS, b, nchunk, GDN_CHUNK).transpose(2, 3, 0, 1, 4)
    smt = jnp.pad(smt.reshape(b * nchunk, 2, GDN_HEADS * GDN_CHUNK), ((0, 0), (0, 6), (0, 0)))
    o_b = _gdn(qkv, gz, sm, smt, lw["conv_w"], lw["alog_row"], lw["dtb_row"], lw["alog_lanes"],
               lw["dtb_lanes"], lw["norm_w"], s)

    nz, kc, vc, ks, kw, xt = _proj_nsa(x2d, lw["w_nsa"], lw["w_nsa_t"])
    slab = lambda a: a.reshape(b, s // CMP_STRIDE, CMP_STRIDE * NSA_KW)
    kcb, vcbt = _compress(slab(kc), slab(vc), lw["pek"], lw["pev"], lw["w1k"], lw["w1v"], lw["w2k"], lw["w2vt"])
    o_c = _nsa_attention(xt, r3(nz), kcb, vcbt, r3(ks), r3(kw), nsa_ovl, b, s)

    out = _merge(x2d, p_all, layer, o_a.reshape(m, -1), o_b.reshape(m, -1), o_c.reshape(m, -1),
                 lw["w_merge"], _bf(w_br_a), _bf(w_br_b), lw["w_br_c"], _bf(w_out), _bf(w_ple_gate), _bf(w_ple),
                 ln_g.astype(F32)[None, :], ln_b.astype(F32)[None, :], alpha)
    return out.reshape(b, s, d)


def kernel(x, p, w_in, gdn_conv_w, gdn_a_log, gdn_dt_bias, gdn_norm_w, nsa_pe_k, nsa_pe_v, nsa_cmp_w1k, nsa_cmp_w2k, nsa_cmp_w1v, nsa_cmp_w2v, w_branch_a, w_branch_b, w_branch_c, w_out, ln_g, ln_b, w_ple, w_ple_gate):
    depth = w_in.shape[0]
    s, d = x.shape[1], x.shape[2]
    alpha = (2 * depth) ** 0.25
    ret_tabs = _ret_tables(s)
    nsa_ovl = _nsa_tables(s)
    p_all = p.reshape(depth, -1, p.shape[-1])
    for i in range(depth):
        lw = _layer_weights(w_in[i], gdn_conv_w[i], gdn_a_log[i], gdn_dt_bias[i], gdn_norm_w[i], nsa_pe_k[i],
                            nsa_pe_v[i], nsa_cmp_w1k[i], nsa_cmp_w2k[i], nsa_cmp_w1v[i], nsa_cmp_w2v[i],
                            w_branch_c[i], d)
        x = _layer(x, p_all, i, lw, w_branch_a[i], w_branch_b[i], w_out[i], ln_g[i], ln_b[i],
                   w_ple[i], w_ple_gate[i], ret_tabs, nsa_ovl, alpha)
    return x
```
